```python
import math
import jax, jax.numpy as jnp
from jax import lax
import numpy as np

D_MODEL = 1024
BATCH = 16
SEQ = 2048
DEPTH = 2

CHUNK = 64
N_META = 16
N_MIXERS = 2
D_FF = 4 * D_MODEL
EPS = 1e-6

M_HEADS = 8
M_DQK = D_MODEL // (2 * M_HEADS)
M_DV = D_MODEL // M_HEADS
M_QK = M_HEADS * M_DQK
M_V = M_HEADS * M_DV
M_IN = 2 * M_QK + 2 * M_V + 2 * M_HEADS
M_CHUNK = 64

A_HEADS = 8
A_DH = D_MODEL // (2 * A_HEADS)
A_DV = 2 * A_DH
Q_BLOCK = 128
SUBLN_EPS = 1e-5

N_MLSTM = (DEPTH + 1) // 2
N_DIFF = DEPTH // 2

kernel_name = "hybrid_mlstm_diffattn_stream_encoder"


def rmsnorm(x, g, eps=EPS):
    xf = x.astype(jnp.float32)
    y = xf * lax.rsqrt(jnp.mean(xf * xf, axis=-1, keepdims=True) + eps)
    return (y * g.astype(jnp.float32)).astype(x.dtype)


def chunk_end(p):
    if p < N_META:
        return N_META
    return N_META + ((p - N_META) // CHUNK + 1) * CHUNK


def mlstm_mixer(h, w_in, b_gate, head_gain, w_out):
    B, L, _ = h.shape
    f32 = jnp.float32
    proj = (h @ w_in).astype(f32)
    q, k, v, o, gates = jnp.split(proj, [M_QK, 2 * M_QK, 2 * M_QK + M_V, 2 * M_QK + 2 * M_V], axis=-1)
    gates = gates + b_gate.astype(f32)
    ig, fg = gates[..., :M_HEADS], gates[..., M_HEADS:]

    pad = (-L) % M_CHUNK
    Lp = L + pad
    nc = Lp // M_CHUNK

    def to_chunks(t, d):
        t = jnp.pad(t, ((0, 0), (0, pad), (0, 0)))
        return t.reshape(B, nc, M_CHUNK, M_HEADS, d).transpose(0, 3, 1, 2, 4)

    def gate_chunks(g):
        g = jnp.pad(g, ((0, 0), (0, pad), (0, 0)))
        return g.reshape(B, nc, M_CHUNK, M_HEADS).transpose(0, 3, 1, 2)

    qc = to_chunks(q, M_DQK) * (M_DQK ** -0.5)
    kc = to_chunks(k, M_DQK)
    vc = to_chunks(v, M_DV)
    igc = gate_chunks(ig)
    lfc = gate_chunks(jax.nn.log_sigmoid(fg))

    b = jnp.cumsum(lfc, axis=-1)
    b_end = b[..., -1]
    a = b_end[..., None] - b + igc

    def step(carry, inp):
        C, n, m = carry
        k_s, v_s, a_s, be = inp
        m_new = jnp.maximum(be + m, jnp.max(a_s, axis=-1))
        decay = jnp.exp(be + m - m_new)
        w = jnp.exp(a_s - m_new[..., None])
        C_new = decay[..., None, None] * C + jnp.einsum('bhs,bhsd,bhse->bhde', w, k_s, v_s)
        n_new = decay[..., None] * n + jnp.einsum('bhs,bhsd->bhd', w, k_s)
        return (C_new, n_new, m_new), (C, n, m)

    init = (jnp.zeros((B, M_HEADS, M_DQK, M_DV), f32),
            jnp.zeros((B, M_HEADS, M_DQK), f32),
            jnp.zeros((B, M_HEADS), f32))
    xs = (kc.transpose(2, 0, 1, 3, 4), vc.transpose(2, 0, 1, 3, 4),
          a.transpose(2, 0, 1, 3), b_end.transpose(2, 0, 1))
    _, (C_prev, n_prev, m_prev) = lax.scan(step, init, xs)
    C_prev = C_prev.transpose(1, 2, 0, 3, 4)
    n_prev = n_prev.transpose(1, 2, 0, 3)
    m_prev = m_prev.transpose(1, 2, 0)

    causal = np.tril(np.ones((M_CHUNK, M_CHUNK), dtype=bool))
    log_d = jnp.where(causal, b[..., :, None] - b[..., None, :] + igc[..., None, :], -jnp.inf)
    m_inter = b + m_prev[..., None]
    m_t = jnp.maximum(m_inter, jnp.max(log_d, axis=-1))
    d_mat = jnp.exp(log_d - m_t[..., None])
    s = jnp.einsum('bhctd,bhcsd->bhcts', qc, kc) * d_mat
    inter = jnp.exp(m_inter - m_t)
    num = (inter[..., None] * jnp.einsum('bhctd,bhcde->bhcte', qc, C_prev)
           + jnp.einsum('bhcts,bhcse->bhcte', s, vc))
    den = inter * jnp.einsum('bhctd,bhcd->bhct', qc, n_prev) + jnp.sum(s, axis=-1)
    hh = num / jnp.maximum(jnp.abs(den), jnp.exp(-m_t))[..., None]

    hh = hh.transpose(0, 2, 3, 1, 4).reshape(B, Lp, M_HEADS, M_DV)[:, :L]
    hh = hh * lax.rsqrt(jnp.mean(hh * hh, axis=-1, keepdims=True) + EPS)
    hh = hh * head_gain.astype(f32).reshape(M_HEADS, M_DV)
    out = (jax.nn.sigmoid(o) * hh.reshape(B, L, M_V)).astype(h.dtype)
    return out @ w_out


def diff_attention(h, w_in, lam_vecs, head_gain, w_out, layer_idx):
    B, L, _ = h.shape
    f32 = jnp.float32
    lambda_init = 0.8 - 0.6 * math.exp(-0.3 * layer_idx)
    proj = h @ w_in
    q, k, v = jnp.split(proj, 3, axis=-1)
    q = (q.astype(f32) * (A_DH ** -0.5)).reshape(B, L, A_HEADS, 2, A_DH)
    k = k.astype(f32).reshape(B, L, A_HEADS, 2, A_DH)
    v = v.astype(f32).reshape(B, L, A_HEADS, A_DV)
    lv = lam_vecs.astype(f32)
    lam = jnp.exp(jnp.sum(lv[0] * lv[1])) - jnp.exp(jnp.sum(lv[2] * lv[3])) + lambda_init

    pos = np.arange(L)
    cid = np.where(pos < N_META, 0, 1 + (pos - N_META) // CHUNK)
    outs = []
    for s0 in range(0, L, Q_BLOCK):
        e0 = min(s0 + Q_BLOCK, L)
        ke = min(L, chunk_end(e0 - 1))
        mask = cid[None, :ke] <= cid[s0:e0, None]
        sc = jnp.einsum('bqhcd,bkhcd->bhcqk', q[:, s0:e0], k[:, :ke])
        p = jax.nn.softmax(jnp.where(mask, sc, -jnp.inf), axis=-1)
        attn = p[:, :, 0] - lam * p[:, :, 1]
        outs.append(jnp.einsum('bhqk,bkhe->bqhe', attn, v[:, :ke]))
    o = jnp.concatenate(outs, axis=1)
    o = o * lax.rsqrt(jnp.mean(o * o, axis=-1, keepdims=True) + SUBLN_EPS)
    o = o * head_gain.astype(f32).reshape(A_HEADS, A_DV) * (1.0 - lambda_init)
    return o.reshape(B, L, A_HEADS * A_DV).astype(h.dtype) @ w_out


def sq_relu_mlp(y, w_up, w_down):
    u = y @ w_up
    return jnp.square(jax.nn.relu(u)) @ w_down


def setup_inputs(seed: int = 0) -> dict:
    key = jax.random.key(seed)
    ks = jax.random.split(key, 16)
    nrm = jax.random.normal
    f32 = jnp.float32
    x = nrm(ks[0], (BATCH, SEQ, D_MODEL), f32)
    meta_tokens = nrm(ks[1], (N_META, D_MODEL), f32)
    norm_gains = 1.0 + 0.02 * nrm(ks[2], (DEPTH, 2, D_MODEL), f32)
    mlstm_w_in = nrm(ks[3], (N_MLSTM, D_MODEL, M_IN), f32) * D_MODEL ** -0.5
    ig_bias = 0.1 * nrm(ks[4], (N_MLSTM, M_HEADS), f32)
    fg_bias = jnp.linspace(3.0, 6.0, M_HEADS, dtype=f32)[None, :] + 0.1 * nrm(ks[5], (N_MLSTM, M_HEADS), f32)
    mlstm_b_gate = jnp.concatenate([ig_bias, fg_bias], axis=-1)
    mlstm_head_gain = 1.0 + 0.02 * nrm(ks[6], (N_MLSTM, M_V), f32)
    mlstm_w_out = nrm(ks[7], (N_MLSTM, M_V, D_MODEL), f32) * M_V ** -0.5
    diff_w_in = nrm(ks[8], (N_DIFF, D_MODEL, 3 * D_MODEL), f32) * D_MODEL ** -0.5
    diff_lambda = 0.1 * nrm(ks[9], (N_DIFF, 4, A_DH), f32)
    diff_head_gain = 1.0 + 0.02 * nrm(ks[10], (N_DIFF, A_HEADS * A_DV), f32)
    diff_w_out = nrm(ks[11], (N_DIFF, A_HEADS * A_DV, D_MODEL), f32) * (A_HEADS * A_DV) ** -0.5
    mlp_w_up = nrm(ks[12], (DEPTH, D_MODEL, D_FF), f32) * D_MODEL ** -0.5
    mlp_w_down = nrm(ks[13], (DEPTH, D_FF, D_MODEL), f32) * D_FF ** -0.5
    final_gain = 1.0 + 0.02 * nrm(ks[14], (D_MODEL,), f32)
    return {"x": x, "meta_tokens": meta_tokens, "norm_gains": norm_gains,
            "mlstm_w_in": mlstm_w_in, "mlstm_b_gate": mlstm_b_gate,
            "mlstm_head_gain": mlstm_head_gain, "mlstm_w_out": mlstm_w_out,
            "diff_w_in": diff_w_in, "diff_lambda": diff_lambda,
            "diff_head_gain": diff_head_gain, "diff_w_out": diff_w_out,
            "mlp_w_up": mlp_w_up, "mlp_w_down": mlp_w_down, "final_gain": final_gain}


def reference(x, meta_tokens, norm_gains, mlstm_w_in, mlstm_b_gate, mlstm_head_gain, mlstm_w_out,
              diff_w_in, diff_lambda, diff_head_gain, diff_w_out, mlp_w_up, mlp_w_down, final_gain):
    B = x.shape[0]
    meta = jnp.broadcast_to(meta_tokens.astype(x.dtype)[None], (B, N_META, D_MODEL))
    h = jnp.concatenate([meta, x], axis=1)
    for i in range(DEPTH):
        j = i // N_MIXERS
        y = rmsnorm(h, norm_gains[i, 0])
        if i % N_MIXERS == 0:
            y = mlstm_mixer(y, mlstm_w_in[j], mlstm_b_gate[j], mlstm_head_gain[j], mlstm_w_out[j])
        else:
            y = diff_attention(y, diff_w_in[j], diff_lambda[j], diff_head_gain[j], diff_w_out[j], i)
        h = h + y
        h = h + sq_relu_mlp(rmsnorm(h, norm_gains[i, 1]), mlp_w_up[i], mlp_w_down[i])
    h = rmsnorm(h, final_gain)
    return h[:, N_META:]
```

```python
import functools
import math

import jax
import jax.numpy as jnp
from jax import lax
from jax.experimental import pallas as pl
from jax.experimental.pallas import tpu as pltpu

D_MODEL = 1024
D_FF = 4 * D_MODEL
N_META = 16
CHUNK = 64
N_HEADS = 8
DQK = 64
DV = 128
EPS = 1e-6
SUBLN_EPS = 1e-5

FRONT_PAD = CHUNK - N_META
LANES = 128
GATE_COLS = LANES

VMEM_LIMIT = 56 * 1024 * 1024

F32 = jnp.float32
BF16 = jnp.bfloat16


def _compiler_params(semantics):
    return pltpu.CompilerParams(dimension_semantics=semantics, vmem_limit_bytes=VMEM_LIMIT)


def _resident(shape):
    return pl.BlockSpec(shape, lambda *_: (0,) * len(shape), pipeline_mode=pl.Buffered(1))


def _norm_proj_kernel(x_ref, g_ref, w_ref, *out_refs, splits, scales):
    x = x_ref[...]
    ms = jnp.mean(x * x, axis=-1, keepdims=True)
    xn = (x * lax.rsqrt(ms + EPS) * g_ref[...]).astype(BF16)
    off = 0
    for o_ref, n, sc in zip(out_refs, splits, scales):
        r = jnp.dot(xn, w_ref[:, off:off + n], preferred_element_type=F32)
        if sc != 1.0:
            r = r * sc
        o_ref[...] = r.astype(o_ref.dtype)
        off += n


def _norm_proj(h, gain, w, splits, scales, dtypes, tm, name):
    t = h.shape[0]
    n_total = sum(splits)
    assert w.shape == (D_MODEL, n_total) and t % tm == 0
    return pl.pallas_call(
        functools.partial(_norm_proj_kernel, splits=splits, scales=scales),
        grid=(t // tm,),
        in_specs=[pl.BlockSpec((tm, D_MODEL), lambda i: (i, 0)),
                  _resident((1, D_MODEL)),
                  _resident((D_MODEL, n_total))],
        out_specs=[pl.BlockSpec((tm, n), lambda i: (i, 0)) for n in splits],
        out_shape=[jax.ShapeDtypeStruct((t, n), dt) for n, dt in zip(splits, dtypes)],
        compiler_params=_compiler_params(("parallel",)),
        name=name,
    )(h, gain, w)


def _mlstm_kernel(q_ref, k_ref, v_ref, o_ref, gates_ref, bias_ref, gain_ref, y_ref, c_ref, m_ref,
                  *, chunks_per_step):
    @pl.when(pl.program_id(1) == 0)
    def _():
        c_ref[...] = jnp.zeros_like(c_ref)
        m_ref[...] = jnp.zeros_like(m_ref)

    row = lax.broadcasted_iota(jnp.int32, (CHUNK, CHUNK), 0)
    col = lax.broadcasted_iota(jnp.int32, (CHUNK, CHUNK), 1)
    causal = col <= row
    tril = causal.astype(F32)
    lane = lax.broadcasted_iota(jnp.int32, (CHUNK, LANES), 1)
    low_half = lane < DQK
    ones_col = (lane == 0).astype(BF16)
    bias = bias_ref[...]

    def chunk_body(c, carry):
        rows = pl.ds(pl.multiple_of(c * CHUNK, CHUNK), CHUNK)
        g = gates_ref[rows, :] + bias
        lf = jax.nn.log_sigmoid(g)
        bc = jnp.dot(tril, lf, precision=lax.Precision.HIGHEST, preferred_element_type=F32)
        g_t = g.T
        bc_t = bc.T

        def gate_terms(h):
            b_col = bc[:, N_HEADS + h:N_HEADS + h + 1]
            ig_col = g[:, h:h + 1]
            b_row = bc_t[N_HEADS + h:N_HEADS + h + 1, :]
            ig_row = g_t[h:h + 1, :]
            m_prev = m_ref[h:h + 1, 0:1]
            log_d = jnp.where(causal, b_col - b_row + ig_row, -jnp.inf)
            m_inter = b_col + m_prev
            m_t = jnp.maximum(m_inter, jnp.max(log_d, axis=-1, keepdims=True))
            d_mat = jnp.exp(log_d - m_t)
            inter = jnp.exp(m_inter - m_t)
            b_end = b_row[:, CHUNK - 1:CHUNK]
            a_col = b_end - b_col + ig_col
            m_new = jnp.maximum(b_end + m_prev, jnp.max(a_col, axis=0, keepdims=True))
            decay = jnp.exp(b_end + m_prev - m_new)
            w_col = jnp.exp(a_col - m_new)
            return d_mat, inter, m_t, decay, w_col, m_new

        for p in range(N_HEADS // 2):
            pair_cols = slice(p * LANES, (p + 1) * LANES)
            q2 = q_ref[rows, pair_cols]
            k2 = k_ref[rows, pair_cols]
            zero = jnp.zeros_like(q2)
            qs = jnp.concatenate([jnp.where(low_half, q2, zero), jnp.where(low_half, zero, q2)], axis=0)
            s = lax.dot_general(qs, k2, (((1,), (1,)), ((), ())), preferred_element_type=F32)
            c_pair = c_ref[p]
            inter_num = jnp.dot(qs, c_pair.astype(BF16), preferred_element_type=F32)

            terms = [gate_terms(2 * p), gate_terms(2 * p + 1)]
            w_lanes = jnp.where(low_half, terms[0][4], terms[1][4])
            kw_t = (k2.astype(F32) * w_lanes).T.astype(BF16)
            updates, decays = [], []
            for i in range(2):
                h = 2 * p + i
                d_mat, inter, m_t, decay, _, m_new = terms[i]
                head_cols = slice(h * DV, (h + 1) * DV)
                v_aug = jnp.concatenate([v_ref[rows, head_cols], ones_col], axis=1)
                s_h = (s[i * CHUNK:(i + 1) * CHUNK] * d_mat).astype(BF16)
                num = (inter * inter_num[i * CHUNK:(i + 1) * CHUNK]
                       + jnp.dot(s_h, v_aug, preferred_element_type=F32))
                den = num[:, DV:DV + 1]
                hh = num[:, :DV] / jnp.maximum(jnp.abs(den), jnp.exp(-m_t))
                hh = hh * lax.rsqrt(jnp.mean(hh * hh, axis=-1, keepdims=True) + EPS)
                hh = hh * gain_ref[:, head_cols]
                out = jax.nn.sigmoid(o_ref[rows, head_cols].astype(F32)) * hh
                y_ref[rows, head_cols] = out.astype(y_ref.dtype)
                updates.append(jnp.dot(kw_t[i * DQK:(i + 1) * DQK], v_aug, preferred_element_type=F32))
                decays.append(jnp.broadcast_to(decay, (DQK, 1)))
                m_ref[h:h + 1, :] = jnp.broadcast_to(m_new, (1, LANES))
            c_ref[p] = jnp.concatenate(decays, axis=0) * c_pair + jnp.concatenate(updates, axis=0)
        return carry

    lax.fori_loop(0, chunks_per_step, chunk_body, 0)


def _mlstm_core(q, k, v, o, gates, bias, gain, batch, lp, chunks_per_step):
    rows = chunks_per_step * CHUNK
    steps = lp // rows
    assert steps * rows == lp

    def tok(width):
        return pl.BlockSpec((rows, width), lambda b, j: (b * steps + j, 0))

    return pl.pallas_call(
        functools.partial(_mlstm_kernel, chunks_per_step=chunks_per_step),
        grid=(batch, steps),
        in_specs=[tok(N_HEADS * DQK), tok(N_HEADS * DQK), tok(N_HEADS * DV), tok(N_HEADS * DV),
                  tok(GATE_COLS), _resident((1, GATE_COLS)), _resident((1, N_HEADS * DV))],
        out_specs=tok(N_HEADS * DV),
        out_shape=jax.ShapeDtypeStruct((batch * lp, N_HEADS * DV), BF16),
        scratch_shapes=[pltpu.VMEM((N_HEADS // 2, 2 * DQK, 2 * DV), F32),
                        pltpu.VMEM((N_HEADS, LANES), F32)],
        compiler_params=_compiler_params(("parallel", "arbitrary")),
        name="mlstm_core",
    )(q, k, v, o, gates, bias, gain)


def _diff_attn_kernel(lam_ref, q_ref, k_ref, v_ref, gain_ref, o_ref, m_ref, l_ref, acc_ref,
                      *, tq, lambda_init):
    qi = pl.program_id(1)
    lv = lam_ref[...]
    lam = (jnp.exp(jnp.sum(lv[0:1] * lv[1:2], axis=-1, keepdims=True))
           - jnp.exp(jnp.sum(lv[2:3] * lv[3:4], axis=-1, keepdims=True)) + lambda_init)

    lane = lax.broadcasted_iota(jnp.int32, (tq, LANES), 1)
    low_half = lane < DQK
    chunk_shift = CHUNK.bit_length() - 1
    q_chunk = lax.shift_right_logical(qi * tq + lax.broadcasted_iota(jnp.int32, (tq, tq), 0), chunk_shift)
    q_chunk = jnp.concatenate([q_chunk, q_chunk], axis=0)
    k_off = lax.broadcasted_iota(jnp.int32, (2 * tq, tq), 1)

    for h in range(N_HEADS):
        cols = slice(h * DV, (h + 1) * DV)
        q2 = q_ref[:, cols]
        zero = jnp.zeros_like(q2)
        qs = jnp.concatenate([jnp.where(low_half, q2, zero), jnp.where(low_half, zero, q2)], axis=0)

        def step(j, masked):
            rows = pl.ds(pl.multiple_of(j * tq, CHUNK), tq)
            s = lax.dot_general(qs, k_ref[rows, cols], (((1,), (1,)), ((), ())),
                                preferred_element_type=F32)
            if masked:
                k_pos = j * tq + k_off
                visible = (lax.shift_right_logical(k_pos, chunk_shift) <= q_chunk) & (k_pos >= FRONT_PAD)
                s = jnp.where(visible, s, -jnp.inf)
            m_old = m_ref[...]
            m_new = jnp.maximum(m_old, jnp.max(s, axis=-1, keepdims=True))
            alpha = jnp.exp(m_old - m_new)
            p = jnp.exp(s - m_new)
            l_ref[...] = alpha * l_ref[...] + jnp.sum(p, axis=-1, keepdims=True)
            acc_ref[...] = alpha * acc_ref[...] + jnp.dot(p.astype(BF16), v_ref[rows, cols],
                                                          preferred_element_type=F32)
            m_ref[...] = m_new

        m_ref[...] = jnp.full_like(m_ref, -jnp.inf)
        l_ref[...] = jnp.zeros_like(l_ref)
        acc_ref[...] = jnp.zeros_like(acc_ref)

        step(0, True)

        def full_tile(j, carry):
            step(j, False)
            return carry

        lax.fori_loop(1, qi, full_tile, 0)

        @pl.when(qi > 0)
        def _():
            step(qi, True)

        o_both = acc_ref[...] / l_ref[...]
        o = o_both[:tq] - lam * o_both[tq:]
        o = o * lax.rsqrt(jnp.mean(o * o, axis=-1, keepdims=True) + SUBLN_EPS)
        o = o * gain_ref[:, cols] * (1.0 - lambda_init)
        o_ref[:, cols] = o.astype(o_ref.dtype)


def _diff_attention(q, k, v, lam_vecs, gain, batch, lp, tq, lambda_init):
    nq = lp // tq
    assert nq * tq == lp and tq % CHUNK == 0
    width = N_HEADS * DV
    return pl.pallas_call(
        functools.partial(_diff_attn_kernel, tq=tq, lambda_init=lambda_init),
        grid=(batch, nq),
        in_specs=[_resident(lam_vecs.shape),
                  pl.BlockSpec((tq, width), lambda b, i: (b * nq + i, 0)),
                  pl.BlockSpec((lp, width), lambda b, i: (b, 0)),
                  pl.BlockSpec((lp, width), lambda b, i: (b, 0)),
                  _resident((1, width))],
        out_specs=pl.BlockSpec((tq, width), lambda b, i: (b * nq + i, 0)),
        out_shape=jax.ShapeDtypeStruct((batch * lp, width), BF16),
        scratch_shapes=[pltpu.VMEM((2 * tq, 1), F32), pltpu.VMEM((2 * tq, 1), F32),
                        pltpu.VMEM((2 * tq, DV), F32)],
        compiler_params=_compiler_params(("parallel", "arbitrary")),
        name="diff_attention",
    )(lam_vecs, q, k, v, gain)


def _post_mlp_kernel(h_ref, y_ref, wo_ref, g_ref, wu_ref, wd_ref, fg_ref, out_ref, *, final_norm, ff_block):
    h1 = h_ref[...] + jnp.dot(y_ref[...], wo_ref[...], preferred_element_type=F32)
    ms = jnp.mean(h1 * h1, axis=-1, keepdims=True)
    hn = (h1 * lax.rsqrt(ms + EPS) * g_ref[...]).astype(BF16)
    acc = h1
    for c in range(D_FF // ff_block):
        blk = slice(c * ff_block, (c + 1) * ff_block)
        u = jnp.dot(hn, wu_ref[:, blk], preferred_element_type=F32)
        a = jnp.square(jnp.maximum(u, 0.0)).astype(BF16)
        acc = acc + jnp.dot(a, wd_ref[blk, :], preferred_element_type=F32)
    if final_norm:
        ms = jnp.mean(acc * acc, axis=-1, keepdims=True)
        acc = acc * lax.rsqrt(ms + EPS) * fg_ref[...]
    out_ref[...] = acc


def _post_mlp(h, y, w_out, gain, w_up, w_down, final_gain, final_norm, tm, name):
    t = h.shape[0]
    assert t % tm == 0
    return pl.pallas_call(
        functools.partial(_post_mlp_kernel, final_norm=final_norm, ff_block=1024),
        grid=(t // tm,),
        in_specs=[pl.BlockSpec((tm, D_MODEL), lambda i: (i, 0)),
                  pl.BlockSpec((tm, D_MODEL), lambda i: (i, 0)),
                  _resident((D_MODEL, D_MODEL)),
                  _resident((1, D_MODEL)),
                  _resident((D_MODEL, D_FF)),
                  _resident((D_FF, D_MODEL)),
                  _resident((1, D_MODEL))],
        out_specs=pl.BlockSpec((tm, D_MODEL), lambda i: (i, 0)),
        out_shape=jax.ShapeDtypeStruct((t, D_MODEL), F32),
        compiler_params=_compiler_params(("parallel",)),
        name=name,
    )(h, y, w_out, gain, w_up, w_down, final_gain)


def kernel(x, meta_tokens, norm_gains, mlstm_w_in, mlstm_b_gate, mlstm_head_gain, mlstm_w_out,
           diff_w_in, diff_lambda, diff_head_gain, diff_w_out, mlp_w_up, mlp_w_down, final_gain):
    batch, seq, _ = x.shape
    lp = FRONT_PAD + N_META + seq
    t = batch * lp

    meta = jnp.broadcast_to(meta_tokens.astype(x.dtype)[None], (batch, N_META, D_MODEL))
    h = jnp.concatenate([jnp.zeros((batch, FRONT_PAD, D_MODEL), x.dtype), meta, x], axis=1)
    h = h.reshape(t, D_MODEL)

    m_qk = N_HEADS * DQK
    m_v = N_HEADS * DV
    w0 = mlstm_w_in[0]
    w0 = jnp.concatenate([w0[:, :2 * m_qk + 2 * m_v],
                          jnp.pad(w0[:, 2 * m_qk + 2 * m_v:], ((0, 0), (0, GATE_COLS - 2 * N_HEADS)))],
                         axis=1).astype(BF16)
    q, k, v, o, gates = _norm_proj(
        h, norm_gains[0, 0][None], w0,
        splits=(m_qk, m_qk, m_v, m_v, GATE_COLS), scales=(DQK ** -0.5, 1.0, 1.0, 1.0, 1.0),
        dtypes=(BF16, BF16, BF16, BF16, F32), tm=512, name="mlstm_in_proj")
    bias = jnp.pad(mlstm_b_gate[0], (0, GATE_COLS - 2 * N_HEADS))[None]
    y = _mlstm_core(q, k, v, o, gates, bias, mlstm_head_gain[0][None], batch, lp, chunks_per_step=11)
    h = _post_mlp(h, y, mlstm_w_out[0].astype(BF16), norm_gains[0, 1][None],
                  mlp_w_up[0].astype(BF16), mlp_w_down[0].astype(BF16), final_gain[None],
                  final_norm=False, tm=512, name="mlstm_out_mlp")

    lambda_init = 0.8 - 0.6 * math.exp(-0.3 * 1)
    q, k, v = _norm_proj(
        h, norm_gains[1, 0][None], diff_w_in[0].astype(BF16),
        splits=(D_MODEL, D_MODEL, D_MODEL), scales=(DQK ** -0.5, 1.0, 1.0),
        dtypes=(BF16, BF16, BF16), tm=512, name="diff_in_proj")
    y = _diff_attention(q, k, v, diff_lambda[0], diff_head_gain[0][None], batch, lp, tq=192,
                        lambda_init=lambda_init)
    h = _post_mlp(h, y, diff_w_out[0].astype(BF16), norm_gains[1, 1][None],
                  mlp_w_up[1].astype(BF16), mlp_w_down[1].astype(BF16), final_gain[None],
                  final_norm=True, tm=512, name="diff_out_mlp")

    return h.reshape(batch, lp, D_MODEL)[:, FRONT_PAD + N_META:]
```

```python
import functools
import math

import jax
import jax.numpy as jnp
from jax import lax
from jax.experimental import pallas as pl
from jax.experimental.pallas import tpu as pltpu

D_MODEL = 1024
D_FF = 4 * D_MODEL
N_META = 16
CHUNK = 64
N_HEADS = 8
DQK = 64
DV = 128
EPS = 1e-6
SUBLN_EPS = 1e-5

META_ROWS = CHUNK
META_PAD = META_ROWS - N_META
LANES = 128
GATE_COLS = LANES

ATT_TQ = 256
ATT_TK = 256

VMEM_LIMIT = 56 * 1024 * 1024

F32 = jnp.float32
BF16 = jnp.bfloat16
NT_DIMS = (((1,), (1,)), ((), ()))


def _compiler_params(semantics):
    return pltpu.CompilerParams(dimension_semantics=semantics, vmem_limit_bytes=VMEM_LIMIT)


def _resident(shape):
    return pl.BlockSpec(shape, lambda *_: (0,) * len(shape), pipeline_mode=pl.Buffered(1))


def _norm_proj_kernel(x_ref, g_ref, *refs, scales, transposed):
    n_out = len(scales)
    w_refs, out_refs = refs[:n_out], refs[n_out:]
    x = x_ref[...]
    ms = jnp.mean(x * x, axis=-1, keepdims=True)
    xn = (x * lax.rsqrt(ms + EPS) * g_ref[...]).astype(BF16)
    for w_ref, o_ref, sc, tr in zip(w_refs, out_refs, scales, transposed):
        if tr:
            r = lax.dot_general(w_ref[...], xn, NT_DIMS, preferred_element_type=F32)
        else:
            r = jnp.dot(xn, w_ref[...], preferred_element_type=F32)
        if sc != 1.0:
            r = r * sc
        o_ref[...] = r.astype(o_ref.dtype)


def _norm_proj(h, gain, weights, scales, transposed, dtypes, tm, name):
    t = h.shape[0]
    assert t % tm == 0
    w_specs, out_specs, out_shapes = [], [], []
    for w, tr, dt in zip(weights, transposed, dtypes):
        w_specs.append(_resident(w.shape))
        if tr:
            n = w.shape[0]
            out_specs.append(pl.BlockSpec((n, tm), lambda i: (0, i)))
            out_shapes.append(jax.ShapeDtypeStruct((n, t), dt))
        else:
            n = w.shape[1]
            out_specs.append(pl.BlockSpec((tm, n), lambda i: (i, 0)))
            out_shapes.append(jax.ShapeDtypeStruct((t, n), dt))
    return pl.pallas_call(
        functools.partial(_norm_proj_kernel, scales=scales, transposed=transposed),
        grid=(t // tm,),
        in_specs=[pl.BlockSpec((tm, D_MODEL), lambda i: (i, 0)), _resident((1, D_MODEL))] + w_specs,
        out_specs=out_specs,
        out_shape=out_shapes,
        compiler_params=_compiler_params(("parallel",)),
        name=name,
    )(h, gain, *weights)


def _mlstm_kernel(q_ref, k_ref, v_ref, o_ref, gates_ref, bias_ref, gain_ref, c0_ref, m0_ref,
                  y_ref, c_out_ref, m_out_ref, c_ref, m_ref, *, chunks_per_step):
    @pl.when(pl.program_id(1) == 0)
    def _():
        c_ref[...] = c0_ref[...]
        m_ref[...] = m0_ref[...]

    row = lax.broadcasted_iota(jnp.int32, (CHUNK, CHUNK), 0)
    col = lax.broadcasted_iota(jnp.int32, (CHUNK, CHUNK), 1)
    causal = col <= row
    tril = causal.astype(F32)
    lane = lax.broadcasted_iota(jnp.int32, (CHUNK, LANES), 1)
    low_half = lane < DQK
    ones_col = (lane == 0).astype(BF16)
    bias = bias_ref[...]

    def chunk_body(c, carry):
        rows = pl.ds(pl.multiple_of(c * CHUNK, CHUNK), CHUNK)
        g = gates_ref[rows, :] + bias
        lf = jax.nn.log_sigmoid(g)
        bc = jnp.dot(tril, lf, precision=lax.Precision.HIGHEST, preferred_element_type=F32)
        g_t = g.T
        bc_t = bc.T

        def gate_terms(h):
            b_col = bc[:, N_HEADS + h:N_HEADS + h + 1]
            ig_col = g[:, h:h + 1]
            b_row = bc_t[N_HEADS + h:N_HEADS + h + 1, :]
            ig_row = g_t[h:h + 1, :]
            m_prev = m_ref[h:h + 1, 0:1]
            log_d = jnp.where(causal, b_col - b_row + ig_row, -jnp.inf)
            m_inter = b_col + m_prev
            m_t = jnp.maximum(m_inter, jnp.max(log_d, axis=-1, keepdims=True))
            d_mat = jnp.exp(log_d - m_t)
            inter = jnp.exp(m_inter - m_t)
            b_end = b_row[:, CHUNK - 1:CHUNK]
            a_col = b_end - b_col + ig_col
            m_new = jnp.maximum(b_end + m_prev, jnp.max(a_col, axis=0, keepdims=True))
            decay = jnp.exp(b_end + m_prev - m_new)
            w_col = jnp.exp(a_col - m_new)
            return d_mat, inter, m_t, decay, w_col, m_new

        for p in range(N_HEADS // 2):
            pair_cols = slice(p * LANES, (p + 1) * LANES)
            q2 = q_ref[rows, pair_cols]
            k2 = k_ref[rows, pair_cols]
            zero = jnp.zeros_like(q2)
            qs = jnp.concatenate([jnp.where(low_half, q2, zero), jnp.where(low_half, zero, q2)], axis=0)
            s = lax.dot_general(qs, k2, NT_DIMS, preferred_element_type=F32)
            c_pair = c_ref[p]
            inter_num = jnp.dot(qs, c_pair.astype(BF16), preferred_element_type=F32)

            terms = [gate_terms(2 * p), gate_terms(2 * p + 1)]
            w_lanes = jnp.where(low_half, terms[0][4], terms[1][4])
            kw_t = (k2.astype(F32) * w_lanes).T.astype(BF16)
            updates, decays = [], []
            for i in range(2):
                h = 2 * p + i
                d_mat, inter, m_t, decay, _, m_new = terms[i]
                head_cols = slice(h * DV, (h + 1) * DV)
                v_aug = jnp.concatenate([v_ref[rows, head_cols], ones_col], axis=1)
                s_h = (s[i * CHUNK:(i + 1) * CHUNK] * d_mat).astype(BF16)
                num = (inter * inter_num[i * CHUNK:(i + 1) * CHUNK]
                       + jnp.dot(s_h, v_aug, preferred_element_type=F32))
                den = num[:, DV:DV + 1]
                hh = num[:, :DV] / jnp.maximum(jnp.abs(den), jnp.exp(-m_t))
                hh = hh * lax.rsqrt(jnp.mean(hh * hh, axis=-1, keepdims=True) + EPS)
                hh = hh * gain_ref[:, head_cols]
                out = jax.nn.sigmoid(o_ref[rows, head_cols].astype(F32)) * hh
                y_ref[rows, head_cols] = out.astype(y_ref.dtype)
                updates.append(jnp.dot(kw_t[i * DQK:(i + 1) * DQK], v_aug, preferred_element_type=F32))
                decays.append(jnp.broadcast_to(decay, (DQK, 1)))
                m_ref[h:h + 1, :] = jnp.broadcast_to(m_new, (1, LANES))
            c_ref[p] = jnp.concatenate(decays, axis=0) * c_pair + jnp.concatenate(updates, axis=0)
        return carry

    lax.fori_loop(0, chunks_per_step, chunk_body, 0)
    c_out_ref[...] = c_ref[...]
    m_out_ref[...] = m_ref[...]


def _mlstm_core(q, k, v, o, gates, bias, gain, c0, m0, batch, chunks_per_step, name):
    rows = chunks_per_step * CHUNK
    steps = q.shape[0] // batch // rows
    assert batch * steps * rows == q.shape[0]
    c_shape = (N_HEADS // 2, 2 * DQK, 2 * DV)
    m_shape = (N_HEADS, LANES)

    def tok(width):
        return pl.BlockSpec((rows, width), lambda b, j: (b * steps + j, 0))

    return pl.pallas_call(
        functools.partial(_mlstm_kernel, chunks_per_step=chunks_per_step),
        grid=(batch, steps),
        in_specs=[tok(N_HEADS * DQK), tok(N_HEADS * DQK), tok(N_HEADS * DV), tok(N_HEADS * DV),
                  tok(GATE_COLS), _resident((1, GATE_COLS)), _resident((1, N_HEADS * DV)),
                  _resident(c_shape), _resident(m_shape)],
        out_specs=[tok(N_HEADS * DV),
                   pl.BlockSpec((None,) + c_shape, lambda b, j: (b, 0, 0, 0)),
                   pl.BlockSpec((None,) + m_shape, lambda b, j: (b, 0, 0))],
        out_shape=[jax.ShapeDtypeStruct((q.shape[0], N_HEADS * DV), BF16),
                   jax.ShapeDtypeStruct((batch,) + c_shape, F32),
                   jax.ShapeDtypeStruct((batch,) + m_shape, F32)],
        scratch_shapes=[pltpu.VMEM(c_shape, F32), pltpu.VMEM(m_shape, F32)],
        compiler_params=_compiler_params(("parallel", "arbitrary")),
        name=name,
    )(q, k, v, o, gates, bias, gain, c0, m0)


def _diff_attn_kernel(lam_ref, q_ref, k_ref, vt_ref, km_ref, vtm_ref, gain_ref, o_ref,
                      qs_ref, m_ref, l_ref, acc_ref, *, lambda_init):
    qi = pl.program_id(1)
    tq, tk = ATT_TQ, ATT_TK
    neg_inf = jnp.float32(-jnp.inf)

    lane = lax.broadcasted_iota(jnp.int32, (tq, LANES), 1)
    low_half = lane < DQK
    for h in range(N_HEADS):
        q2 = q_ref[:, h * DV:(h + 1) * DV]
        zero = jnp.zeros_like(q2)
        qs_ref[h] = jnp.concatenate([jnp.where(low_half, q2, zero), jnp.where(low_half, zero, q2)], axis=0)
    m_ref[...] = jnp.full_like(m_ref, neg_inf)
    l_ref[...] = jnp.zeros_like(l_ref)
    acc_ref[...] = jnp.zeros_like(acc_ref)

    def tile_step(get_k, get_vt, bias):
        for h in range(N_HEADS):
            s = lax.dot_general(get_k(h), qs_ref[h], NT_DIMS, preferred_element_type=F32)
            if bias is not None:
                s = s + bias
            m_old = m_ref[h]
            m_new = jnp.maximum(m_old, jnp.max(s, axis=0, keepdims=True))
            alpha = jnp.exp(m_old - m_new)
            p = jnp.exp(s - m_new)
            l_ref[h] = alpha * l_ref[h] + jnp.sum(p, axis=0, keepdims=True)
            acc_ref[h] = alpha * acc_ref[h] + jnp.dot(get_vt(h), p.astype(BF16), preferred_element_type=F32)
            m_ref[h] = m_new

    meta_row = lax.broadcasted_iota(jnp.int32, (META_ROWS, 2 * tq), 0)
    meta_bias = jnp.where(meta_row >= META_PAD, 0.0, neg_inf)
    tile_step(lambda h: km_ref[:, h * DV:(h + 1) * DV], lambda h: vtm_ref[h * DV:(h + 1) * DV, :], meta_bias)

    def full_tile(j, carry):
        start = pl.multiple_of(j * tk, tk)
        tile_step(lambda h: k_ref[pl.ds(start, tk), h * DV:(h + 1) * DV],
                  lambda h: vt_ref[h * DV:(h + 1) * DV, pl.ds(start, tk)], None)
        return carry

    lax.fori_loop(0, qi * (tq // tk), full_tile, 0)

    chunk_shift = CHUNK.bit_length() - 1
    k_chunk = lax.shift_right_logical(lax.broadcasted_iota(jnp.int32, (tk, 2 * tq), 0), chunk_shift)
    q_col = lax.broadcasted_iota(jnp.int32, (tk, 2 * tq), 1)
    q_chunk = lax.shift_right_logical(jnp.bitwise_and(q_col, tq - 1), chunk_shift)
    for d in range(tq // tk):
        diag_bias = jnp.where(k_chunk + d * (tk // CHUNK) <= q_chunk, 0.0, neg_inf)
        start = pl.multiple_of(qi * tq + d * tk, tk)
        tile_step(lambda h: k_ref[pl.ds(start, tk), h * DV:(h + 1) * DV],
                  lambda h: vt_ref[h * DV:(h + 1) * DV, pl.ds(start, tk)], diag_bias)

    lv = lam_ref[...]
    lam = (jnp.exp(jnp.sum(lv[0:1] * lv[1:2], axis=-1, keepdims=True))
           - jnp.exp(jnp.sum(lv[2:3] * lv[3:4], axis=-1, keepdims=True)) + lambda_init)
    for h in range(N_HEADS):
        cols = slice(h * DV, (h + 1) * DV)
        o_both = acc_ref[h] / l_ref[h]
        o = (o_both[:, :tq] - lam * o_both[:, tq:]).T
        o = o * lax.rsqrt(jnp.mean(o * o, axis=-1, keepdims=True) + SUBLN_EPS)
        o = o * gain_ref[:, cols] * (1.0 - lambda_init)
        o_ref[:, cols] = o.astype(o_ref.dtype)


def _diff_attention(q, k, vt, k_meta, vt_meta, lam_vecs, gain, batch, lambda_init):
    t = q.shape[0]
    seq = t // batch
    nq = seq // ATT_TQ
    assert nq * ATT_TQ == seq and ATT_TQ % ATT_TK == 0 and ATT_TK % CHUNK == 0
    width = N_HEADS * DV
    return pl.pallas_call(
        functools.partial(_diff_attn_kernel, lambda_init=lambda_init),
        grid=(batch, nq),
        in_specs=[_resident(lam_vecs.shape),
                  pl.BlockSpec((ATT_TQ, width), lambda b, i: (b * nq + i, 0)),
                  pl.BlockSpec((seq, width), lambda b, i: (b, 0)),
                  pl.BlockSpec((width, seq), lambda b, i: (0, b)),
                  _resident(k_meta.shape),
                  _resident(vt_meta.shape),
                  _resident((1, width))],
        out_specs=pl.BlockSpec((ATT_TQ, width), lambda b, i: (b * nq + i, 0)),
        out_shape=jax.ShapeDtypeStruct((t, width), BF16),
        scratch_shapes=[pltpu.VMEM((N_HEADS, 2 * ATT_TQ, DV), BF16),
                        pltpu.VMEM((N_HEADS, 1, 2 * ATT_TQ), F32),
                        pltpu.VMEM((N_HEADS, 1, 2 * ATT_TQ), F32),
                        pltpu.VMEM((N_HEADS, DV, 2 * ATT_TQ), F32)],
        compiler_params=_compiler_params(("parallel", "arbitrary")),
        name="diff_attention",
    )(lam_vecs, q, k, vt, k_meta, vt_meta, gain)


def _post_mlp_kernel(h_ref, y_ref, wo_ref, g_ref, wu_ref, wd_ref, fg_ref, out_ref, *, final_norm, ff_block):
    h1 = h_ref[...] + jnp.dot(y_ref[...], wo_ref[...], preferred_element_type=F32)
    ms = jnp.mean(h1 * h1, axis=-1, keepdims=True)
    hn = (h1 * lax.rsqrt(ms + EPS) * g_ref[...]).astype(BF16)
    acc = h1
    for c in range(D_FF // ff_block):
        blk = slice(c * ff_block, (c + 1) * ff_block)
        u = jnp.dot(hn, wu_ref[:, blk], preferred_element_type=F32)
        a = jnp.square(jnp.maximum(u, 0.0)).astype(BF16)
        acc = acc + jnp.dot(a, wd_ref[blk, :], preferred_element_type=F32)
    if final_norm:
        ms = jnp.mean(acc * acc, axis=-1, keepdims=True)
        acc = acc * lax.rsqrt(ms + EPS) * fg_ref[...]
    out_ref[...] = acc


def _post_mlp(h, y, w_out, gain, w_up, w_down, final_gain, final_norm, tm, name):
    t = h.shape[0]
    assert t % tm == 0
    return pl.pallas_call(
        functools.partial(_post_mlp_kernel, final_norm=final_norm, ff_block=1024),
        grid=(t // tm,),
        in_specs=[pl.BlockSpec((tm, D_MODEL), lambda i: (i, 0)),
                  pl.BlockSpec((tm, D_MODEL), lambda i: (i, 0)),
                  _resident((D_MODEL, D_MODEL)),
                  _resident((1, D_MODEL)),
                  _resident((D_MODEL, D_FF)),
                  _resident((D_FF, D_MODEL)),
                  _resident((1, D_MODEL))],
        out_specs=pl.BlockSpec((tm, D_MODEL), lambda i: (i, 0)),
        out_shape=jax.ShapeDtypeStruct((t, D_MODEL), F32),
        compiler_params=_compiler_params(("parallel",)),
        name=name,
    )(h, y, w_out, gain, w_up, w_down, final_gain)


def kernel(x, meta_tokens, norm_gains, mlstm_w_in, mlstm_b_gate, mlstm_head_gain, mlstm_w_out,
           diff_w_in, diff_lambda, diff_head_gain, diff_w_out, mlp_w_up, mlp_w_down, final_gain):
    batch, seq, _ = x.shape
    t = batch * seq
    tm = 512
    h = x.reshape(t, D_MODEL)
    h_meta = jnp.concatenate([jnp.zeros((META_PAD, D_MODEL), x.dtype), meta_tokens.astype(x.dtype)], axis=0)

    m_qk = N_HEADS * DQK
    m_v = N_HEADS * DV
    w0 = mlstm_w_in[0]
    offs = (0, m_qk, 2 * m_qk, 2 * m_qk + m_v, 2 * m_qk + 2 * m_v)
    w0_parts = [w0[:, offs[0]:offs[1]], w0[:, offs[1]:offs[2]], w0[:, offs[2]:offs[3]], w0[:, offs[3]:offs[4]],
                jnp.pad(w0[:, offs[4]:], ((0, 0), (0, GATE_COLS - 2 * N_HEADS)))]
    w0_parts = [w.astype(BF16) for w in w0_parts]
    in0 = dict(gain=norm_gains[0, 0][None], weights=w0_parts, scales=(DQK ** -0.5, 1.0, 1.0, 1.0, 1.0),
               transposed=(False,) * 5, dtypes=(BF16, BF16, BF16, BF16, F32))
    bias = jnp.pad(mlstm_b_gate[0], (0, GATE_COLS - 2 * N_HEADS))[None]
    head_gain0 = mlstm_head_gain[0][None]
    mlp0 = dict(w_out=mlstm_w_out[0].astype(BF16), gain=norm_gains[0, 1][None], w_up=mlp_w_up[0].astype(BF16),
                w_down=mlp_w_down[0].astype(BF16), final_gain=final_gain[None], final_norm=False)

    proj_meta = _norm_proj(h_meta, tm=META_ROWS, name="mlstm_in_proj_meta", **in0)
    c_zero = jnp.zeros((N_HEADS // 2, 2 * DQK, 2 * DV), F32)
    m_zero = jnp.zeros((N_HEADS, LANES), F32)
    y_meta, c_meta, m_meta = _mlstm_core(*proj_meta, bias, head_gain0, c_zero, m_zero, batch=1,
                                         chunks_per_step=1, name="mlstm_core_meta")
    h_meta = _post_mlp(h_meta, y_meta, tm=META_ROWS, name="mlstm_out_mlp_meta", **mlp0)

    proj = _norm_proj(h, tm=tm, name="mlstm_in_proj", **in0)
    y, _, _ = _mlstm_core(*proj, bias, head_gain0, c_meta[0], m_meta[0], batch=batch,
                          chunks_per_step=8, name="mlstm_core")
    h = _post_mlp(h, y, tm=tm, name="mlstm_out_mlp", **mlp0)

    lambda_init = 0.8 - 0.6 * math.exp(-0.3 * 1)
    w1 = diff_w_in[0]
    w1_parts = [w1[:, :D_MODEL].astype(BF16), w1[:, D_MODEL:2 * D_MODEL].astype(BF16),
                w1[:, 2 * D_MODEL:].T.astype(BF16)]
    in1 = dict(gain=norm_gains[1, 0][None], weights=w1_parts, scales=(DQK ** -0.5, 1.0, 1.0),
               transposed=(False, False, True), dtypes=(BF16, BF16, BF16))
    _, k_meta, vt_meta = _norm_proj(h_meta, tm=META_ROWS, name="diff_in_proj_meta", **in1)
    q, k, vt = _norm_proj(h, tm=tm, name="diff_in_proj", **in1)
    y = _diff_attention(q, k, vt, k_meta, vt_meta, diff_lambda[0], diff_head_gain[0][None], batch, lambda_init)
    out = _post_mlp(h, y, diff_w_out[0].astype(BF16), norm_gains[1, 1][None], mlp_w_up[1].astype(BF16),
                    mlp_w_down[1].astype(BF16), final_gain[None], final_norm=True, tm=tm, name="diff_out_mlp")
    return out.reshape(batch, seq, D_MODEL)
```

```python
import functools
import math

import jax
import jax.numpy as jnp
from jax import lax
from jax.experimental import pallas as pl
from jax.experimental.pallas import tpu as pltpu

D_MODEL = 1024
D_FF = 4 * D_MODEL
N_META = 16
CHUNK = 64
N_HEADS = 8
DQK = 64
DV = 128
EPS = 1e-6
SUBLN_EPS = 1e-5

LANES = 128
BF16_ROWS = 16
AUG_ROWS = BF16_ROWS
VAUG_ROWS = DV + AUG_ROWS
GATE_ROWS = LANES

MCHUNK = 256
META_ROWS = MCHUNK
ATT_META_ROWS = CHUNK
ATT_META_PAD = ATT_META_ROWS - N_META
ATT_TQ = 256
ATT_TK = 256

VMEM_LIMIT = 56 * 1024 * 1024

F32 = jnp.float32
BF16 = jnp.bfloat16
NT_DIMS = (((1,), (1,)), ((), ()))


def _compiler_params(semantics):
    return pltpu.CompilerParams(dimension_semantics=semantics, vmem_limit_bytes=VMEM_LIMIT)


def _resident(shape):
    return pl.BlockSpec(shape, lambda *_: (0,) * len(shape), pipeline_mode=pl.Buffered(1))


def _ones_rows(width):
    return (lax.broadcasted_iota(jnp.int32, (AUG_ROWS, width), 0) == 0).astype(BF16)


def _norm_proj_kernel(x_ref, g_ref, *refs, scales, transposed):
    n_out = len(scales)
    w_refs, out_refs = refs[:n_out], refs[n_out:]
    x = x_ref[...]
    ms = jnp.mean(x * x, axis=-1, keepdims=True)
    xn = (x * lax.rsqrt(ms + EPS) * g_ref[...]).astype(BF16)
    for w_ref, o_ref, sc, tr in zip(w_refs, out_refs, scales, transposed):
        if tr:
            r = lax.dot_general(w_ref[...], xn, NT_DIMS, preferred_element_type=F32)
        else:
            r = jnp.dot(xn, w_ref[...], preferred_element_type=F32)
        if sc != 1.0:
            r = r * sc
        o_ref[...] = r.astype(o_ref.dtype)


def _norm_proj(h, gain, weights, scales, transposed, dtypes, tm, name):
    t = h.shape[0]
    assert t % tm == 0
    w_specs, out_specs, out_shapes = [], [], []
    for w, tr, dt in zip(weights, transposed, dtypes):
        w_specs.append(_resident(w.shape))
        if tr:
            n = w.shape[0]
            out_specs.append(pl.BlockSpec((n, tm), lambda i: (0, i)))
            out_shapes.append(jax.ShapeDtypeStruct((n, t), dt))
        else:
            n = w.shape[1]
            out_specs.append(pl.BlockSpec((tm, n), lambda i: (i, 0)))
            out_shapes.append(jax.ShapeDtypeStruct((t, n), dt))
    return pl.pallas_call(
        functools.partial(_norm_proj_kernel, scales=scales, transposed=transposed),
        grid=(t // tm,),
        in_specs=[pl.BlockSpec((tm, D_MODEL), lambda i: (i, 0)), _resident((1, D_MODEL))] + w_specs,
        out_specs=out_specs,
        out_shape=out_shapes,
        compiler_params=_compiler_params(("parallel",)),
        name=name,
    )(h, gain, *weights)


def _mlstm_kernel(q_ref, k_ref, vt_ref, o_ref, gt_ref, bias_ref, gain_ref, ct0_ref, m0_ref,
                  y_ref, ct_out_ref, m_out_ref, ct_ref, m_ref, b_ref, r_ref, cm_ref, rt_ref,
                  *, chunks_per_step):
    t = MCHUNK

    @pl.when(pl.program_id(1) == 0)
    def _():
        ct_ref[...] = ct0_ref[...]
        m_ref[...] = m0_ref[...]

    neg_inf = jnp.float32(-jnp.inf)
    causal = (lax.broadcasted_iota(jnp.int32, (t, t), 0)
              <= lax.broadcasted_iota(jnp.int32, (t, t), 1))
    upper = causal.astype(BF16)
    lane_t = lax.broadcasted_iota(jnp.int32, (N_HEADS, t), 1)
    bias = bias_ref[...]

    for c in range(chunks_per_step):
        g = gt_ref[0:2 * N_HEADS, c * t:(c + 1) * t] + bias
        ig = g[0:N_HEADS]
        lf = jax.nn.log_sigmoid(g[N_HEADS:])
        hi = lf.astype(BF16).astype(F32)
        mid = (lf - hi).astype(BF16).astype(F32)
        lo = (lf - hi - mid).astype(BF16).astype(F32)
        parts = jnp.dot(jnp.concatenate([hi, mid, lo], axis=0).astype(BF16), upper,
                        preferred_element_type=F32)
        b = parts[0:N_HEADS] + parts[N_HEADS:2 * N_HEADS] + parts[2 * N_HEADS:]
        r = ig - b
        cm = r
        shift = 1
        while shift < t:
            cm = jnp.maximum(cm, jnp.where(lane_t >= shift, pltpu.roll(cm, shift, 1), neg_inf))
            shift *= 2
        b_ref[c] = b
        r_ref[c] = r
        cm_ref[c] = cm
        rt_ref[c] = r.T

    low = lax.broadcasted_iota(jnp.int32, (t, LANES), 1) < DQK
    low_state = lax.broadcasted_iota(jnp.int32, (VAUG_ROWS, LANES), 1) < DQK
    ones_rows = _ones_rows(t)

    def chunk_body(c, carry):
        rows = pl.ds(pl.multiple_of(c * t, t), t)
        b, r, cm, rt = b_ref[c], r_ref[c], cm_ref[c], rt_ref[c]
        m_prev = m_ref[:, 0:1]
        u = jnp.maximum(m_prev, cm)
        inter = jnp.exp(m_prev - u)
        e_row = jnp.exp(-(b + u))
        b_end = b[:, t - 1:t]
        a = b_end + r
        m_new = jnp.maximum(b_end + m_prev, jnp.max(a, axis=1, keepdims=True))
        decay = jnp.exp(b_end + m_prev - m_new)
        w = jnp.exp(a - m_new)
        m_ref[...] = jnp.broadcast_to(m_new, m_ref.shape)

        def pair_scores(p):
            q2 = q_ref[rows, p * LANES:(p + 1) * LANES]
            k2 = k_ref[rows, p * LANES:(p + 1) * LANES]
            zero = jnp.zeros_like(q2)
            qs = jnp.concatenate([jnp.where(low, q2, zero), jnp.where(low, zero, q2)], axis=0)
            return q2, k2, lax.dot_general(k2, qs, NT_DIMS, preferred_element_type=F32)

        nxt = pair_scores(0)
        for p in range(N_HEADS // 2):
            q2, k2, st = nxt
            heads = (2 * p, 2 * p + 1)
            cts = [ct_ref[h] for h in heads]
            vaugs = [jnp.concatenate([vt_ref[h * DV:(h + 1) * DV, rows], ones_rows], axis=0) for h in heads]
            inter_nums = [lax.dot_general(ct.astype(BF16), q2, NT_DIMS, preferred_element_type=F32)
                          for ct in cts]
            for i, h in enumerate(heads):
                wv = (vaugs[i].astype(F32) * w[h:h + 1, :]).astype(BF16)
                upd = jnp.dot(wv, k2, preferred_element_type=F32)
                own = low_state if i == 0 else jnp.logical_not(low_state)
                ct_ref[h] = decay[h:h + 1, :] * cts[i] + jnp.where(own, upd, 0.0)
            if p + 1 < N_HEADS // 2:
                nxt = pair_scores(p + 1)
            for i, h in enumerate(heads):
                cols = slice(h * DV, (h + 1) * DV)
                arg = rt[:, h:h + 1] - u[h:h + 1, :]
                dt = jnp.exp(jnp.where(causal, arg, neg_inf))
                s_h = (st[:, i * t:(i + 1) * t] * dt).astype(BF16)
                num = (inter[h:h + 1, :] * inter_nums[i]
                       + jnp.dot(vaugs[i], s_h, preferred_element_type=F32))
                inv = 1.0 / jnp.maximum(jnp.abs(num[DV:DV + 1, :]), e_row[h:h + 1, :])
                hh = num[0:DV, :] * inv
                hh = hh * lax.rsqrt(jnp.mean(hh * hh, axis=0, keepdims=True) + EPS)
                out = hh.T * gain_ref[:, cols] * jax.nn.sigmoid(o_ref[rows, cols].astype(F32))
                y_ref[rows, cols] = out.astype(y_ref.dtype)
        return carry

    lax.fori_loop(0, chunks_per_step, chunk_body, 0)
    ct_out_ref[...] = ct_ref[...]
    m_out_ref[...] = m_ref[...]


def _mlstm_core(q, k, vt, o, gates_t, bias, gain, ct0, m0, batch, chunks_per_step, name):
    rows = chunks_per_step * MCHUNK
    steps = q.shape[0] // batch // rows
    assert batch * steps * rows == q.shape[0]
    ct_shape = (N_HEADS, VAUG_ROWS, LANES)
    m_shape = (N_HEADS, LANES)
    gate_shape = (chunks_per_step, N_HEADS, MCHUNK)

    def tok(width):
        return pl.BlockSpec((rows, width), lambda b, j: (b * steps + j, 0))

    def tok_t(height):
        return pl.BlockSpec((height, rows), lambda b, j: (0, b * steps + j))

    return pl.pallas_call(
        functools.partial(_mlstm_kernel, chunks_per_step=chunks_per_step),
        grid=(batch, steps),
        in_specs=[tok(N_HEADS * DQK), tok(N_HEADS * DQK), tok_t(N_HEADS * DV), tok(N_HEADS * DV),
                  tok_t(GATE_ROWS), _resident((2 * N_HEADS, 1)), _resident((1, N_HEADS * DV)),
                  _resident(ct_shape), _resident(m_shape)],
        out_specs=[tok(N_HEADS * DV),
                   pl.BlockSpec((None,) + ct_shape, lambda b, j: (b, 0, 0, 0)),
                   pl.BlockSpec((None,) + m_shape, lambda b, j: (b, 0, 0))],
        out_shape=[jax.ShapeDtypeStruct((q.shape[0], N_HEADS * DV), BF16),
                   jax.ShapeDtypeStruct((batch,) + ct_shape, F32),
                   jax.ShapeDtypeStruct((batch,) + m_shape, F32)],
        scratch_shapes=[pltpu.VMEM(ct_shape, F32), pltpu.VMEM(m_shape, F32),
                        pltpu.VMEM(gate_shape, F32), pltpu.VMEM(gate_shape, F32), pltpu.VMEM(gate_shape, F32),
                        pltpu.VMEM((chunks_per_step, MCHUNK, N_HEADS), F32)],
        compiler_params=_compiler_params(("parallel", "arbitrary")),
        name=name,
    )(q, k, vt, o, gates_t, bias, gain, ct0, m0)


def _diff_attn_kernel(lam_ref, q_ref, k_ref, vt_ref, km_ref, vtm_ref, gain_ref, o_ref,
                      qs_ref, m_ref, acc_ref, *, lambda_init):
    qi = pl.program_id(1)
    tq, tk = ATT_TQ, ATT_TK
    neg_inf = jnp.float32(-jnp.inf)

    low = lax.broadcasted_iota(jnp.int32, (tq, LANES), 1) < DQK
    for h in range(N_HEADS):
        q2 = q_ref[:, h * DV:(h + 1) * DV]
        zero = jnp.zeros_like(q2)
        qs_ref[h] = jnp.concatenate([jnp.where(low, q2, zero), jnp.where(low, zero, q2)], axis=0)
    m_ref[...] = jnp.full_like(m_ref, neg_inf)
    acc_ref[...] = jnp.zeros_like(acc_ref)

    def tile_step(get_k, get_vt, n_keys, bias):
        ones_rows = _ones_rows(n_keys)

        def scores(h):
            s = lax.dot_general(get_k(h), qs_ref[h], NT_DIMS, preferred_element_type=F32)
            return s if bias is None else s + bias

        s_next = scores(0)
        for h in range(N_HEADS):
            s = s_next
            if h + 1 < N_HEADS:
                s_next = scores(h + 1)
            m_old = m_ref[h]
            m_new = jnp.maximum(m_old, jnp.max(s, axis=0, keepdims=True))
            alpha = jnp.exp2(m_old - m_new)
            p = jnp.exp2(s - m_new).astype(BF16)
            v_aug = jnp.concatenate([get_vt(h), ones_rows], axis=0)
            acc_ref[h] = alpha * acc_ref[h] + jnp.dot(v_aug, p, preferred_element_type=F32)
            m_ref[h] = m_new

    meta_row = lax.broadcasted_iota(jnp.int32, (ATT_META_ROWS, 2 * tq), 0)
    meta_bias = jnp.where(meta_row >= ATT_META_PAD, 0.0, neg_inf)
    tile_step(lambda h: km_ref[:, h * DV:(h + 1) * DV], lambda h: vtm_ref[h * DV:(h + 1) * DV, :],
              ATT_META_ROWS, meta_bias)

    def full_tile(j, carry):
        start = pl.multiple_of(j * tk, tk)
        tile_step(lambda h: k_ref[pl.ds(start, tk), h * DV:(h + 1) * DV],
                  lambda h: vt_ref[h * DV:(h + 1) * DV, pl.ds(start, tk)], tk, None)
        return carry

    lax.fori_loop(0, qi * (tq // tk), full_tile, 0)

    chunk_shift = CHUNK.bit_length() - 1
    k_chunk = lax.shift_right_logical(lax.broadcasted_iota(jnp.int32, (tk, 2 * tq), 0), chunk_shift)
    q_col = lax.broadcasted_iota(jnp.int32, (tk, 2 * tq), 1)
    q_chunk = lax.shift_right_logical(jnp.bitwise_and(q_col, tq - 1), chunk_shift)
    for d in range(tq // tk):
        diag_bias = jnp.where(k_chunk + d * (tk // CHUNK) <= q_chunk, 0.0, neg_inf)
        start = pl.multiple_of(qi * tq + d * tk, tk)
        tile_step(lambda h: k_ref[pl.ds(start, tk), h * DV:(h + 1) * DV],
                  lambda h: vt_ref[h * DV:(h + 1) * DV, pl.ds(start, tk)], tk, diag_bias)

    lv = lam_ref[...]
    lam = (jnp.exp(jnp.sum(lv[0:1] * lv[1:2], axis=-1, keepdims=True))
           - jnp.exp(jnp.sum(lv[2:3] * lv[3:4], axis=-1, keepdims=True)) + lambda_init)
    for h in range(N_HEADS):
        cols = slice(h * DV, (h + 1) * DV)
        acc = acc_ref[h]
        o_both = acc[0:DV, :] * (1.0 / acc[DV:DV + 1, :])
        o = (o_both[:, :tq] - lam * o_both[:, tq:]).T
        o = o * lax.rsqrt(jnp.mean(o * o, axis=-1, keepdims=True) + SUBLN_EPS)
        o = o * gain_ref[:, cols] * (1.0 - lambda_init)
        o_ref[:, cols] = o.astype(o_ref.dtype)


def _diff_attention(q, k, vt, k_meta, vt_meta, lam_vecs, gain, batch, lambda_init):
    t = q.shape[0]
    seq = t // batch
    nq = seq // ATT_TQ
    assert nq * ATT_TQ == seq and ATT_TQ % ATT_TK == 0 and ATT_TK % CHUNK == 0
    width = N_HEADS * DV
    return pl.pallas_call(
        functools.partial(_diff_attn_kernel, lambda_init=lambda_init),
        grid=(batch, nq),
        in_specs=[_resident(lam_vecs.shape),
                  pl.BlockSpec((ATT_TQ, width), lambda b, i: (b * nq + i, 0)),
                  pl.BlockSpec((seq, width), lambda b, i: (b, 0)),
                  pl.BlockSpec((width, seq), lambda b, i: (0, b)),
                  _resident(k_meta.shape),
                  _resident(vt_meta.shape),
                  _resident((1, width))],
        out_specs=pl.BlockSpec((ATT_TQ, width), lambda b, i: (b * nq + i, 0)),
        out_shape=jax.ShapeDtypeStruct((t, width), BF16),
        scratch_shapes=[pltpu.VMEM((N_HEADS, 2 * ATT_TQ, DV), BF16),
                        pltpu.VMEM((N_HEADS, 1, 2 * ATT_TQ), F32),
                        pltpu.VMEM((N_HEADS, VAUG_ROWS, 2 * ATT_TQ), F32)],
        compiler_params=_compiler_params(("parallel", "arbitrary")),
        name="diff_attention",
    )(lam_vecs, q, k, vt, k_meta, vt_meta, gain)


def _post_mlp_kernel(h_ref, y_ref, wo_ref, g_ref, wu_ref, wd_ref, fg_ref, out_ref, *, final_norm, ff_block):
    h1 = h_ref[...] + jnp.dot(y_ref[...], wo_ref[...], preferred_element_type=F32)
    ms = jnp.mean(h1 * h1, axis=-1, keepdims=True)
    hn = (h1 * lax.rsqrt(ms + EPS) * g_ref[...]).astype(BF16)
    acc = h1
    for c in range(D_FF // ff_block):
        blk = slice(c * ff_block, (c + 1) * ff_block)
        u = jnp.dot(hn, wu_ref[:, blk], preferred_element_type=F32)
        a = jnp.square(jnp.maximum(u, 0.0)).astype(BF16)
        acc = acc + jnp.dot(a, wd_ref[blk, :], preferred_element_type=F32)
    if final_norm:
        ms = jnp.mean(acc * acc, axis=-1, keepdims=True)
        acc = acc * lax.rsqrt(ms + EPS) * fg_ref[...]
    out_ref[...] = acc


def _post_mlp(h, y, w_out, gain, w_up, w_down, final_gain, final_norm, tm, name):
    t = h.shape[0]
    assert t % tm == 0
    return pl.pallas_call(
        functools.partial(_post_mlp_kernel, final_norm=final_norm, ff_block=1024),
        grid=(t // tm,),
        in_specs=[pl.BlockSpec((tm, D_MODEL), lambda i: (i, 0)),
                  pl.BlockSpec((tm, D_MODEL), lambda i: (i, 0)),
                  _resident((D_MODEL, D_MODEL)),
                  _resident((1, D_MODEL)),
                  _resident((D_MODEL, D_FF)),
                  _resident((D_FF, D_MODEL)),
                  _resident((1, D_MODEL))],
        out_specs=pl.BlockSpec((tm, D_MODEL), lambda i: (i, 0)),
        out_shape=jax.ShapeDtypeStruct((t, D_MODEL), F32),
        compiler_params=_compiler_params(("parallel",)),
        name=name,
    )(h, y, w_out, gain, w_up, w_down, final_gain)


def kernel(x, meta_tokens, norm_gains, mlstm_w_in, mlstm_b_gate, mlstm_head_gain, mlstm_w_out,
           diff_w_in, diff_lambda, diff_head_gain, diff_w_out, mlp_w_up, mlp_w_down, final_gain):
    batch, seq, _ = x.shape
    t = batch * seq
    tm = 512
    h = x.reshape(t, D_MODEL)
    h_meta = jnp.concatenate([jnp.zeros((META_ROWS - N_META, D_MODEL), x.dtype), meta_tokens.astype(x.dtype)],
                             axis=0)

    m_qk = N_HEADS * DQK
    m_v = N_HEADS * DV
    w0 = mlstm_w_in[0]
    offs = (0, m_qk, 2 * m_qk, 2 * m_qk + m_v, 2 * m_qk + 2 * m_v)
    w0_parts = [w0[:, offs[0]:offs[1]], w0[:, offs[1]:offs[2]], w0[:, offs[2]:offs[3]].T, w0[:, offs[3]:offs[4]],
                jnp.pad(w0[:, offs[4]:].T, ((0, GATE_ROWS - 2 * N_HEADS), (0, 0)))]
    w0_parts = [w.astype(BF16) for w in w0_parts]
    in0 = dict(gain=norm_gains[0, 0][None], weights=w0_parts, scales=(DQK ** -0.5, 1.0, 1.0, 1.0, 1.0),
               transposed=(False, False, True, False, True), dtypes=(BF16, BF16, BF16, BF16, F32))
    bias = mlstm_b_gate[0][:, None]
    head_gain0 = mlstm_head_gain[0][None]
    mlp0 = dict(w_out=mlstm_w_out[0].astype(BF16), gain=norm_gains[0, 1][None], w_up=mlp_w_up[0].astype(BF16),
                w_down=mlp_w_down[0].astype(BF16), final_gain=final_gain[None], final_norm=False)

    proj_meta = _norm_proj(h_meta, tm=META_ROWS, name="mlstm_in_proj_meta", **in0)
    ct_zero = jnp.zeros((N_HEADS, VAUG_ROWS, LANES), F32)
    m_zero = jnp.zeros((N_HEADS, LANES), F32)
    y_meta, ct_meta, m_meta = _mlstm_core(*proj_meta, bias, head_gain0, ct_zero, m_zero, batch=1,
                                          chunks_per_step=1, name="mlstm_core_meta")
    h_meta = _post_mlp(h_meta, y_meta, tm=META_ROWS, name="mlstm_out_mlp_meta", **mlp0)

    proj = _norm_proj(h, tm=tm, name="mlstm_in_proj", **in0)
    y, _, _ = _mlstm_core(*proj, bias, head_gain0, ct_meta[0], m_meta[0], batch=batch,
                          chunks_per_step=4, name="mlstm_core")
    h = _post_mlp(h, y, tm=tm, name="mlstm_out_mlp", **mlp0)

    lambda_init = 0.8 - 0.6 * math.exp(-0.3 * 1)
    w1 = diff_w_in[0]
    w1_parts = [w1[:, :D_MODEL].astype(BF16), w1[:, D_MODEL:2 * D_MODEL].astype(BF16),
                w1[:, 2 * D_MODEL:].T.astype(BF16)]
    in1 = dict(gain=norm_gains[1, 0][None], weights=w1_parts, scales=(DQK ** -0.5 * math.log2(math.e), 1.0, 1.0),
               transposed=(False, False, True), dtypes=(BF16, BF16, BF16))
    _, k_meta, vt_meta = _norm_proj(h_meta, tm=META_ROWS, name="diff_in_proj_meta", **in1)
    k_meta = k_meta[META_ROWS - ATT_META_ROWS:]
    vt_meta = vt_meta[:, META_ROWS - ATT_META_ROWS:]
    q, k, vt = _norm_proj(h, tm=tm, name="diff_in_proj", **in1)
    y = _diff_attention(q, k, vt, k_meta, vt_meta, diff_lambda[0], diff_head_gain[0][None], batch, lambda_init)
    out = _post_mlp(h, y, diff_w_out[0].astype(BF16), norm_gains[1, 1][None], mlp_w_up[1].astype(BF16),
                    mlp_w_down[1].astype(BF16), final_gain[None], final_norm=True, tm=tm, name="diff_out_mlp")
    return out.reshape(batch, seq, D_MODEL)
```

```python
import functools
import math

import jax
import jax.numpy as jnp
from jax import lax
from jax.experimental import pallas as pl
from jax.experimental.pallas import tpu as pltpu

D_MODEL = 1024
D_FF = 4 * D_MODEL
N_META = 16
CHUNK = 64
N_HEADS = 8
DQK = 64
DV = 128
EPS = 1e-6
SUBLN_EPS = 1e-5

LANES = 128
BF16_ROWS = 16
AUG_ROWS = BF16_ROWS
VAUG_ROWS = DV + AUG_ROWS
GATE_ROWS = LANES

MCHUNK = 256
META_ROWS = MCHUNK
ATT_META_ROWS = CHUNK
ATT_META_PAD = ATT_META_ROWS - N_META
ATT_TQ = 512
ATT_TK = 256
ATT_AHEAD = 5

PROJ_TM = 1024
MLP_TM = 512

VMEM_LIMIT = 56 * 1024 * 1024

F32 = jnp.float32
BF16 = jnp.bfloat16
NT_DIMS = (((1,), (1,)), ((), ()))


def _compiler_params(semantics):
    return pltpu.CompilerParams(dimension_semantics=semantics, vmem_limit_bytes=VMEM_LIMIT)


def _resident(shape):
    return pl.BlockSpec(shape, lambda *_: (0,) * len(shape), pipeline_mode=pl.Buffered(1))


def _ones_rows(width):
    return (lax.broadcasted_iota(jnp.int32, (AUG_ROWS, width), 0) == 0).astype(BF16)


def _norm_proj_kernel(x_ref, g_ref, *refs, scales, transposed):
    n_out = len(scales)
    w_refs, out_refs = refs[:n_out], refs[n_out:]
    x = x_ref[...]
    ms = jnp.mean(x * x, axis=-1, keepdims=True)
    xn = (x * lax.rsqrt(ms + EPS) * g_ref[...]).astype(BF16)
    for w_ref, o_ref, sc, tr in zip(w_refs, out_refs, scales, transposed):
        if tr:
            r = lax.dot_general(w_ref[...], xn, NT_DIMS, preferred_element_type=F32)
        else:
            r = jnp.dot(xn, w_ref[...], preferred_element_type=F32)
        if sc != 1.0:
            r = r * sc
        o_ref[...] = r.astype(o_ref.dtype)


def _norm_proj(h, gain, weights, scales, transposed, dtypes, tm, name):
    t = h.shape[0]
    assert t % tm == 0
    w_specs, out_specs, out_shapes = [], [], []
    for w, tr, dt in zip(weights, transposed, dtypes):
        w_specs.append(_resident(w.shape))
        if tr:
            n = w.shape[0]
            out_specs.append(pl.BlockSpec((n, tm), lambda i: (0, i)))
            out_shapes.append(jax.ShapeDtypeStruct((n, t), dt))
        else:
            n = w.shape[1]
            out_specs.append(pl.BlockSpec((tm, n), lambda i: (i, 0)))
            out_shapes.append(jax.ShapeDtypeStruct((t, n), dt))
    return pl.pallas_call(
        functools.partial(_norm_proj_kernel, scales=scales, transposed=transposed),
        grid=(t // tm,),
        in_specs=[pl.BlockSpec((tm, D_MODEL), lambda i: (i, 0)), _resident((1, D_MODEL))] + w_specs,
        out_specs=out_specs,
        out_shape=out_shapes,
        compiler_params=_compiler_params(("parallel",)),
        name=name,
    )(h, gain, *weights)


def _mlstm_kernel(q_ref, k_ref, vt_ref, o_ref, gt_ref, bias_ref, gain_ref, ct0_ref, m0_ref,
                  y_ref, ct_out_ref, m_out_ref, ct_ref, m_ref, b_ref, r_ref, cm_ref, rt_ref,
                  *, chunks_per_step):
    t = MCHUNK

    @pl.when(pl.program_id(1) == 0)
    def _():
        ct_ref[...] = ct0_ref[...]
        m_ref[...] = m0_ref[...]

    neg_inf = jnp.float32(-jnp.inf)
    causal = (lax.broadcasted_iota(jnp.int32, (t, t), 0)
              <= lax.broadcasted_iota(jnp.int32, (t, t), 1))
    upper = causal.astype(BF16)
    lane_t = lax.broadcasted_iota(jnp.int32, (N_HEADS, t), 1)
    bias = bias_ref[...]

    for c in range(chunks_per_step):
        g = gt_ref[0:2 * N_HEADS, c * t:(c + 1) * t] + bias
        ig = g[0:N_HEADS]
        lf = jax.nn.log_sigmoid(g[N_HEADS:])
        hi = lf.astype(BF16).astype(F32)
        mid = (lf - hi).astype(BF16).astype(F32)
        lo = (lf - hi - mid).astype(BF16).astype(F32)
        parts = jnp.dot(jnp.concatenate([hi, mid, lo], axis=0).astype(BF16), upper,
                        preferred_element_type=F32)
        b = parts[0:N_HEADS] + parts[N_HEADS:2 * N_HEADS] + parts[2 * N_HEADS:]
        r = ig - b
        cm = r
        shift = 1
        while shift < t:
            cm = jnp.maximum(cm, jnp.where(lane_t >= shift, pltpu.roll(cm, shift, 1), neg_inf))
            shift *= 2
        b_ref[c] = b
        r_ref[c] = r
        cm_ref[c] = cm
        rt_ref[c] = r.T

    low = lax.broadcasted_iota(jnp.int32, (t, LANES), 1) < DQK
    low_state = lax.broadcasted_iota(jnp.int32, (VAUG_ROWS, LANES), 1) < DQK
    ones_rows = _ones_rows(t)

    def chunk_body(c, carry):
        rows = pl.ds(pl.multiple_of(c * t, t), t)
        b, r, cm, rt = b_ref[c], r_ref[c], cm_ref[c], rt_ref[c]
        m_prev = m_ref[:, 0:1]
        u = jnp.maximum(m_prev, cm)
        inter = jnp.exp(m_prev - u)
        e_row = jnp.exp(-(b + u))
        b_end = b[:, t - 1:t]
        a = b_end + r
        m_new = jnp.maximum(b_end + m_prev, jnp.max(a, axis=1, keepdims=True))
        decay = jnp.exp(b_end + m_prev - m_new)
        w = jnp.exp(a - m_new)
        m_ref[...] = jnp.broadcast_to(m_new, m_ref.shape)

        def pair_scores(p):
            q2 = q_ref[rows, p * LANES:(p + 1) * LANES]
            k2 = k_ref[rows, p * LANES:(p + 1) * LANES]
            zero = jnp.zeros_like(q2)
            qs = jnp.concatenate([jnp.where(low, q2, zero), jnp.where(low, zero, q2)], axis=0)
            return q2, k2, lax.dot_general(k2, qs, NT_DIMS, preferred_element_type=F32)

        nxt = pair_scores(0)
        for p in range(N_HEADS // 2):
            q2, k2, st = nxt
            heads = (2 * p, 2 * p + 1)
            cts = [ct_ref[h] for h in heads]
            vaugs = [jnp.concatenate([vt_ref[h * DV:(h + 1) * DV, rows], ones_rows], axis=0) for h in heads]
            inter_nums = [lax.dot_general(ct.astype(BF16), q2, NT_DIMS, preferred_element_type=F32)
                          for ct in cts]
            for i, h in enumerate(heads):
                wv = (vaugs[i].astype(F32) * w[h:h + 1, :]).astype(BF16)
                upd = jnp.dot(wv, k2, preferred_element_type=F32)
                own = low_state if i == 0 else jnp.logical_not(low_state)
                ct_ref[h] = decay[h:h + 1, :] * cts[i] + jnp.where(own, upd, 0.0)
            if p + 1 < N_HEADS // 2:
                nxt = pair_scores(p + 1)
            for i, h in enumerate(heads):
                cols = slice(h * DV, (h + 1) * DV)
                arg = rt[:, h:h + 1] - u[h:h + 1, :]
                dt = jnp.exp(jnp.where(causal, arg, neg_inf))
                s_h = (st[:, i * t:(i + 1) * t] * dt).astype(BF16)
                num = (inter[h:h + 1, :] * inter_nums[i]
                       + jnp.dot(vaugs[i], s_h, preferred_element_type=F32))
                inv = 1.0 / jnp.maximum(jnp.abs(num[DV:DV + 1, :]), e_row[h:h + 1, :])
                hh = num[0:DV, :] * inv
                hh = hh * lax.rsqrt(jnp.mean(hh * hh, axis=0, keepdims=True) + EPS)
                out = hh.T * gain_ref[:, cols] * jax.nn.sigmoid(o_ref[rows, cols].astype(F32))
                y_ref[rows, cols] = out.astype(y_ref.dtype)
        return carry

    lax.fori_loop(0, chunks_per_step, chunk_body, 0)
    ct_out_ref[...] = ct_ref[...]
    m_out_ref[...] = m_ref[...]


def _mlstm_core(q, k, vt, o, gates_t, bias, gain, ct0, m0, batch, chunks_per_step, name):
    rows = chunks_per_step * MCHUNK
    steps = q.shape[0] // batch // rows
    assert batch * steps * rows == q.shape[0]
    ct_shape = (N_HEADS, VAUG_ROWS, LANES)
    m_shape = (N_HEADS, LANES)
    gate_shape = (chunks_per_step, N_HEADS, MCHUNK)

    def tok(width):
        return pl.BlockSpec((rows, width), lambda b, j: (b * steps + j, 0))

    def tok_t(height):
        return pl.BlockSpec((height, rows), lambda b, j: (0, b * steps + j))

    return pl.pallas_call(
        functools.partial(_mlstm_kernel, chunks_per_step=chunks_per_step),
        grid=(batch, steps),
        in_specs=[tok(N_HEADS * DQK), tok(N_HEADS * DQK), tok_t(N_HEADS * DV), tok(N_HEADS * DV),
                  tok_t(GATE_ROWS), _resident((2 * N_HEADS, 1)), _resident((1, N_HEADS * DV)),
                  _resident(ct_shape), _resident(m_shape)],
        out_specs=[tok(N_HEADS * DV),
                   pl.BlockSpec((None,) + ct_shape, lambda b, j: (b, 0, 0, 0)),
                   pl.BlockSpec((None,) + m_shape, lambda b, j: (b, 0, 0))],
        out_shape=[jax.ShapeDtypeStruct((q.shape[0], N_HEADS * DV), BF16),
                   jax.ShapeDtypeStruct((batch,) + ct_shape, F32),
                   jax.ShapeDtypeStruct((batch,) + m_shape, F32)],
        scratch_shapes=[pltpu.VMEM(ct_shape, F32), pltpu.VMEM(m_shape, F32),
                        pltpu.VMEM(gate_shape, F32), pltpu.VMEM(gate_shape, F32), pltpu.VMEM(gate_shape, F32),
                        pltpu.VMEM((chunks_per_step, MCHUNK, N_HEADS), F32)],
        compiler_params=_compiler_params(("parallel", "arbitrary")),
        name=name,
    )(q, k, vt, o, gates_t, bias, gain, ct0, m0)


def _diff_attn_kernel(lam_ref, q_ref, k_ref, vt_ref, km_ref, vtm_ref, gain_ref, o_ref,
                      qs_ref, m_ref, acc_ref, *, lambda_init):
    qi = pl.program_id(1)
    tq, tk = ATT_TQ, ATT_TK
    blocks = tq // tk
    neg_inf = jnp.float32(-jnp.inf)

    low = lax.broadcasted_iota(jnp.int32, (tq, LANES), 1) < DQK
    for h in range(N_HEADS):
        q2 = q_ref[:, h * DV:(h + 1) * DV]
        zero = jnp.zeros_like(q2)
        qs_ref[h] = jnp.concatenate([jnp.where(low, q2, zero), jnp.where(low, zero, q2)], axis=0)

    all_units = [(h, br, blk) for h in range(N_HEADS) for br in range(2) for blk in range(blocks)]

    def tile_step(get_k, get_vt, n_keys, units, bias_of, first=False):
        ones_rows = _ones_rows(n_keys)

        def scores(h, br, blk):
            cols = pl.ds((br * blocks + blk) * tk, tk)
            s = lax.dot_general(get_k(h), qs_ref[h, cols, :], NT_DIMS, preferred_element_type=F32)
            bias = bias_of(blk)
            return s if bias is None else s + bias

        pending = [scores(*u) for u in units[:ATT_AHEAD]]
        for n, (h, br, blk) in enumerate(units):
            s = pending.pop(0)
            if n + ATT_AHEAD < len(units):
                pending.append(scores(*units[n + ATT_AHEAD]))
            cols = pl.ds((br * blocks + blk) * tk, tk)
            v_aug = jnp.concatenate([get_vt(h), ones_rows], axis=0)
            s_max = jnp.max(s, axis=0, keepdims=True)
            if first:
                m_new = s_max
                acc_ref[h, :, cols] = jnp.dot(v_aug, jnp.exp2(s - m_new).astype(BF16), preferred_element_type=F32)
            else:
                m_old = m_ref[h, :, cols]
                m_new = jnp.maximum(m_old, s_max)
                alpha = jnp.exp2(m_old - m_new)
                p = jnp.exp2(s - m_new).astype(BF16)
                acc_ref[h, :, cols] = alpha * acc_ref[h, :, cols] + jnp.dot(v_aug, p, preferred_element_type=F32)
            m_ref[h, :, cols] = m_new

    def frame_k(start):
        return lambda h: k_ref[pl.ds(start, tk), h * DV:(h + 1) * DV]

    def frame_vt(start):
        return lambda h: vt_ref[h * DV:(h + 1) * DV, pl.ds(start, tk)]

    chunk_shift = CHUNK.bit_length() - 1
    k_chunk = lax.shift_right_logical(lax.broadcasted_iota(jnp.int32, (tk, tk), 0), chunk_shift)
    q_chunk = lax.shift_right_logical(lax.broadcasted_iota(jnp.int32, (tk, tk), 1), chunk_shift)
    diag_bias = jnp.where(k_chunk <= q_chunk, 0.0, neg_inf)
    meta_row = lax.broadcasted_iota(jnp.int32, (ATT_META_ROWS, tk), 0)
    meta_bias = jnp.where(meta_row >= ATT_META_PAD, 0.0, neg_inf)
    first_bias = [jnp.concatenate([diag_bias if blk == 0 else jnp.zeros_like(diag_bias), meta_bias], axis=0)
                  for blk in range(min(blocks, 2))]
    start = pl.multiple_of(qi * tq, tk)
    tile_step(lambda h: jnp.concatenate([frame_k(start)(h), km_ref[:, h * DV:(h + 1) * DV]], axis=0),
              lambda h: jnp.concatenate([frame_vt(start)(h), vtm_ref[h * DV:(h + 1) * DV, :]], axis=1),
              tk + ATT_META_ROWS, all_units, lambda blk: first_bias[min(blk, 1)], first=True)
    for d in range(1, blocks):
        start = pl.multiple_of(qi * tq + d * tk, tk)
        tile_step(frame_k(start), frame_vt(start), tk, [u for u in all_units if u[2] >= d],
                  lambda blk, d=d: diag_bias if blk == d else None)

    def full_tile(j, carry):
        start = pl.multiple_of(j * tk, tk)
        tile_step(frame_k(start), frame_vt(start), tk, all_units, lambda blk: None)
        return carry

    lax.fori_loop(0, qi * blocks, full_tile, 0)

    lv = lam_ref[...]
    lam = (jnp.exp(jnp.sum(lv[0:1] * lv[1:2], axis=-1, keepdims=True))
           - jnp.exp(jnp.sum(lv[2:3] * lv[3:4], axis=-1, keepdims=True)) + lambda_init)
    for h in range(N_HEADS):
        cols = slice(h * DV, (h + 1) * DV)
        acc = acc_ref[h]
        o_both = acc[0:DV, :] * (1.0 / acc[DV:DV + 1, :])
        o = (o_both[:, :tq] - lam * o_both[:, tq:]).T
        o = o * lax.rsqrt(jnp.mean(o * o, axis=-1, keepdims=True) + SUBLN_EPS)
        o = o * gain_ref[:, cols] * (1.0 - lambda_init)
        o_ref[:, cols] = o.astype(o_ref.dtype)


def _diff_attention(q, k, vt, k_meta, vt_meta, lam_vecs, gain, batch, lambda_init):
    t = q.shape[0]
    seq = t // batch
    nq = seq // ATT_TQ
    assert nq * ATT_TQ == seq and ATT_TQ % ATT_TK == 0 and ATT_TK % CHUNK == 0
    width = N_HEADS * DV
    return pl.pallas_call(
        functools.partial(_diff_attn_kernel, lambda_init=lambda_init),
        grid=(batch, nq),
        in_specs=[_resident(lam_vecs.shape),
                  pl.BlockSpec((ATT_TQ, width), lambda b, i: (b * nq + i, 0)),
                  pl.BlockSpec((seq, width), lambda b, i: (b, 0)),
                  pl.BlockSpec((width, seq), lambda b, i: (0, b)),
                  _resident(k_meta.shape),
                  _resident(vt_meta.shape),
                  _resident((1, width))],
        out_specs=pl.BlockSpec((ATT_TQ, width), lambda b, i: (b * nq + i, 0)),
        out_shape=jax.ShapeDtypeStruct((t, width), BF16),
        scratch_shapes=[pltpu.VMEM((N_HEADS, 2 * ATT_TQ, DV), BF16),
                        pltpu.VMEM((N_HEADS, 1, 2 * ATT_TQ), F32),
                        pltpu.VMEM((N_HEADS, VAUG_ROWS, 2 * ATT_TQ), F32)],
        compiler_params=_compiler_params(("parallel", "arbitrary")),
        name="diff_attention",
    )(lam_vecs, q, k, vt, k_meta, vt_meta, gain)


def _post_mlp_kernel(h_ref, y_ref, wo_ref, g_ref, wu_ref, wd_ref, fg_ref, out_ref, *, final_norm, ff_block):
    h1 = h_ref[...] + jnp.dot(y_ref[...], wo_ref[...], preferred_element_type=F32)
    ms = jnp.mean(h1 * h1, axis=-1, keepdims=True)
    hn = (h1 * lax.rsqrt(ms + EPS) * g_ref[...]).astype(BF16)
    acc = h1
    for c in range(D_FF // ff_block):
        blk = slice(c * ff_block, (c + 1) * ff_block)
        u = jnp.dot(hn, wu_ref[:, blk], preferred_element_type=F32)
        a = jnp.square(jnp.maximum(u, 0.0)).astype(BF16)
        acc = acc + jnp.dot(a, wd_ref[blk, :], preferred_element_type=F32)
    if final_norm:
        ms = jnp.mean(acc * acc, axis=-1, keepdims=True)
        acc = acc * lax.rsqrt(ms + EPS) * fg_ref[...]
    out_ref[...] = acc


def _post_mlp(h, y, w_out, gain, w_up, w_down, final_gain, final_norm, tm, name):
    t = h.shape[0]
    assert t % tm == 0
    return pl.pallas_call(
        functools.partial(_post_mlp_kernel, final_norm=final_norm, ff_block=1024),
        grid=(t // tm,),
        in_specs=[pl.BlockSpec((tm, D_MODEL), lambda i: (i, 0)),
                  pl.BlockSpec((tm, D_MODEL), lambda i: (i, 0)),
                  _resident((D_MODEL, D_MODEL)),
                  _resident((1, D_MODEL)),
                  _resident((D_MODEL, D_FF)),
                  _resident((D_FF, D_MODEL)),
                  _resident((1, D_MODEL))],
        out_specs=pl.BlockSpec((tm, D_MODEL), lambda i: (i, 0)),
        out_shape=jax.ShapeDtypeStruct((t, D_MODEL), F32),
        compiler_params=_compiler_params(("parallel",)),
        name=name,
    )(h, y, w_out, gain, w_up, w_down, final_gain)


def kernel(x, meta_tokens, norm_gains, mlstm_w_in, mlstm_b_gate, mlstm_head_gain, mlstm_w_out,
           diff_w_in, diff_lambda, diff_head_gain, diff_w_out, mlp_w_up, mlp_w_down, final_gain):
    batch, seq, _ = x.shape
    t = batch * seq
    h = x.reshape(t, D_MODEL)
    h_meta = jnp.concatenate([jnp.zeros((META_ROWS - N_META, D_MODEL), x.dtype), meta_tokens.astype(x.dtype)],
                             axis=0)

    m_qk = N_HEADS * DQK
    m_v = N_HEADS * DV
    w0 = mlstm_w_in[0]
    offs = (0, m_qk, 2 * m_qk, 2 * m_qk + m_v, 2 * m_qk + 2 * m_v)
    w0_parts = [w0[:, offs[0]:offs[1]], w0[:, offs[1]:offs[2]], w0[:, offs[2]:offs[3]].T, w0[:, offs[3]:offs[4]],
                jnp.pad(w0[:, offs[4]:].T, ((0, GATE_ROWS - 2 * N_HEADS), (0, 0)))]
    w0_parts = [w.astype(BF16) for w in w0_parts]
    in0 = dict(gain=norm_gains[0, 0][None], weights=w0_parts, scales=(DQK ** -0.5, 1.0, 1.0, 1.0, 1.0),
               transposed=(False, False, True, False, True), dtypes=(BF16, BF16, BF16, BF16, F32))
    bias = mlstm_b_gate[0][:, None]
    head_gain0 = mlstm_head_gain[0][None]
    mlp0 = dict(w_out=mlstm_w_out[0].astype(BF16), gain=norm_gains[0, 1][None], w_up=mlp_w_up[0].astype(BF16),
                w_down=mlp_w_down[0].astype(BF16), final_gain=final_gain[None], final_norm=False)

    proj_meta = _norm_proj(h_meta, tm=META_ROWS, name="mlstm_in_proj_meta", **in0)
    ct_zero = jnp.zeros((N_HEADS, VAUG_ROWS, LANES), F32)
    m_zero = jnp.zeros((N_HEADS, LANES), F32)
    y_meta, ct_meta, m_meta = _mlstm_core(*proj_meta, bias, head_gain0, ct_zero, m_zero, batch=1,
                                          chunks_per_step=1, name="mlstm_core_meta")
    h_meta = _post_mlp(h_meta, y_meta, tm=META_ROWS, name="mlstm_out_mlp_meta", **mlp0)

    proj = _norm_proj(h, tm=PROJ_TM, name="mlstm_in_proj", **in0)
    y, _, _ = _mlstm_core(*proj, bias, head_gain0, ct_meta[0], m_meta[0], batch=batch,
                          chunks_per_step=4, name="mlstm_core")
    h = _post_mlp(h, y, tm=MLP_TM, name="mlstm_out_mlp", **mlp0)

    lambda_init = 0.8 - 0.6 * math.exp(-0.3 * 1)
    w1 = diff_w_in[0]
    w1_parts = [w1[:, :D_MODEL].astype(BF16), w1[:, D_MODEL:2 * D_MODEL].astype(BF16),
                w1[:, 2 * D_MODEL:].T.astype(BF16)]
    in1 = dict(gain=norm_gains[1, 0][None], weights=w1_parts, scales=(DQK ** -0.5 * math.log2(math.e), 1.0, 1.0),
               transposed=(False, False, True), dtypes=(BF16, BF16, BF16))
    _, k_meta, vt_meta = _norm_proj(h_meta, tm=META_ROWS, name="diff_in_proj_meta", **in1)
    k_meta = k_meta[META_ROWS - ATT_META_ROWS:]
    vt_meta = vt_meta[:, META_ROWS - ATT_META_ROWS:]
    q, k, vt = _norm_proj(h, tm=PROJ_TM, name="diff_in_proj", **in1)
    y = _diff_attention(q, k, vt, k_meta, vt_meta, diff_lambda[0], diff_head_gain[0][None], batch, lambda_init)
    out = _post_mlp(h, y, diff_w_out[0].astype(BF16), norm_gains[1, 1][None], mlp_w_up[1].astype(BF16),
                    mlp_w_down[1].astype(BF16), final_gain[None], final_norm=True, tm=MLP_TM, name="diff_out_mlp")
    return out.reshape(batch, seq, D_MODEL)
```

```python
import functools
import math

import jax
import jax.numpy as jnp
from jax import lax
from jax.experimental import pallas as pl
from jax.experimental.pallas import tpu as pltpu

D_MODEL = 1024
D_FF = 4 * D_MODEL
N_META = 16
CHUNK = 64
N_HEADS = 8
DQK = 64
DV = 128
EPS = 1e-6
SUBLN_EPS = 1e-5
LOG2E = math.log2(math.e)

LANES = 128
BF16_ROWS = 16
AUG_ROWS = BF16_ROWS
VAUG_ROWS = DV + AUG_ROWS
GATE_ROWS = 2 * N_HEADS

MCHUNK = 256
META_ROWS = MCHUNK
ATT_META_ROWS = CHUNK
ATT_META_PAD = ATT_META_ROWS - N_META
ATT_TQ = 512
ATT_TK = 256
ATT_AHEAD = 5

PROJ_TM = 1024
MLP_TM = 512

VMEM_LIMIT = 56 * 1024 * 1024

F32 = jnp.float32
BF16 = jnp.bfloat16
NT_DIMS = (((1,), (1,)), ((), ()))


def _compiler_params(semantics):
    return pltpu.CompilerParams(dimension_semantics=semantics, vmem_limit_bytes=VMEM_LIMIT)


def _resident(shape):
    return pl.BlockSpec(shape, lambda *_: (0,) * len(shape), pipeline_mode=pl.Buffered(1))


def _ones_rows(width):
    return (lax.broadcasted_iota(jnp.int32, (AUG_ROWS, width), 0) == 0).astype(BF16)


def _norm_proj_kernel(x_ref, g_ref, *refs, scales, transposed, sigmoid):
    n_out = len(scales)
    w_refs, out_refs = refs[:n_out], refs[n_out:]
    x = x_ref[...]
    ms = jnp.mean(x * x, axis=-1, keepdims=True)
    xn = (x * lax.rsqrt(ms + EPS) * g_ref[...]).astype(BF16)
    for w_ref, o_ref, sc, tr, sg in zip(w_refs, out_refs, scales, transposed, sigmoid):
        if tr:
            r = lax.dot_general(w_ref[...], xn, NT_DIMS, preferred_element_type=F32)
        else:
            r = jnp.dot(xn, w_ref[...], preferred_element_type=F32)
        if sc != 1.0:
            r = r * sc
        if sg:
            r = jax.nn.sigmoid(r)
        o_ref[...] = r.astype(o_ref.dtype)


def _norm_proj(h, gain, weights, scales, transposed, sigmoid, dtypes, tm, name):
    t = h.shape[0]
    assert t % tm == 0
    w_specs, out_specs, out_shapes = [], [], []
    for w, tr, dt in zip(weights, transposed, dtypes):
        w_specs.append(_resident(w.shape))
        if tr:
            n = w.shape[0]
            out_specs.append(pl.BlockSpec((n, tm), lambda i: (0, i)))
            out_shapes.append(jax.ShapeDtypeStruct((n, t), dt))
        else:
            n = w.shape[1]
            out_specs.append(pl.BlockSpec((tm, n), lambda i: (i, 0)))
            out_shapes.append(jax.ShapeDtypeStruct((t, n), dt))
    return pl.pallas_call(
        functools.partial(_norm_proj_kernel, scales=scales, transposed=transposed, sigmoid=sigmoid),
        grid=(t // tm,),
        in_specs=[pl.BlockSpec((tm, D_MODEL), lambda i: (i, 0)), _resident((1, D_MODEL))] + w_specs,
        out_specs=out_specs,
        out_shape=out_shapes,
        compiler_params=_compiler_params(("parallel",)),
        name=name,
    )(h, gain, *weights)


def _mlstm_kernel(q_ref, k_ref, vt_ref, o_ref, gt_ref, bias_ref, gain_ref, ct0_ref, m0_ref,
                  y_ref, ct_out_ref, m_out_ref, ct_ref, m_ref, b_ref, r_ref, cm_ref, rt_ref,
                  *, chunks_per_step):
    t = MCHUNK

    @pl.when(pl.program_id(1) == 0)
    def _():
        ct_ref[...] = ct0_ref[...]
        m_ref[...] = m0_ref[...]

    neg_inf = jnp.float32(-jnp.inf)
    causal = (lax.broadcasted_iota(jnp.int32, (t, t), 0)
              <= lax.broadcasted_iota(jnp.int32, (t, t), 1))
    upper = causal.astype(BF16)
    lane_t = lax.broadcasted_iota(jnp.int32, (N_HEADS, t), 1)
    bias = bias_ref[...]

    for c in range(chunks_per_step):
        g = gt_ref[0:2 * N_HEADS, c * t:(c + 1) * t] + bias
        ig = g[0:N_HEADS]
        lf = jax.nn.log_sigmoid(g[N_HEADS:])
        hi = lf.astype(BF16).astype(F32)
        mid = (lf - hi).astype(BF16).astype(F32)
        lo = (lf - hi - mid).astype(BF16).astype(F32)
        parts = jnp.dot(jnp.concatenate([hi, mid, lo], axis=0).astype(BF16), upper,
                        preferred_element_type=F32)
        b = parts[0:N_HEADS] + parts[N_HEADS:2 * N_HEADS] + parts[2 * N_HEADS:]
        r = ig - b
        cm = r
        shift = 1
        while shift < t:
            cm = jnp.maximum(cm, jnp.where(lane_t >= shift, pltpu.roll(cm, shift, 1), neg_inf))
            shift *= 2
        b_ref[c] = b
        r_ref[c] = r
        cm_ref[c] = cm
        rt_ref[c] = (r * LOG2E).T

    low = lax.broadcasted_iota(jnp.int32, (t, LANES), 1) < DQK
    low_state = lax.broadcasted_iota(jnp.int32, (VAUG_ROWS, LANES), 1) < DQK
    ones_rows = _ones_rows(t)
    half = t // 2
    tri = causal[:half, :half]

    def chunk_body(c, carry):
        rows = pl.ds(pl.multiple_of(c * t, t), t)
        b, r, cm, rt = b_ref[c], r_ref[c], cm_ref[c], rt_ref[c]
        m_prev = m_ref[:, 0:1]
        u = jnp.maximum(m_prev, cm)
        inter = jnp.exp(m_prev - u)
        e_row = jnp.exp(-(b + u))
        b_end = b[:, t - 1:t]
        a = b_end + r
        m_new = jnp.maximum(b_end + m_prev, jnp.max(a, axis=1, keepdims=True))
        decay = jnp.exp(b_end + m_prev - m_new)
        w = jnp.exp(a - m_new).astype(BF16)
        u2 = u * LOG2E
        m_ref[...] = jnp.broadcast_to(m_new, m_ref.shape)

        def pair_scores(p):
            q2 = q_ref[rows, p * LANES:(p + 1) * LANES]
            k2 = k_ref[rows, p * LANES:(p + 1) * LANES]
            zero = jnp.zeros_like(q2)
            qs = jnp.concatenate([jnp.where(low, q2, zero), jnp.where(low, zero, q2)], axis=0)
            return q2, k2, lax.dot_general(k2, qs, NT_DIMS, preferred_element_type=F32)

        nxt = pair_scores(0)
        for p in range(N_HEADS // 2):
            q2, k2, st = nxt
            heads = (2 * p, 2 * p + 1)
            cts = [ct_ref[h] for h in heads]
            vaugs = [jnp.concatenate([vt_ref[h * DV:(h + 1) * DV, rows], ones_rows], axis=0) for h in heads]
            inter_nums = [lax.dot_general(ct.astype(BF16), q2, NT_DIMS, preferred_element_type=F32)
                          for ct in cts]
            for i, h in enumerate(heads):
                wv = vaugs[i] * w[h:h + 1, :]
                upd = jnp.dot(wv, k2, preferred_element_type=F32)
                own = low_state if i == 0 else jnp.logical_not(low_state)
                ct_ref[h] = decay[h:h + 1, :] * cts[i] + jnp.where(own, upd, 0.0)
            if p + 1 < N_HEADS // 2:
                nxt = pair_scores(p + 1)
            for i, h in enumerate(heads):
                cols = slice(h * DV, (h + 1) * DV)
                r_col, u_row = rt[:, h:h + 1], u2[h:h + 1, :]
                st_h = st[:, i * t:(i + 1) * t]
                d00 = jnp.exp2(jnp.where(tri, r_col[:half] - u_row[:, :half], neg_inf))
                d01 = jnp.exp2(r_col[:half] - u_row[:, half:])
                d11 = jnp.exp2(jnp.where(tri, r_col[half:] - u_row[:, half:], neg_inf))
                s_h = jnp.concatenate(
                    [jnp.concatenate([st_h[:half, :half] * d00, st_h[:half, half:] * d01], axis=1),
                     jnp.concatenate([jnp.zeros((half, half), F32), st_h[half:, half:] * d11], axis=1)],
                    axis=0).astype(BF16)
                num = (inter[h:h + 1, :] * inter_nums[i]
                       + jnp.dot(vaugs[i], s_h, preferred_element_type=F32))
                inv = 1.0 / jnp.maximum(jnp.abs(num[DV:DV + 1, :]), e_row[h:h + 1, :])
                hh = num[0:DV, :] * inv
                hh = hh * lax.rsqrt(jnp.mean(hh * hh, axis=0, keepdims=True) + EPS)
                out = hh.T * gain_ref[:, cols] * o_ref[rows, cols].astype(F32)
                y_ref[rows, cols] = out.astype(y_ref.dtype)
        return carry

    lax.fori_loop(0, chunks_per_step, chunk_body, 0)
    ct_out_ref[...] = ct_ref[...]
    m_out_ref[...] = m_ref[...]


def _mlstm_core(q, k, vt, o, gates_t, bias, gain, ct0, m0, batch, chunks_per_step, name):
    rows = chunks_per_step * MCHUNK
    steps = q.shape[0] // batch // rows
    assert batch * steps * rows == q.shape[0]
    ct_shape = (N_HEADS, VAUG_ROWS, LANES)
    m_shape = (N_HEADS, LANES)
    gate_shape = (chunks_per_step, N_HEADS, MCHUNK)

    def tok(width):
        return pl.BlockSpec((rows, width), lambda b, j: (b * steps + j, 0))

    def tok_t(height):
        return pl.BlockSpec((height, rows), lambda b, j: (0, b * steps + j))

    return pl.pallas_call(
        functools.partial(_mlstm_kernel, chunks_per_step=chunks_per_step),
        grid=(batch, steps),
        in_specs=[tok(N_HEADS * DQK), tok(N_HEADS * DQK), tok_t(N_HEADS * DV), tok(N_HEADS * DV),
                  tok_t(GATE_ROWS), _resident((2 * N_HEADS, 1)), _resident((1, N_HEADS * DV)),
                  _resident(ct_shape), _resident(m_shape)],
        out_specs=[tok(N_HEADS * DV),
                   pl.BlockSpec((None,) + ct_shape, lambda b, j: (b, 0, 0, 0)),
                   pl.BlockSpec((None,) + m_shape, lambda b, j: (b, 0, 0))],
        out_shape=[jax.ShapeDtypeStruct((q.shape[0], N_HEADS * DV), BF16),
                   jax.ShapeDtypeStruct((batch,) + ct_shape, F32),
                   jax.ShapeDtypeStruct((batch,) + m_shape, F32)],
        scratch_shapes=[pltpu.VMEM(ct_shape, F32), pltpu.VMEM(m_shape, F32),
                        pltpu.VMEM(gate_shape, F32), pltpu.VMEM(gate_shape, F32), pltpu.VMEM(gate_shape, F32),
                        pltpu.VMEM((chunks_per_step, MCHUNK, N_HEADS), F32)],
        compiler_params=_compiler_params(("parallel", "arbitrary")),
        name=name,
    )(q, k, vt, o, gates_t, bias, gain, ct0, m0)


def _diff_attn_kernel(lam_ref, q_ref, k_ref, vt_ref, km_ref, vtm_ref, gain_ref, o_ref,
                      qs_ref, m_ref, acc_ref, sc_ref, *, lambda_init):
    qi = pl.program_id(1)
    tq, tk = ATT_TQ, ATT_TK
    blocks = tq // tk
    neg_inf = jnp.float32(-jnp.inf)

    low = lax.broadcasted_iota(jnp.int32, (tq, LANES), 1) < DQK
    for h in range(N_HEADS):
        q2 = q_ref[:, h * DV:(h + 1) * DV]
        zero = jnp.zeros_like(q2)
        qs_ref[h] = jnp.concatenate([jnp.where(low, q2, zero), jnp.where(low, zero, q2)], axis=0)

    all_units = [(h, br, blk) for h in range(N_HEADS) for br in range(2) for blk in range(blocks)]

    def unit_cols(br, blk):
        return pl.ds((br * blocks + blk) * tk, tk)

    def run(items, primed, next_items):
        stream = items + next_items

        def scores(item):
            tile, (h, br, blk) = item
            s = lax.dot_general(tile["k"](h), qs_ref[h, unit_cols(br, blk), :], NT_DIMS,
                                preferred_element_type=F32)
            bias = tile["bias"](blk)
            return s if bias is None else s + bias

        pending = [sc_ref[n] if primed else scores(stream[n]) for n in range(ATT_AHEAD)]
        for n, (tile, (h, br, blk)) in enumerate(items):
            s = pending.pop(0)
            if n + ATT_AHEAD < len(stream):
                pending.append(scores(stream[n + ATT_AHEAD]))
            cols = unit_cols(br, blk)
            v_aug = jnp.concatenate([tile["vt"](h), tile["ones"]], axis=0)
            s_max = jnp.max(s, axis=0, keepdims=True)
            if tile["first"]:
                m_new = s_max
                acc_ref[h, :, cols] = jnp.dot(v_aug, jnp.exp2(s - m_new).astype(BF16), preferred_element_type=F32)
            else:
                m_old = m_ref[h, :, cols]
                m_new = jnp.maximum(m_old, s_max)
                alpha = jnp.exp2(m_old - m_new)
                p = jnp.exp2(s - m_new).astype(BF16)
                acc_ref[h, :, cols] = alpha * acc_ref[h, :, cols] + jnp.dot(v_aug, p, preferred_element_type=F32)
            m_ref[h, :, cols] = m_new
        for n, s in enumerate(pending):
            sc_ref[n] = s

    ones_tile = _ones_rows(tk)

    def frame_tile(start, bias_of):
        return dict(k=lambda h: k_ref[pl.ds(start, tk), h * DV:(h + 1) * DV],
                    vt=lambda h: vt_ref[h * DV:(h + 1) * DV, pl.ds(start, tk)],
                    ones=ones_tile, bias=bias_of, first=False)

    def full_items(start, units):
        tile = frame_tile(start, lambda blk: None)
        return [(tile, u) for u in units]

    chunk_shift = CHUNK.bit_length() - 1
    k_chunk = lax.shift_right_logical(lax.broadcasted_iota(jnp.int32, (tk, tk), 0), chunk_shift)
    q_chunk = lax.shift_right_logical(lax.broadcasted_iota(jnp.int32, (tk, tk), 1), chunk_shift)
    diag_bias = jnp.where(k_chunk <= q_chunk, 0.0, neg_inf)
    meta_row = lax.broadcasted_iota(jnp.int32, (ATT_META_ROWS, tk), 0)
    meta_bias = jnp.where(meta_row >= ATT_META_PAD, 0.0, neg_inf)
    first_bias = [jnp.concatenate([diag_bias if blk == 0 else jnp.zeros_like(diag_bias), meta_bias], axis=0)
                  for blk in range(min(blocks, 2))]
    start0 = pl.multiple_of(qi * tq, tk)
    first_tile = dict(
        k=lambda h: jnp.concatenate([k_ref[pl.ds(start0, tk), h * DV:(h + 1) * DV],
                                     km_ref[:, h * DV:(h + 1) * DV]], axis=0),
        vt=lambda h: jnp.concatenate([vt_ref[h * DV:(h + 1) * DV, pl.ds(start0, tk)],
                                      vtm_ref[h * DV:(h + 1) * DV, :]], axis=1),
        ones=_ones_rows(tk + ATT_META_ROWS), bias=lambda blk: first_bias[min(blk, 1)], first=True)
    items = [(first_tile, u) for u in all_units]
    for d in range(1, blocks):
        tile = frame_tile(pl.multiple_of(qi * tq + d * tk, tk), lambda blk, d=d: diag_bias if blk == d else None)
        items += [(tile, u) for u in all_units if u[2] >= d]
    head_units = all_units[:ATT_AHEAD]
    run(items, primed=False, next_items=full_items(0, head_units))

    n_full = qi * blocks
    last_tile = k_ref.shape[0] // tk - 1

    def full_tile(j, carry):
        nxt = pl.multiple_of(jnp.minimum(j + 1, last_tile) * tk, tk)
        run(full_items(pl.multiple_of(j * tk, tk), all_units), primed=True,
            next_items=full_items(nxt, head_units))
        return carry

    lax.fori_loop(0, n_full, full_tile, 0)

    lv = lam_ref[...]
    lam = (jnp.exp(jnp.sum(lv[0:1] * lv[1:2], axis=-1, keepdims=True))
           - jnp.exp(jnp.sum(lv[2:3] * lv[3:4], axis=-1, keepdims=True)) + lambda_init)
    out_gain = gain_ref[...] * (1.0 - lambda_init)
    for h in range(N_HEADS):
        cols = slice(h * DV, (h + 1) * DV)
        acc = acc_ref[h]
        inv_l = 1.0 / acc[DV:DV + 1, :]
        o_t = acc[0:DV, :tq] * inv_l[:, :tq] - lam * (acc[0:DV, tq:] * inv_l[:, tq:])
        o_t = o_t * lax.rsqrt(jnp.mean(o_t * o_t, axis=0, keepdims=True) + SUBLN_EPS)
        o_ref[:, cols] = (o_t.T * out_gain[:, cols]).astype(o_ref.dtype)


def _diff_attention(q, k, vt, k_meta, vt_meta, lam_vecs, gain, batch, lambda_init):
    t = q.shape[0]
    seq = t // batch
    nq = seq // ATT_TQ
    assert nq * ATT_TQ == seq and ATT_TQ % ATT_TK == 0 and ATT_TK % CHUNK == 0
    width = N_HEADS * DV
    return pl.pallas_call(
        functools.partial(_diff_attn_kernel, lambda_init=lambda_init),
        grid=(batch, nq),
        in_specs=[_resident(lam_vecs.shape),
                  pl.BlockSpec((ATT_TQ, width), lambda b, i: (b * nq + i, 0)),
                  pl.BlockSpec((seq, width), lambda b, i: (b, 0)),
                  pl.BlockSpec((width, seq), lambda b, i: (0, b)),
                  _resident(k_meta.shape),
                  _resident(vt_meta.shape),
                  _resident((1, width))],
        out_specs=pl.BlockSpec((ATT_TQ, width), lambda b, i: (b * nq + i, 0)),
        out_shape=jax.ShapeDtypeStruct((t, width), BF16),
        scratch_shapes=[pltpu.VMEM((N_HEADS, 2 * ATT_TQ, DV), BF16),
                        pltpu.VMEM((N_HEADS, 1, 2 * ATT_TQ), F32),
                        pltpu.VMEM((N_HEADS, VAUG_ROWS, 2 * ATT_TQ), F32),
                        pltpu.VMEM((ATT_AHEAD, ATT_TK, ATT_TK), F32)],
        compiler_params=_compiler_params(("parallel", "arbitrary")),
        name="diff_attention",
    )(lam_vecs, q, k, vt, k_meta, vt_meta, gain)


def _post_mlp_kernel(h_ref, y_ref, wo_ref, g_ref, wu_ref, wd_ref, fg_ref, out_ref, *, final_norm, ff_block):
    h1 = h_ref[...] + jnp.dot(y_ref[...], wo_ref[...], preferred_element_type=F32)
    ms = jnp.mean(h1 * h1, axis=-1, keepdims=True)
    hn = (h1 * lax.rsqrt(ms + EPS) * g_ref[...]).astype(BF16)
    acc = h1
    for c in range(D_FF // ff_block):
        blk = slice(c * ff_block, (c + 1) * ff_block)
        u = jnp.dot(hn, wu_ref[:, blk], preferred_element_type=F32)
        a = jnp.square(jnp.maximum(u, 0.0)).astype(BF16)
        acc = acc + jnp.dot(a, wd_ref[blk, :], preferred_element_type=F32)
    if final_norm:
        ms = jnp.mean(acc * acc, axis=-1, keepdims=True)
        acc = acc * lax.rsqrt(ms + EPS) * fg_ref[...]
    out_ref[...] = acc


def _post_mlp(h, y, w_out, gain, w_up, w_down, final_gain, final_norm, tm, name):
    t = h.shape[0]
    assert t % tm == 0
    return pl.pallas_call(
        functools.partial(_post_mlp_kernel, final_norm=final_norm, ff_block=1024),
        grid=(t // tm,),
        in_specs=[pl.BlockSpec((tm, D_MODEL), lambda i: (i, 0)),
                  pl.BlockSpec((tm, D_MODEL), lambda i: (i, 0)),
                  _resident((D_MODEL, D_MODEL)),
                  _resident((1, D_MODEL)),
                  _resident((D_MODEL, D_FF)),
                  _resident((D_FF, D_MODEL)),
                  _resident((1, D_MODEL))],
        out_specs=pl.BlockSpec((tm, D_MODEL), lambda i: (i, 0)),
        out_shape=jax.ShapeDtypeStruct((t, D_MODEL), F32),
        compiler_params=_compiler_params(("parallel",)),
        name=name,
    )(h, y, w_out, gain, w_up, w_down, final_gain)


def kernel(x, meta_tokens, norm_gains, mlstm_w_in, mlstm_b_gate, mlstm_head_gain, mlstm_w_out,
           diff_w_in, diff_lambda, diff_head_gain, diff_w_out, mlp_w_up, mlp_w_down, final_gain):
    batch, seq, _ = x.shape
    t = batch * seq
    h = x.reshape(t, D_MODEL)
    h_meta = jnp.concatenate([jnp.zeros((META_ROWS - N_META, D_MODEL), x.dtype), meta_tokens.astype(x.dtype)],
                             axis=0)

    m_qk = N_HEADS * DQK
    m_v = N_HEADS * DV
    w0 = mlstm_w_in[0]
    offs = (0, m_qk, 2 * m_qk, 2 * m_qk + m_v, 2 * m_qk + 2 * m_v)
    w0_parts = [w0[:, offs[0]:offs[1]], w0[:, offs[1]:offs[2]], w0[:, offs[2]:offs[3]].T, w0[:, offs[3]:offs[4]],
                w0[:, offs[4]:].T]
    w0_parts = [w.astype(BF16) for w in w0_parts]
    in0 = dict(gain=norm_gains[0, 0][None], weights=w0_parts, scales=(DQK ** -0.5, 1.0, 1.0, 1.0, 1.0),
               transposed=(False, False, True, False, True), sigmoid=(False, False, False, True, False),
               dtypes=(BF16, BF16, BF16, BF16, F32))
    bias = mlstm_b_gate[0][:, None]
    head_gain0 = mlstm_head_gain[0][None]
    mlp0 = dict(w_out=mlstm_w_out[0].astype(BF16), gain=norm_gains[0, 1][None], w_up=mlp_w_up[0].astype(BF16),
                w_down=mlp_w_down[0].astype(BF16), final_gain=final_gain[None], final_norm=False)

    proj_meta = _norm_proj(h_meta, tm=META_ROWS, name="mlstm_in_proj_meta", **in0)
    ct_zero = jnp.zeros((N_HEADS, VAUG_ROWS, LANES), F32)
    m_zero = jnp.zeros((N_HEADS, LANES), F32)
    y_meta, ct_meta, m_meta = _mlstm_core(*proj_meta, bias, head_gain0, ct_zero, m_zero, batch=1,
                                          chunks_per_step=1, name="mlstm_core_meta")
    h_meta = _post_mlp(h_meta, y_meta, tm=META_ROWS, name="mlstm_out_mlp_meta", **mlp0)

    proj = _norm_proj(h, tm=PROJ_TM, name="mlstm_in_proj", **in0)
    y, _, _ = _mlstm_core(*proj, bias, head_gain0, ct_meta[0], m_meta[0], batch=batch,
                          chunks_per_step=4, name="mlstm_core")
    h = _post_mlp(h, y, tm=MLP_TM, name="mlstm_out_mlp", **mlp0)

    lambda_init = 0.8 - 0.6 * math.exp(-0.3 * 1)
    w1 = diff_w_in[0]
    w1_parts = [w1[:, :D_MODEL].astype(BF16), w1[:, D_MODEL:2 * D_MODEL].astype(BF16),
                w1[:, 2 * D_MODEL:].T.astype(BF16)]
    in1 = dict(gain=norm_gains[1, 0][None], weights=w1_parts, scales=(DQK ** -0.5 * math.log2(math.e), 1.0, 1.0),
               transposed=(False, False, True), sigmoid=(False,) * 3, dtypes=(BF16, BF16, BF16))
    _, k_meta, vt_meta = _norm_proj(h_meta, tm=META_ROWS, name="diff_in_proj_meta", **in1)
    k_meta = k_meta[META_ROWS - ATT_META_ROWS:]
    vt_meta = vt_meta[:, META_ROWS - ATT_META_ROWS:]
    q, k, vt = _norm_proj(h, tm=PROJ_TM, name="diff_in_proj", **in1)
    y = _diff_attention(q, k, vt, k_meta, vt_meta, diff_lambda[0], diff_head_gain[0][None], batch, lambda_init)
    out = _post_mlp(h, y, diff_w_out[0].astype(BF16), norm_gains[1, 1][None], mlp_w_up[1].astype(BF16),
                    mlp_w_down[1].astype(BF16), final_gain[None], final_norm=True, tm=MLP_TM, name="diff_out_mlp")
    return out.reshape(batch, seq, D_MODEL)
```

```python
import functools
import math

import jax
import jax.numpy as jnp
from jax import lax
from jax.experimental import pallas as pl
from jax.experimental.pallas import tpu as pltpu

D_MODEL = 1024
D_FF = 4 * D_MODEL
N_META = 16
CHUNK = 64
N_HEADS = 8
DQK = 64
DV = 128
EPS = 1e-6
SUBLN_EPS = 1e-5
LOG2E = math.log2(math.e)

LANES = 128
BF16_ROWS = 16
AUG_ROWS = BF16_ROWS
VAUG_ROWS = DV + AUG_ROWS
GATE_ROWS = 2 * N_HEADS

MCHUNK = 256
MLSTM_CHUNKS_PER_STEP = 8
MLSTM_GROUP = 1
META_ROWS = MCHUNK
ATT_META_ROWS = CHUNK
ATT_META_PAD = ATT_META_ROWS - N_META
ATT_TQ = 512
ATT_TK = 256
ATT_AHEAD = 5

PROJ_TM = 1024
PROJ_SUB = 512
MLP_TM = 512

VMEM_LIMIT = 56 * 1024 * 1024

F32 = jnp.float32
BF16 = jnp.bfloat16
NT_DIMS = (((1,), (1,)), ((), ()))


def _compiler_params(semantics):
    return pltpu.CompilerParams(dimension_semantics=semantics, vmem_limit_bytes=VMEM_LIMIT)


def _resident(shape):
    return pl.BlockSpec(shape, lambda *_: (0,) * len(shape), pipeline_mode=pl.Buffered(1))


def _ones_rows(width):
    return (lax.broadcasted_iota(jnp.int32, (AUG_ROWS, width), 0) == 0).astype(BF16)


def _cast_specs(arrays, steps):
    in_specs, out_specs, out_shapes = [], [], []
    for a in arrays:
        rows, cols = a.shape[0] // steps, a.shape[1]
        assert rows * steps == a.shape[0] and rows % BF16_ROWS == 0
        in_specs.append(pl.BlockSpec((rows, cols), lambda i: (i, 0)))
        out_specs.append(pl.BlockSpec((rows, cols), lambda i: (i, 0)))
        out_shapes.append(jax.ShapeDtypeStruct(a.shape, BF16))
    return in_specs, out_specs, out_shapes


def _cast_slabs(src_refs, dst_refs):
    for src, dst in zip(src_refs, dst_refs):
        dst[...] = src[...].astype(dst.dtype)


def _norm_proj_kernel(x_ref, g_ref, *refs, scales, transposed, sigmoid):
    n_out = len(scales)
    n_cast = (len(refs) - 2 * n_out) // 2
    w_refs, out_refs = refs[:n_out], refs[n_out + n_cast:2 * n_out + n_cast]
    _cast_slabs(refs[n_out:n_out + n_cast], refs[2 * n_out + n_cast:])
    tm = x_ref.shape[0]
    sub = min(PROJ_SUB, tm)
    for r0 in range(0, tm, sub):
        rows = slice(r0, r0 + sub)
        x = x_ref[rows, :]
        ms = jnp.mean(x * x, axis=-1, keepdims=True)
        xn = (x * lax.rsqrt(ms + EPS) * g_ref[...]).astype(BF16)
        for w_ref, o_ref, sc, tr, sg in zip(w_refs, out_refs, scales, transposed, sigmoid):
            if tr:
                r = lax.dot_general(w_ref[...], xn, NT_DIMS, preferred_element_type=F32)
            else:
                r = jnp.dot(xn, w_ref[...], preferred_element_type=F32)
            if sc != 1.0:
                r = r * sc
            if sg:
                r = 0.5 * jnp.tanh(0.5 * r) + 0.5
            if tr:
                o_ref[:, rows] = r.astype(o_ref.dtype)
            else:
                o_ref[rows, :] = r.astype(o_ref.dtype)


def _norm_proj(h, gain, weights, scales, transposed, sigmoid, dtypes, tm, name, to_cast=()):
    t = h.shape[0]
    assert t % tm == 0 and tm % min(PROJ_SUB, tm) == 0
    cast_in, cast_out, cast_shapes = _cast_specs(to_cast, t // tm)
    w_specs, out_specs, out_shapes = [], [], []
    for w, tr, dt in zip(weights, transposed, dtypes):
        w_specs.append(_resident(w.shape))
        if tr:
            n = w.shape[0]
            out_specs.append(pl.BlockSpec((n, tm), lambda i: (0, i)))
            out_shapes.append(jax.ShapeDtypeStruct((n, t), dt))
        else:
            n = w.shape[1]
            out_specs.append(pl.BlockSpec((tm, n), lambda i: (i, 0)))
            out_shapes.append(jax.ShapeDtypeStruct((t, n), dt))
    return pl.pallas_call(
        functools.partial(_norm_proj_kernel, scales=scales, transposed=transposed, sigmoid=sigmoid),
        grid=(t // tm,),
        in_specs=[pl.BlockSpec((tm, D_MODEL), lambda i: (i, 0)), _resident((1, D_MODEL))] + w_specs + cast_in,
        out_specs=out_specs + cast_out,
        out_shape=out_shapes + cast_shapes,
        compiler_params=_compiler_params(("parallel",)),
        name=name,
    )(h, gain, *weights, *to_cast)


def _mlstm_kernel(*refs, chunks_per_step, group):
    q_ref, k_ref, o_ref = refs[0:3]
    vt_refs = refs[3:3 + group]
    gt_refs = refs[3 + group:3 + 2 * group]
    (bias_ref, gain_ref, ct0_ref, m0_ref, y_ref, ct_out_ref, m_out_ref,
     ct_ref, m_ref, b_ref, r_ref, cm_ref, rt_ref) = refs[3 + 2 * group:]
    t = MCHUNK

    @pl.when(pl.program_id(1) == 0)
    def _():
        for g in range(group):
            ct_ref[g] = ct0_ref[...]
            m_ref[g] = m0_ref[...]

    neg_inf = jnp.float32(-jnp.inf)
    causal = (lax.broadcasted_iota(jnp.int32, (t, t), 0)
              <= lax.broadcasted_iota(jnp.int32, (t, t), 1))
    upper = causal.astype(BF16)
    lane_t = lax.broadcasted_iota(jnp.int32, (N_HEADS, t), 1)
    bias = bias_ref[...]

    for g in range(group):
        for c in range(chunks_per_step):
            gates = gt_refs[g][:, c * t:(c + 1) * t] + bias
            ig = gates[0:N_HEADS]
            lf = jax.nn.log_sigmoid(gates[N_HEADS:])
            hi = lf.astype(BF16).astype(F32)
            mid = (lf - hi).astype(BF16).astype(F32)
            lo = (lf - hi - mid).astype(BF16).astype(F32)
            parts = jnp.dot(jnp.concatenate([hi, mid, lo], axis=0).astype(BF16), upper,
                            preferred_element_type=F32)
            b = parts[0:N_HEADS] + parts[N_HEADS:2 * N_HEADS] + parts[2 * N_HEADS:]
            r = ig - b
            cm = r
            shift = 1
            while shift < t:
                cm = jnp.maximum(cm, jnp.where(lane_t >= shift, pltpu.roll(cm, shift, 1), neg_inf))
                shift *= 2
            b_ref[g, c] = b
            r_ref[g, c] = r
            cm_ref[g, c] = cm
            rt_ref[g, c] = (r * LOG2E).T

    low = lax.broadcasted_iota(jnp.int32, (t, LANES), 1) < DQK
    low_state = lax.broadcasted_iota(jnp.int32, (VAUG_ROWS, LANES), 1) < DQK
    ones_rows = _ones_rows(t)
    half = t // 2
    tri = causal[:half, :half]

    def chunk_body(c, carry):
        rows = pl.ds(pl.multiple_of(c * t, t), t)
        row_terms = []
        for g in range(group):
            b, r, cm = b_ref[g, c], r_ref[g, c], cm_ref[g, c]
            m_prev = m_ref[g, :, 0:1]
            u = jnp.maximum(m_prev, cm)
            b_end = b[:, t - 1:t]
            a = b_end + r
            m_new = jnp.maximum(b_end + m_prev, jnp.max(a, axis=1, keepdims=True))
            row_terms.append(dict(
                inter=jnp.exp(m_prev - u), e_row=jnp.exp(-(b + u)), u2=u * LOG2E,
                decay=jnp.exp(b_end + m_prev - m_new), w=jnp.exp(a - m_new).astype(BF16), rt=rt_ref[g, c]))
            m_ref[g] = jnp.broadcast_to(m_new, m_ref.shape[1:])

        def pair_scores(g, p):
            q2 = q_ref[g, rows, p * LANES:(p + 1) * LANES]
            k2 = k_ref[g, rows, p * LANES:(p + 1) * LANES]
            zero = jnp.zeros_like(q2)
            qs = jnp.concatenate([jnp.where(low, q2, zero), jnp.where(low, zero, q2)], axis=0)
            return q2, k2, lax.dot_general(k2, qs, NT_DIMS, preferred_element_type=F32)

        seq = [(g, p) for p in range(N_HEADS // 2) for g in range(group)]
        nxt = pair_scores(*seq[0])
        for n, (g, p) in enumerate(seq):
            q2, k2, st = nxt
            terms = row_terms[g]
            heads = (2 * p, 2 * p + 1)
            cts = [ct_ref[g, h] for h in heads]
            vaugs = [jnp.concatenate([vt_refs[g][h * DV:(h + 1) * DV, rows], ones_rows], axis=0) for h in heads]
            inter_nums = [lax.dot_general(ct.astype(BF16), q2, NT_DIMS, preferred_element_type=F32)
                          for ct in cts]
            for i, h in enumerate(heads):
                wv = vaugs[i] * terms["w"][h:h + 1, :]
                upd = jnp.dot(wv, k2, preferred_element_type=F32)
                own = low_state if i == 0 else jnp.logical_not(low_state)
                ct_ref[g, h] = terms["decay"][h:h + 1, :] * cts[i] + jnp.where(own, upd, 0.0)
            if n + 1 < len(seq):
                nxt = pair_scores(*seq[n + 1])
            for i, h in enumerate(heads):
                cols = slice(h * DV, (h + 1) * DV)
                r_col, u_row = terms["rt"][:, h:h + 1], terms["u2"][h:h + 1, :]
                st_h = st[:, i * t:(i + 1) * t]
                d00 = jnp.exp2(jnp.where(tri, r_col[:half] - u_row[:, :half], neg_inf))
                d01 = jnp.exp2(r_col[:half] - u_row[:, half:])
                d11 = jnp.exp2(jnp.where(tri, r_col[half:] - u_row[:, half:], neg_inf))
                s_h = jnp.concatenate(
                    [jnp.concatenate([st_h[:half, :half] * d00, st_h[:half, half:] * d01], axis=1),
                     jnp.concatenate([jnp.zeros((half, half), F32), st_h[half:, half:] * d11], axis=1)],
                    axis=0).astype(BF16)
                num = (terms["inter"][h:h + 1, :] * inter_nums[i]
                       + jnp.dot(vaugs[i], s_h, preferred_element_type=F32))
                inv = 1.0 / jnp.maximum(jnp.abs(num[DV:DV + 1, :]), terms["e_row"][h:h + 1, :])
                hh = num[0:DV, :] * inv
                hh = hh * lax.rsqrt(jnp.mean(hh * hh, axis=0, keepdims=True) + EPS)
                y_ref[g, rows, cols] = (hh.T * gain_ref[:, cols]).astype(BF16) * o_ref[g, rows, cols]
        return carry

    lax.fori_loop(0, chunks_per_step, chunk_body, 0)
    ct_out_ref[...] = ct_ref[...]
    m_out_ref[...] = m_ref[...]


def _mlstm_core(q, k, vt, o, gates_t, bias, gain, ct0, m0, batch, chunks_per_step, group, name):
    tokens = q.shape[0]
    rows = chunks_per_step * MCHUNK
    per_group = batch // group
    steps = tokens // batch // rows
    assert per_group * group == batch and batch * steps * rows == tokens
    ct_shape = (N_HEADS, VAUG_ROWS, LANES)
    m_shape = (N_HEADS, LANES)
    gate_shape = (group, chunks_per_step, N_HEADS, MCHUNK)

    def tok(width):
        return pl.BlockSpec((group, rows, width), lambda b, j: (0, b * steps + j, 0))

    def tok_t(height, g):
        return pl.BlockSpec((height, rows), lambda b, j, g=g: (0, (g * per_group + b) * steps + j))

    def grouped(x):
        return x.reshape(group, tokens // group, x.shape[1])

    y, ct, m = pl.pallas_call(
        functools.partial(_mlstm_kernel, chunks_per_step=chunks_per_step, group=group),
        grid=(per_group, steps),
        in_specs=([tok(N_HEADS * DQK), tok(N_HEADS * DQK), tok(N_HEADS * DV)]
                  + [tok_t(N_HEADS * DV, g) for g in range(group)]
                  + [tok_t(GATE_ROWS, g) for g in range(group)]
                  + [_resident((2 * N_HEADS, 1)), _resident((1, N_HEADS * DV)),
                     _resident(ct_shape), _resident(m_shape)]),
        out_specs=[tok(N_HEADS * DV),
                   pl.BlockSpec((group, None) + ct_shape, lambda b, j: (0, b, 0, 0, 0)),
                   pl.BlockSpec((group, None) + m_shape, lambda b, j: (0, b, 0, 0))],
        out_shape=[jax.ShapeDtypeStruct((group, tokens // group, N_HEADS * DV), BF16),
                   jax.ShapeDtypeStruct((group, per_group) + ct_shape, F32),
                   jax.ShapeDtypeStruct((group, per_group) + m_shape, F32)],
        scratch_shapes=[pltpu.VMEM((group,) + ct_shape, F32), pltpu.VMEM((group,) + m_shape, F32),
                        pltpu.VMEM(gate_shape, F32), pltpu.VMEM(gate_shape, F32), pltpu.VMEM(gate_shape, F32),
                        pltpu.VMEM((group, chunks_per_step, MCHUNK, N_HEADS), F32)],
        compiler_params=_compiler_params(("parallel", "arbitrary")),
        name=name,
    )(grouped(q), grouped(k), grouped(o), *([vt] * group), *([gates_t] * group), bias, gain, ct0, m0)
    return y.reshape(tokens, N_HEADS * DV), ct, m


def _diff_attn_kernel(lam_ref, q_ref, k_ref, vt_ref, km_ref, vtm_ref, gain_ref, o_ref,
                      qs_ref, m_ref, acc_ref, sc_ref, *, lambda_init):
    qi = pl.program_id(1)
    tq, tk = ATT_TQ, ATT_TK
    blocks = tq // tk
    neg_inf = jnp.float32(-jnp.inf)

    low = lax.broadcasted_iota(jnp.int32, (tq, LANES), 1) < DQK
    for h in range(N_HEADS):
        q2 = q_ref[:, h * DV:(h + 1) * DV]
        zero = jnp.zeros_like(q2)
        qs_ref[h] = jnp.concatenate([jnp.where(low, q2, zero), jnp.where(low, zero, q2)], axis=0)

    all_units = [(h, br, blk) for h in range(N_HEADS) for br in range(2) for blk in range(blocks)]

    def unit_cols(br, blk):
        return pl.ds((br * blocks + blk) * tk, tk)

    def run(items, primed, next_items):
        stream = items + next_items

        def scores(item):
            tile, (h, br, blk) = item
            s = lax.dot_general(tile["k"](h), qs_ref[h, unit_cols(br, blk), :], NT_DIMS,
                                preferred_element_type=F32)
            bias = tile["bias"](blk)
            return s if bias is None else s + bias

        pending = [sc_ref[n] if primed else scores(stream[n]) for n in range(ATT_AHEAD)]
        for n, (tile, (h, br, blk)) in enumerate(items):
            s = pending.pop(0)
            ahead = n + ATT_AHEAD
            if ahead < len(items):
                pending.append(scores(stream[ahead]))
            elif ahead < len(stream):
                sc_ref[ahead - len(items)] = scores(stream[ahead])
            cols = unit_cols(br, blk)
            v_aug = jnp.concatenate([tile["vt"](h), tile["ones"]], axis=0)
            s_max = jnp.max(s, axis=0, keepdims=True)
            if tile["first"]:
                m_new = s_max
                acc_ref[h, :, cols] = jnp.dot(v_aug, jnp.exp2(s - m_new).astype(BF16), preferred_element_type=F32)
            else:
                m_old = m_ref[h, :, cols]
                m_new = jnp.maximum(m_old, s_max)
                alpha = jnp.exp2(m_old - m_new)
                p = jnp.exp2(s - m_new).astype(BF16)
                acc_ref[h, :, cols] = alpha * acc_ref[h, :, cols] + jnp.dot(v_aug, p, preferred_element_type=F32)
            m_ref[h, :, cols] = m_new
        assert not pending and len(items) >= ATT_AHEAD

    ones_tile = _ones_rows(tk)

    def frame_tile(start, bias_of):
        return dict(k=lambda h: k_ref[pl.ds(start, tk), h * DV:(h + 1) * DV],
                    vt=lambda h: vt_ref[h * DV:(h + 1) * DV, pl.ds(start, tk)],
                    ones=ones_tile, bias=bias_of, first=False)

    def full_items(start, units):
        tile = frame_tile(start, lambda blk: None)
        return [(tile, u) for u in units]

    chunk_shift = CHUNK.bit_length() - 1
    k_chunk = lax.shift_right_logical(lax.broadcasted_iota(jnp.int32, (tk, tk), 0), chunk_shift)
    q_chunk = lax.shift_right_logical(lax.broadcasted_iota(jnp.int32, (tk, tk), 1), chunk_shift)
    diag_bias = jnp.where(k_chunk <= q_chunk, 0.0, neg_inf)
    meta_row = lax.broadcasted_iota(jnp.int32, (ATT_META_ROWS, tk), 0)
    meta_bias = jnp.where(meta_row >= ATT_META_PAD, 0.0, neg_inf)
    first_bias = [jnp.concatenate([diag_bias if blk == 0 else jnp.zeros_like(diag_bias), meta_bias], axis=0)
                  for blk in range(min(blocks, 2))]
    start0 = pl.multiple_of(qi * tq, tk)
    first_tile = dict(
        k=lambda h: jnp.concatenate([k_ref[pl.ds(start0, tk), h * DV:(h + 1) * DV],
                                     km_ref[:, h * DV:(h + 1) * DV]], axis=0),
        vt=lambda h: jnp.concatenate([vt_ref[h * DV:(h + 1) * DV, pl.ds(start0, tk)],
                                      vtm_ref[h * DV:(h + 1) * DV, :]], axis=1),
        ones=_ones_rows(tk + ATT_META_ROWS), bias=lambda blk: first_bias[min(blk, 1)], first=True)
    items = [(first_tile, u) for u in all_units]
    for d in range(1, blocks):
        tile = frame_tile(pl.multiple_of(qi * tq + d * tk, tk), lambda blk, d=d: diag_bias if blk == d else None)
        items += [(tile, u) for u in all_units if u[2] >= d]
    head_units = all_units[:ATT_AHEAD]
    run(items, primed=False, next_items=full_items(0, head_units))

    n_full = qi * blocks
    last_tile = k_ref.shape[0] // tk - 1

    def full_tile(j, carry):
        nxt = pl.multiple_of(jnp.minimum(j + 1, last_tile) * tk, tk)
        run(full_items(pl.multiple_of(j * tk, tk), all_units), primed=True,
            next_items=full_items(nxt, head_units))
        return carry

    lax.fori_loop(0, n_full, full_tile, 0)

    lv = lam_ref[...]
    lam = (jnp.exp(jnp.sum(lv[0:1] * lv[1:2], axis=-1, keepdims=True))
           - jnp.exp(jnp.sum(lv[2:3] * lv[3:4], axis=-1, keepdims=True)) + lambda_init)
    out_gain = gain_ref[...] * (1.0 - lambda_init)
    for h in range(N_HEADS):
        cols = slice(h * DV, (h + 1) * DV)
        acc = acc_ref[h]
        inv_l = 1.0 / acc[DV:DV + 1, :]
        o_t = acc[0:DV, :tq] * inv_l[:, :tq] - lam * (acc[0:DV, tq:] * inv_l[:, tq:])
        o_t = o_t * lax.rsqrt(jnp.mean(o_t * o_t, axis=0, keepdims=True) + SUBLN_EPS)
        o_ref[:, cols] = (o_t.T * out_gain[:, cols]).astype(o_ref.dtype)


def _diff_attention(q, k, vt, k_meta, vt_meta, lam_vecs, gain, batch, lambda_init):
    t = q.shape[0]
    seq = t // batch
    nq = seq // ATT_TQ
    assert nq * ATT_TQ == seq and ATT_TQ % ATT_TK == 0 and ATT_TK % CHUNK == 0
    width = N_HEADS * DV
    return pl.pallas_call(
        functools.partial(_diff_attn_kernel, lambda_init=lambda_init),
        grid=(batch, nq),
        in_specs=[_resident(lam_vecs.shape),
                  pl.BlockSpec((ATT_TQ, width), lambda b, i: (b * nq + i, 0)),
                  pl.BlockSpec((seq, width), lambda b, i: (b, 0)),
                  pl.BlockSpec((width, seq), lambda b, i: (0, b)),
                  _resident(k_meta.shape),
                  _resident(vt_meta.shape),
                  _resident((1, width))],
        out_specs=pl.BlockSpec((ATT_TQ, width), lambda b, i: (b * nq + i, 0)),
        out_shape=jax.ShapeDtypeStruct((t, width), BF16),
        scratch_shapes=[pltpu.VMEM((N_HEADS, 2 * ATT_TQ, DV), BF16),
                        pltpu.VMEM((N_HEADS, 1, 2 * ATT_TQ), F32),
                        pltpu.VMEM((N_HEADS, VAUG_ROWS, 2 * ATT_TQ), F32),
                        pltpu.VMEM((ATT_AHEAD, ATT_TK, ATT_TK), F32)],
        compiler_params=_compiler_params(("parallel", "arbitrary")),
        name="diff_attention",
    )(lam_vecs, q, k, vt, k_meta, vt_meta, gain)


def _post_mlp_kernel(h_ref, y_ref, wo_ref, g_ref, wu_ref, wd_ref, fg_ref, *refs, final_norm, ff_block):
    n_cast = (len(refs) - 1) // 2
    out_ref = refs[n_cast]
    _cast_slabs(refs[:n_cast], refs[n_cast + 1:])
    h1 = h_ref[...] + jnp.dot(y_ref[...], wo_ref[...], preferred_element_type=F32)
    ms = jnp.mean(h1 * h1, axis=-1, keepdims=True)
    hn = (h1 * lax.rsqrt(ms + EPS) * g_ref[...]).astype(BF16)
    acc = h1
    for c in range(D_FF // ff_block):
        blk = slice(c * ff_block, (c + 1) * ff_block)
        u = jnp.dot(hn, wu_ref[:, blk], preferred_element_type=F32)
        a = jnp.square(jnp.maximum(u, 0.0)).astype(BF16)
        acc = acc + jnp.dot(a, wd_ref[blk, :], preferred_element_type=F32)
    if final_norm:
        ms = jnp.mean(acc * acc, axis=-1, keepdims=True)
        acc = acc * lax.rsqrt(ms + EPS) * fg_ref[...]
    out_ref[...] = acc


def _post_mlp(h, y, w_out, gain, w_up, w_down, final_gain, final_norm, tm, name, to_cast=()):
    t = h.shape[0]
    assert t % tm == 0
    cast_in, cast_out, cast_shapes = _cast_specs(to_cast, t // tm)
    return pl.pallas_call(
        functools.partial(_post_mlp_kernel, final_norm=final_norm, ff_block=1024),
        grid=(t // tm,),
        in_specs=[pl.BlockSpec((tm, D_MODEL), lambda i: (i, 0)),
                  pl.BlockSpec((tm, D_MODEL), lambda i: (i, 0)),
                  _resident((D_MODEL, D_MODEL)),
                  _resident((1, D_MODEL)),
                  _resident((D_MODEL, D_FF)),
                  _resident((D_FF, D_MODEL)),
                  _resident((1, D_MODEL))] + cast_in,
        out_specs=[pl.BlockSpec((tm, D_MODEL), lambda i: (i, 0))] + cast_out,
        out_shape=[jax.ShapeDtypeStruct((t, D_MODEL), F32)] + cast_shapes,
        compiler_params=_compiler_params(("parallel",)),
        name=name,
    )(h, y, w_out, gain, w_up, w_down, final_gain, *to_cast)


def kernel(x, meta_tokens, norm_gains, mlstm_w_in, mlstm_b_gate, mlstm_head_gain, mlstm_w_out,
           diff_w_in, diff_lambda, diff_head_gain, diff_w_out, mlp_w_up, mlp_w_down, final_gain):
    batch, seq, _ = x.shape
    t = batch * seq
    h = x.reshape(t, D_MODEL)
    h_meta = jnp.concatenate([jnp.zeros((META_ROWS - N_META, D_MODEL), x.dtype), meta_tokens.astype(x.dtype)],
                             axis=0)

    m_qk = N_HEADS * DQK
    m_v = N_HEADS * DV
    w0 = mlstm_w_in[0]
    offs = (0, m_qk, 2 * m_qk, 2 * m_qk + m_v, 2 * m_qk + 2 * m_v)
    w0_parts = [w0[:, offs[3]:offs[4]], w0[:, offs[2]:offs[3]].T, w0[:, offs[0]:offs[1]], w0[:, offs[1]:offs[2]],
                w0[:, offs[4]:].T]
    w0_parts = [w.astype(BF16) for w in w0_parts]
    in0 = dict(gain=norm_gains[0, 0][None], weights=w0_parts, scales=(1.0, 1.0, DQK ** -0.5, 1.0, 1.0),
               transposed=(False, True, False, False, True), sigmoid=(True, False, False, False, False),
               dtypes=(BF16, BF16, BF16, BF16, F32))
    bias = mlstm_b_gate[0][:, None]
    head_gain0 = mlstm_head_gain[0][None]

    o_gate, vt, q, k, gates_t = _norm_proj(h_meta, tm=META_ROWS, name="mlstm_in_proj_meta", **in0)
    ct_zero = jnp.zeros((N_HEADS, VAUG_ROWS, LANES), F32)
    m_zero = jnp.zeros((N_HEADS, LANES), F32)
    y_meta, ct_meta, m_meta = _mlstm_core(q, k, vt, o_gate, gates_t, bias, head_gain0, ct_zero, m_zero, batch=1,
                                          chunks_per_step=1, group=1, name="mlstm_core_meta")

    o_gate, vt, q, k, gates_t, w_out0, w_up0, w_down0 = _norm_proj(
        h, tm=PROJ_TM, name="mlstm_in_proj", to_cast=(mlstm_w_out[0], mlp_w_up[0], mlp_w_down[0]), **in0)
    mlp0 = dict(w_out=w_out0, gain=norm_gains[0, 1][None], w_up=w_up0, w_down=w_down0,
                final_gain=final_gain[None], final_norm=False)
    h_meta, = _post_mlp(h_meta, y_meta, tm=META_ROWS, name="mlstm_out_mlp_meta", **mlp0)
    y, _, _ = _mlstm_core(q, k, vt, o_gate, gates_t, bias, head_gain0, ct_meta[0, 0], m_meta[0, 0], batch=batch,
                          chunks_per_step=MLSTM_CHUNKS_PER_STEP, group=MLSTM_GROUP, name="mlstm_core")
    w1 = diff_w_in[0]
    layer1_f32 = (w1[:, :D_MODEL], w1[:, D_MODEL:2 * D_MODEL], w1[:, 2 * D_MODEL:].T,
                  diff_w_out[0], mlp_w_up[1], mlp_w_down[1])
    h, w_q1, w_k1, w_vt1, w_out1, w_up1, w_down1 = _post_mlp(h, y, tm=MLP_TM, name="mlstm_out_mlp",
                                                             to_cast=layer1_f32, **mlp0)

    lambda_init = 0.8 - 0.6 * math.exp(-0.3 * 1)
    in1 = dict(gain=norm_gains[1, 0][None], weights=[w_q1, w_k1, w_vt1], scales=(DQK ** -0.5 * LOG2E, 1.0, 1.0),
               transposed=(False, False, True), sigmoid=(False,) * 3, dtypes=(BF16, BF16, BF16))
    _, k_meta, vt_meta = _norm_proj(h_meta, tm=META_ROWS, name="diff_in_proj_meta", **in1)
    k_meta = k_meta[META_ROWS - ATT_META_ROWS:]
    vt_meta = vt_meta[:, META_ROWS - ATT_META_ROWS:]
    q, k, vt = _norm_proj(h, tm=PROJ_TM, name="diff_in_proj", **in1)
    y = _diff_attention(q, k, vt, k_meta, vt_meta, diff_lambda[0], diff_head_gain[0][None], batch, lambda_init)
    out, = _post_mlp(h, y, w_out1, norm_gains[1, 1][None], w_up1, w_down1, final_gain[None], final_norm=True,
                     tm=MLP_TM, name="diff_out_mlp")
    return out.reshape(batch, seq, D_MODEL)
```

```python
import functools
import math

import jax
import jax.numpy as jnp
from jax import lax
from jax.experimental import pallas as pl
from jax.experimental.pallas import tpu as pltpu

D_MODEL = 1024
D_FF = 4 * D_MODEL
N_META = 16
CHUNK = 64
N_HEADS = 8
DQK = 64
DV = 128
EPS = 1e-6
SUBLN_EPS = 1e-5
LOG2E = math.log2(math.e)

LANES = 128
BF16_ROWS = 16
AUG_ROWS = BF16_ROWS
VAUG_ROWS = DV + AUG_ROWS
GATE_ROWS = 2 * N_HEADS

MCHUNK = 256
MLSTM_CHUNKS_PER_STEP = 8
MLSTM_GROUP = 1
META_ROWS = MCHUNK
ATT_META_ROWS = CHUNK
ATT_META_PAD = ATT_META_ROWS - N_META
ATT_TQ = 512
ATT_TK = 256
ATT_AHEAD = 5

PROJ_TM = 1024
PROJ_SUB = 512
MLP_TM = 512

VMEM_LIMIT = 56 * 1024 * 1024

F32 = jnp.float32
BF16 = jnp.bfloat16
NT_DIMS = (((1,), (1,)), ((), ()))


def _compiler_params(semantics):
    return pltpu.CompilerParams(dimension_semantics=semantics, vmem_limit_bytes=VMEM_LIMIT)


def _resident(shape):
    return pl.BlockSpec(shape, lambda *_: (0,) * len(shape), pipeline_mode=pl.Buffered(1))


def _ones_rows(width):
    return (lax.broadcasted_iota(jnp.int32, (AUG_ROWS, width), 0) == 0).astype(BF16)


def _cast_job(stacked, layer, col_block=0, n_col_blocks=1, transpose=False):
    return dict(array=stacked, layer=layer, col_block=col_block, n_col_blocks=n_col_blocks, transpose=transpose)


def _cast_specs(jobs, steps):
    in_specs, out_specs, out_shapes = [], [], []
    for job in jobs:
        _, n_rows, n_cols = job["array"].shape
        cols = n_cols // job["n_col_blocks"]
        layer, col_block = job["layer"], job["col_block"]
        if job["transpose"]:
            per_slab = steps * LANES // n_rows
            assert per_slab * n_rows == steps * LANES
            in_specs.append(pl.BlockSpec((None, LANES, cols),
                                         lambda i, l=layer, c=col_block, p=per_slab: (l, i // p, c)))
            out_specs.append(pl.BlockSpec((cols, LANES), lambda i, p=per_slab: (0, i // p)))
            out_shapes.append(jax.ShapeDtypeStruct((cols, n_rows), BF16))
        else:
            rows = n_rows // steps
            assert rows * steps == n_rows and rows % BF16_ROWS == 0
            in_specs.append(pl.BlockSpec((None, rows, cols), lambda i, l=layer, c=col_block: (l, i, c)))
            out_specs.append(pl.BlockSpec((rows, cols), lambda i: (i, 0)))
            out_shapes.append(jax.ShapeDtypeStruct((n_rows, cols), BF16))
    return in_specs, out_specs, out_shapes


def _cast_slabs(src_refs, dst_refs, transposed):
    for src, dst, tr in zip(src_refs, dst_refs, transposed):
        x = src[...]
        dst[...] = (x.T if tr else x).astype(dst.dtype)


def _norm_proj_kernel(x_ref, g_ref, *refs, scales, transposed, sigmoid, cast_transposed):
    n_out, n_cast = len(scales), len(cast_transposed)
    w_refs, out_refs = refs[:n_out], refs[n_out + n_cast:2 * n_out + n_cast]
    _cast_slabs(refs[n_out:n_out + n_cast], refs[2 * n_out + n_cast:], cast_transposed)
    tm = x_ref.shape[0]
    sub = min(PROJ_SUB, tm)
    for r0 in range(0, tm, sub):
        rows = slice(r0, r0 + sub)
        x = x_ref[rows, :]
        ms = jnp.mean(x * x, axis=-1, keepdims=True)
        xn = (x * lax.rsqrt(ms + EPS) * g_ref[...]).astype(BF16)
        for w_ref, o_ref, sc, tr, sg in zip(w_refs, out_refs, scales, transposed, sigmoid):
            if tr:
                r = lax.dot_general(w_ref[...], xn, NT_DIMS, preferred_element_type=F32)
            else:
                r = jnp.dot(xn, w_ref[...], preferred_element_type=F32)
            if sc != 1.0:
                r = r * sc
            if sg:
                r = 0.5 * jnp.tanh(0.5 * r) + 0.5
            if tr:
                o_ref[:, rows] = r.astype(o_ref.dtype)
            else:
                o_ref[rows, :] = r.astype(o_ref.dtype)


def _norm_proj(h, gain, weights, scales, transposed, sigmoid, dtypes, tm, name, to_cast=()):
    t = h.shape[0]
    assert t % tm == 0 and tm % min(PROJ_SUB, tm) == 0
    cast_in, cast_out, cast_shapes = _cast_specs(to_cast, t // tm)
    w_specs, out_specs, out_shapes = [], [], []
    for w, tr, dt in zip(weights, transposed, dtypes):
        w_specs.append(_resident(w.shape))
        if tr:
            n = w.shape[0]
            out_specs.append(pl.BlockSpec((n, tm), lambda i: (0, i)))
            out_shapes.append(jax.ShapeDtypeStruct((n, t), dt))
        else:
            n = w.shape[1]
            out_specs.append(pl.BlockSpec((tm, n), lambda i: (i, 0)))
            out_shapes.append(jax.ShapeDtypeStruct((t, n), dt))
    return pl.pallas_call(
        functools.partial(_norm_proj_kernel, scales=scales, transposed=transposed, sigmoid=sigmoid,
                          cast_transposed=tuple(job["transpose"] for job in to_cast)),
        grid=(t // tm,),
        in_specs=[pl.BlockSpec((tm, D_MODEL), lambda i: (i, 0)), _resident((1, D_MODEL))] + w_specs + cast_in,
        out_specs=out_specs + cast_out,
        out_shape=out_shapes + cast_shapes,
        compiler_params=_compiler_params(("arbitrary",)),
        name=name,
    )(h, gain, *weights, *[job["array"] for job in to_cast])


def _mlstm_kernel(*refs, chunks_per_step, group):
    q_ref, k_ref, o_ref = refs[0:3]
    vt_refs = refs[3:3 + group]
    gt_refs = refs[3 + group:3 + 2 * group]
    (bias_ref, gain_ref, ct0_ref, m0_ref, y_ref, ct_out_ref, m_out_ref,
     ct_ref, m_ref, b_ref, r_ref, cm_ref, rt_ref) = refs[3 + 2 * group:]
    t = MCHUNK

    @pl.when(pl.program_id(1) == 0)
    def _():
        for g in range(group):
            ct_ref[g] = ct0_ref[...]
            m_ref[g] = m0_ref[...]

    neg_inf = jnp.float32(-jnp.inf)
    causal = (lax.broadcasted_iota(jnp.int32, (t, t), 0)
              <= lax.broadcasted_iota(jnp.int32, (t, t), 1))
    upper = causal.astype(BF16)
    lane_t = lax.broadcasted_iota(jnp.int32, (N_HEADS, t), 1)
    bias = bias_ref[...]

    for g in range(group):
        for c in range(chunks_per_step):
            gates = gt_refs[g][:, c * t:(c + 1) * t] + bias
            ig = gates[0:N_HEADS]
            lf = jax.nn.log_sigmoid(gates[N_HEADS:])
            hi = lf.astype(BF16).astype(F32)
            mid = (lf - hi).astype(BF16).astype(F32)
            lo = (lf - hi - mid).astype(BF16).astype(F32)
            parts = jnp.dot(jnp.concatenate([hi, mid, lo], axis=0).astype(BF16), upper,
                            preferred_element_type=F32)
            b = parts[0:N_HEADS] + parts[N_HEADS:2 * N_HEADS] + parts[2 * N_HEADS:]
            r = ig - b
            cm = r
            shift = 1
            while shift < t:
                cm = jnp.maximum(cm, jnp.where(lane_t >= shift, pltpu.roll(cm, shift, 1), neg_inf))
                shift *= 2
            b_ref[g, c] = b
            r_ref[g, c] = r
            cm_ref[g, c] = cm
            rt_ref[g, c] = (r * LOG2E).T

    low = lax.broadcasted_iota(jnp.int32, (t, LANES), 1) < DQK
    low_state = lax.broadcasted_iota(jnp.int32, (VAUG_ROWS, LANES), 1) < DQK
    ones_rows = _ones_rows(t)
    half = t // 2
    tri = causal[:half, :half]

    def chunk_body(c, carry):
        rows = pl.ds(pl.multiple_of(c * t, t), t)
        row_terms = []
        for g in range(group):
            b, r, cm = b_ref[g, c], r_ref[g, c], cm_ref[g, c]
            m_prev = m_ref[g, :, 0:1]
            u = jnp.maximum(m_prev, cm)
            b_end = b[:, t - 1:t]
            a = b_end + r
            m_new = jnp.maximum(b_end + m_prev, jnp.max(a, axis=1, keepdims=True))
            row_terms.append(dict(
                inter=jnp.exp(m_prev - u), e_row=jnp.exp(-(b + u)), u2=u * LOG2E,
                decay=jnp.exp(b_end + m_prev - m_new), w=jnp.exp(a - m_new).astype(BF16), rt=rt_ref[g, c]))
            m_ref[g] = jnp.broadcast_to(m_new, m_ref.shape[1:])

        def pair_scores(g, p):
            q2 = q_ref[g, rows, p * LANES:(p + 1) * LANES]
            k2 = k_ref[g, rows, p * LANES:(p + 1) * LANES]
            zero = jnp.zeros_like(q2)
            qs = jnp.concatenate([jnp.where(low, q2, zero), jnp.where(low, zero, q2)], axis=0)
            return q2, k2, lax.dot_general(k2, qs, NT_DIMS, preferred_element_type=F32)

        seq = [(g, p) for p in range(N_HEADS // 2) for g in range(group)]
        nxt = pair_scores(*seq[0])
        for n, (g, p) in enumerate(seq):
            q2, k2, st = nxt
            terms = row_terms[g]
            heads = (2 * p, 2 * p + 1)
            cts = [ct_ref[g, h] for h in heads]
            vaugs = [jnp.concatenate([vt_refs[g][h * DV:(h + 1) * DV, rows], ones_rows], axis=0) for h in heads]
            inter_nums = [lax.dot_general(ct.astype(BF16), q2, NT_DIMS, preferred_element_type=F32)
                          for ct in cts]
            for i, h in enumerate(heads):
                wv = vaugs[i] * terms["w"][h:h + 1, :]
                upd = jnp.dot(wv, k2, preferred_element_type=F32)
                own = low_state if i == 0 else jnp.logical_not(low_state)
                ct_ref[g, h] = terms["decay"][h:h + 1, :] * cts[i] + jnp.where(own, upd, 0.0)
            if n + 1 < len(seq):
                nxt = pair_scores(*seq[n + 1])
            for i, h in enumerate(heads):
                cols = slice(h * DV, (h + 1) * DV)
                r_col, u_row = terms["rt"][:, h:h + 1], terms["u2"][h:h + 1, :]
                st_h = st[:, i * t:(i + 1) * t]
                d00 = jnp.exp2(jnp.where(tri, r_col[:half] - u_row[:, :half], neg_inf))
                d01 = jnp.exp2(r_col[:half] - u_row[:, half:])
                d11 = jnp.exp2(jnp.where(tri, r_col[half:] - u_row[:, half:], neg_inf))
                s_h = jnp.concatenate(
                    [jnp.concatenate([st_h[:half, :half] * d00, st_h[:half, half:] * d01], axis=1),
                     jnp.concatenate([jnp.zeros((half, half), F32), st_h[half:, half:] * d11], axis=1)],
                    axis=0).astype(BF16)
                num = (terms["inter"][h:h + 1, :] * inter_nums[i]
                       + jnp.dot(vaugs[i], s_h, preferred_element_type=F32))
                inv = 1.0 / jnp.maximum(jnp.abs(num[DV:DV + 1, :]), terms["e_row"][h:h + 1, :])
                hh = num[0:DV, :] * inv
                hh = hh * lax.rsqrt(jnp.mean(hh * hh, axis=0, keepdims=True) + EPS)
                y_ref[g, rows, cols] = (hh.T * gain_ref[:, cols]).astype(BF16) * o_ref[g, rows, cols]
        return carry

    lax.fori_loop(0, chunks_per_step, chunk_body, 0)
    ct_out_ref[...] = ct_ref[...]
    m_out_ref[...] = m_ref[...]


def _mlstm_core(q, k, vt, o, gates_t, bias, gain, ct0, m0, batch, chunks_per_step, group, name):
    tokens = q.shape[0]
    rows = chunks_per_step * MCHUNK
    per_group = batch // group
    steps = tokens // batch // rows
    assert per_group * group == batch and batch * steps * rows == tokens
    ct_shape = (N_HEADS, VAUG_ROWS, LANES)
    m_shape = (N_HEADS, LANES)
    gate_shape = (group, chunks_per_step, N_HEADS, MCHUNK)

    def tok(width):
        return pl.BlockSpec((group, rows, width), lambda b, j: (0, b * steps + j, 0))

    def tok_t(height, g):
        return pl.BlockSpec((height, rows), lambda b, j, g=g: (0, (g * per_group + b) * steps + j))

    def grouped(x):
        return x.reshape(group, tokens // group, x.shape[1])

    y, ct, m = pl.pallas_call(
        functools.partial(_mlstm_kernel, chunks_per_step=chunks_per_step, group=group),
        grid=(per_group, steps),
        in_specs=([tok(N_HEADS * DQK), tok(N_HEADS * DQK), tok(N_HEADS * DV)]
                  + [tok_t(N_HEADS * DV, g) for g in range(group)]
                  + [tok_t(GATE_ROWS, g) for g in range(group)]
                  + [_resident((2 * N_HEADS, 1)), _resident((1, N_HEADS * DV)),
                     _resident(ct_shape), _resident(m_shape)]),
        out_specs=[tok(N_HEADS * DV),
                   pl.BlockSpec((group, None) + ct_shape, lambda b, j: (0, b, 0, 0, 0)),
                   pl.BlockSpec((group, None) + m_shape, lambda b, j: (0, b, 0, 0))],
        out_shape=[jax.ShapeDtypeStruct((group, tokens // group, N_HEADS * DV), BF16),
                   jax.ShapeDtypeStruct((group, per_group) + ct_shape, F32),
                   jax.ShapeDtypeStruct((group, per_group) + m_shape, F32)],
        scratch_shapes=[pltpu.VMEM((group,) + ct_shape, F32), pltpu.VMEM((group,) + m_shape, F32),
                        pltpu.VMEM(gate_shape, F32), pltpu.VMEM(gate_shape, F32), pltpu.VMEM(gate_shape, F32),
                        pltpu.VMEM((group, chunks_per_step, MCHUNK, N_HEADS), F32)],
        compiler_params=_compiler_params(("parallel", "arbitrary")),
        name=name,
    )(grouped(q), grouped(k), grouped(o), *([vt] * group), *([gates_t] * group), bias, gain, ct0, m0)
    return y.reshape(tokens, N_HEADS * DV), ct, m


def _diff_attn_kernel(lam_ref, q_ref, k_ref, vt_ref, km_ref, vtm_ref, gain_ref, o_ref,
                      qs_ref, m_ref, acc_ref, sc_ref, *, lambda_init):
    qi = pl.program_id(1)
    tq, tk = ATT_TQ, ATT_TK
    blocks = tq // tk
    neg_inf = jnp.float32(-jnp.inf)

    low = lax.broadcasted_iota(jnp.int32, (tq, LANES), 1) < DQK
    for h in range(N_HEADS):
        q2 = q_ref[:, h * DV:(h + 1) * DV]
        zero = jnp.zeros_like(q2)
        qs_ref[h] = jnp.concatenate([jnp.where(low, q2, zero), jnp.where(low, zero, q2)], axis=0)

    all_units = [(h, br, blk) for h in range(N_HEADS) for br in range(2) for blk in range(blocks)]

    def unit_cols(br, blk):
        return pl.ds((br * blocks + blk) * tk, tk)

    def run(items, primed, next_items):
        stream = items + next_items

        def scores(item):
            tile, (h, br, blk) = item
            s = lax.dot_general(tile["k"](h), qs_ref[h, unit_cols(br, blk), :], NT_DIMS,
                                preferred_element_type=F32)
            bias = tile["bias"](blk)
            return s if bias is None else s + bias

        pending = [sc_ref[n] if primed else scores(stream[n]) for n in range(ATT_AHEAD)]
        for n, (tile, (h, br, blk)) in enumerate(items):
            s = pending.pop(0)
            ahead = n + ATT_AHEAD
            if ahead < len(items):
                pending.append(scores(stream[ahead]))
            elif ahead < len(stream):
                sc_ref[ahead - len(items)] = scores(stream[ahead])
            cols = unit_cols(br, blk)
            v_aug = jnp.concatenate([tile["vt"](h), tile["ones"]], axis=0)
            s_max = jnp.max(s, axis=0, keepdims=True)
            if tile["first"]:
                m_new = s_max
                acc_ref[h, :, cols] = jnp.dot(v_aug, jnp.exp2(s - m_new).astype(BF16), preferred_element_type=F32)
            else:
                m_old = m_ref[h, :, cols]
                m_new = jnp.maximum(m_old, s_max)
                alpha = jnp.exp2(m_old - m_new)
                p = jnp.exp2(s - m_new).astype(BF16)
                acc_ref[h, :, cols] = alpha * acc_ref[h, :, cols] + jnp.dot(v_aug, p, preferred_element_type=F32)
            m_ref[h, :, cols] = m_new
        assert not pending and len(items) >= ATT_AHEAD

    ones_tile = _ones_rows(tk)

    def frame_tile(start, bias_of):
        return dict(k=lambda h: k_ref[pl.ds(start, tk), h * DV:(h + 1) * DV],
                    vt=lambda h: vt_ref[h * DV:(h + 1) * DV, pl.ds(start, tk)],
                    ones=ones_tile, bias=bias_of, first=False)

    def full_items(start, units):
        tile = frame_tile(start, lambda blk: None)
        return [(tile, u) for u in units]

    chunk_shift = CHUNK.bit_length() - 1
    k_chunk = lax.shift_right_logical(lax.broadcasted_iota(jnp.int32, (tk, tk), 0), chunk_shift)
    q_chunk = lax.shift_right_logical(lax.broadcasted_iota(jnp.int32, (tk, tk), 1), chunk_shift)
    diag_bias = jnp.where(k_chunk <= q_chunk, 0.0, neg_inf)
    meta_row = lax.broadcasted_iota(jnp.int32, (ATT_META_ROWS, tk), 0)
    meta_bias = jnp.where(meta_row >= ATT_META_PAD, 0.0, neg_inf)
    first_bias = [jnp.concatenate([diag_bias if blk == 0 else jnp.zeros_like(diag_bias), meta_bias], axis=0)
                  for blk in range(min(blocks, 2))]
    start0 = pl.multiple_of(qi * tq, tk)
    first_tile = dict(
        k=lambda h: jnp.concatenate([k_ref[pl.ds(start0, tk), h * DV:(h + 1) * DV],
                                     km_ref[:, h * DV:(h + 1) * DV]], axis=0),
        vt=lambda h: jnp.concatenate([vt_ref[h * DV:(h + 1) * DV, pl.ds(start0, tk)],
                                      vtm_ref[h * DV:(h + 1) * DV, :]], axis=1),
        ones=_ones_rows(tk + ATT_META_ROWS), bias=lambda blk: first_bias[min(blk, 1)], first=True)
    items = [(first_tile, u) for u in all_units]
    for d in range(1, blocks):
        tile = frame_tile(pl.multiple_of(qi * tq + d * tk, tk), lambda blk, d=d: diag_bias if blk == d else None)
        items += [(tile, u) for u in all_units if u[2] >= d]
    head_units = all_units[:ATT_AHEAD]
    run(items, primed=False, next_items=full_items(0, head_units))

    n_full = qi * blocks
    last_tile = k_ref.shape[0] // tk - 1

    def full_tile(j, carry):
        nxt = pl.multiple_of(jnp.minimum(j + 1, last_tile) * tk, tk)
        run(full_items(pl.multiple_of(j * tk, tk), all_units), primed=True,
            next_items=full_items(nxt, head_units))
        return carry

    lax.fori_loop(0, n_full, full_tile, 0)

    lv = lam_ref[...]
    lam = (jnp.exp(jnp.sum(lv[0:1] * lv[1:2], axis=-1, keepdims=True))
           - jnp.exp(jnp.sum(lv[2:3] * lv[3:4], axis=-1, keepdims=True)) + lambda_init)
    out_gain = gain_ref[...] * (1.0 - lambda_init)
    for h in range(N_HEADS):
        cols = slice(h * DV, (h + 1) * DV)
        acc = acc_ref[h]
        inv_l = 1.0 / acc[DV:DV + 1, :]
        o_t = acc[0:DV, :tq] * inv_l[:, :tq] - lam * (acc[0:DV, tq:] * inv_l[:, tq:])
        o_t = o_t * lax.rsqrt(jnp.mean(o_t * o_t, axis=0, keepdims=True) + SUBLN_EPS)
        o_ref[:, cols] = (o_t.T * out_gain[:, cols]).astype(o_ref.dtype)


def _diff_attention(q, k, vt, k_meta, vt_meta, lam_vecs, gain, batch, lambda_init):
    t = q.shape[0]
    seq = t // batch
    nq = seq // ATT_TQ
    assert nq * ATT_TQ == seq and ATT_TQ % ATT_TK == 0 and ATT_TK % CHUNK == 0
    width = N_HEADS * DV
    return pl.pallas_call(
        functools.partial(_diff_attn_kernel, lambda_init=lambda_init),
        grid=(batch, nq),
        in_specs=[_resident(lam_vecs.shape),
                  pl.BlockSpec((ATT_TQ, width), lambda b, i: (b * nq + i, 0)),
                  pl.BlockSpec((seq, width), lambda b, i: (b, 0)),
                  pl.BlockSpec((width, seq), lambda b, i: (0, b)),
                  _resident(k_meta.shape),
                  _resident(vt_meta.shape),
                  _resident((1, width))],
        out_specs=pl.BlockSpec((ATT_TQ, width), lambda b, i: (b * nq + i, 0)),
        out_shape=jax.ShapeDtypeStruct((t, width), BF16),
        scratch_shapes=[pltpu.VMEM((N_HEADS, 2 * ATT_TQ, DV), BF16),
                        pltpu.VMEM((N_HEADS, 1, 2 * ATT_TQ), F32),
                        pltpu.VMEM((N_HEADS, VAUG_ROWS, 2 * ATT_TQ), F32),
                        pltpu.VMEM((ATT_AHEAD, ATT_TK, ATT_TK), F32)],
        compiler_params=_compiler_params(("parallel", "arbitrary")),
        name="diff_attention",
    )(lam_vecs, q, k, vt, k_meta, vt_meta, gain)


def _post_mlp_kernel(h_ref, y_ref, wo_ref, g_ref, wu_ref, wd_ref, fg_ref, *refs, final_norm, ff_block,
                     cast_transposed):
    n_cast = len(cast_transposed)
    out_ref = refs[n_cast]
    _cast_slabs(refs[:n_cast], refs[n_cast + 1:], cast_transposed)
    h1 = h_ref[...] + jnp.dot(y_ref[...], wo_ref[...], preferred_element_type=F32)
    ms = jnp.mean(h1 * h1, axis=-1, keepdims=True)
    hn = (h1 * lax.rsqrt(ms + EPS) * g_ref[...]).astype(BF16)
    acc = h1
    for c in range(D_FF // ff_block):
        blk = slice(c * ff_block, (c + 1) * ff_block)
        u = jnp.dot(hn, wu_ref[:, blk], preferred_element_type=F32)
        a = jnp.square(jnp.maximum(u, 0.0)).astype(BF16)
        acc = acc + jnp.dot(a, wd_ref[blk, :], preferred_element_type=F32)
    if final_norm:
        ms = jnp.mean(acc * acc, axis=-1, keepdims=True)
        acc = acc * lax.rsqrt(ms + EPS) * fg_ref[...]
    out_ref[...] = acc


def _post_mlp(h, y, w_out, gain, w_up, w_down, final_gain, final_norm, tm, name, to_cast=()):
    t = h.shape[0]
    assert t % tm == 0
    cast_in, cast_out, cast_shapes = _cast_specs(to_cast, t // tm)
    return pl.pallas_call(
        functools.partial(_post_mlp_kernel, final_norm=final_norm, ff_block=1024,
                          cast_transposed=tuple(job["transpose"] for job in to_cast)),
        grid=(t // tm,),
        in_specs=[pl.BlockSpec((tm, D_MODEL), lambda i: (i, 0)),
                  pl.BlockSpec((tm, D_MODEL), lambda i: (i, 0)),
                  _resident((D_MODEL, D_MODEL)),
                  _resident((1, D_MODEL)),
                  _resident((D_MODEL, D_FF)),
                  _resident((D_FF, D_MODEL)),
                  _resident((1, D_MODEL))] + cast_in,
        out_specs=[pl.BlockSpec((tm, D_MODEL), lambda i: (i, 0))] + cast_out,
        out_shape=[jax.ShapeDtypeStruct((t, D_MODEL), F32)] + cast_shapes,
        compiler_params=_compiler_params(("arbitrary",)),
        name=name,
    )(h, y, w_out, gain, w_up, w_down, final_gain, *[job["array"] for job in to_cast])


def kernel(x, meta_tokens, norm_gains, mlstm_w_in, mlstm_b_gate, mlstm_head_gain, mlstm_w_out,
           diff_w_in, diff_lambda, diff_head_gain, diff_w_out, mlp_w_up, mlp_w_down, final_gain):
    batch, seq, _ = x.shape
    t = batch * seq
    h = x.reshape(t, D_MODEL)
    h_meta = jnp.concatenate([jnp.zeros((META_ROWS - N_META, D_MODEL), x.dtype), meta_tokens.astype(x.dtype)],
                             axis=0)

    m_qk = N_HEADS * DQK
    m_v = N_HEADS * DV
    w0 = mlstm_w_in[0]
    offs = (0, m_qk, 2 * m_qk, 2 * m_qk + m_v, 2 * m_qk + 2 * m_v)
    w0_parts = [w0[:, offs[3]:offs[4]], w0[:, offs[2]:offs[3]].T, w0[:, offs[0]:offs[1]], w0[:, offs[1]:offs[2]],
                w0[:, offs[4]:].T]
    w0_parts = [w.astype(BF16) for w in w0_parts]
    in0 = dict(gain=norm_gains[0, 0][None], weights=w0_parts, scales=(1.0, 1.0, DQK ** -0.5, 1.0, 1.0),
               transposed=(False, True, False, False, True), sigmoid=(True, False, False, False, False),
               dtypes=(BF16, BF16, BF16, BF16, F32))
    bias = mlstm_b_gate[0][:, None]
    head_gain0 = mlstm_head_gain[0][None]

    o_gate, vt, q, k, gates_t = _norm_proj(h_meta, tm=META_ROWS, name="mlstm_in_proj_meta", **in0)
    ct_zero = jnp.zeros((N_HEADS, VAUG_ROWS, LANES), F32)
    m_zero = jnp.zeros((N_HEADS, LANES), F32)
    y_meta, ct_meta, m_meta = _mlstm_core(q, k, vt, o_gate, gates_t, bias, head_gain0, ct_zero, m_zero, batch=1,
                                          chunks_per_step=1, group=1, name="mlstm_core_meta")

    o_gate, vt, q, k, gates_t, w_out0, w_up0, w_down0 = _norm_proj(
        h, tm=PROJ_TM, name="mlstm_in_proj",
        to_cast=(_cast_job(mlstm_w_out, 0), _cast_job(mlp_w_up, 0), _cast_job(mlp_w_down, 0)), **in0)
    mlp0 = dict(w_out=w_out0, gain=norm_gains[0, 1][None], w_up=w_up0, w_down=w_down0,
                final_gain=final_gain[None], final_norm=False)
    h_meta, = _post_mlp(h_meta, y_meta, tm=META_ROWS, name="mlstm_out_mlp_meta", **mlp0)
    y, _, _ = _mlstm_core(q, k, vt, o_gate, gates_t, bias, head_gain0, ct_meta[0, 0], m_meta[0, 0], batch=batch,
                          chunks_per_step=MLSTM_CHUNKS_PER_STEP, group=MLSTM_GROUP, name="mlstm_core")
    layer1_jobs = (_cast_job(diff_w_in, 0, 0, 3), _cast_job(diff_w_in, 0, 1, 3),
                   _cast_job(diff_w_in, 0, 2, 3, transpose=True),
                   _cast_job(diff_w_out, 0), _cast_job(mlp_w_up, 1), _cast_job(mlp_w_down, 1))
    h, w_q1, w_k1, w_vt1, w_out1, w_up1, w_down1 = _post_mlp(h, y, tm=MLP_TM, name="mlstm_out_mlp",
                                                             to_cast=layer1_jobs, **mlp0)

    lambda_init = 0.8 - 0.6 * math.exp(-0.3 * 1)
    in1 = dict(gain=norm_gains[1, 0][None], weights=[w_q1, w_k1, w_vt1], scales=(DQK ** -0.5 * LOG2E, 1.0, 1.0),
               transposed=(False, False, True), sigmoid=(False,) * 3, dtypes=(BF16, BF16, BF16))
    _, k_meta, vt_meta = _norm_proj(h_meta, tm=META_ROWS, name="diff_in_proj_meta", **in1)
    k_meta = k_meta[META_ROWS - ATT_META_ROWS:]
    vt_meta = vt_meta[:, META_ROWS - ATT_META_ROWS:]
    q, k, vt = _norm_proj(h, tm=PROJ_TM, name="diff_in_proj", **in1)
    y = _diff_attention(q, k, vt, k_meta, vt_meta, diff_lambda[0], diff_head_gain[0][None], batch, lambda_init)
    out, = _post_mlp(h, y, w_out1, norm_gains[1, 1][None], w_up1, w_down1, final_gain[None], final_norm=True,
                     tm=MLP_TM, name="diff_out_mlp")
    return out.reshape(batch, seq, D_MODEL)
```

```python
import functools
import math

import jax
import jax.numpy as jnp
from jax import lax
from jax.experimental import pallas as pl
from jax.experimental.pallas import tpu as pltpu

D_MODEL = 1024
D_FF = 4 * D_MODEL
N_META = 16
CHUNK = 64
N_HEADS = 8
DQK = 64
DV = 128
EPS = 1e-6
SUBLN_EPS = 1e-5
LOG2E = math.log2(math.e)

LANES = 128
BF16_ROWS = 16
AUG_ROWS = BF16_ROWS
VAUG_ROWS = DV + AUG_ROWS
GATE_ROWS = 2 * N_HEADS

MCHUNK = 256
MLSTM_CHUNKS_PER_STEP = 8
MLSTM_GROUP = 1
META_ROWS = MCHUNK
ATT_META_ROWS = CHUNK
ATT_META_PAD = ATT_META_ROWS - N_META
ATT_TQ = 1024
ATT_TK = 256
ATT_AHEAD = 5

PROJ_TM = 1024
PROJ_SUB = 512
MLP_TM = 512
MLP_FF_BLOCK = 1024

VMEM_LIMIT = 56 * 1024 * 1024

F32 = jnp.float32
BF16 = jnp.bfloat16
NT_DIMS = (((1,), (1,)), ((), ()))


def _compiler_params(semantics):
    return pltpu.CompilerParams(dimension_semantics=semantics, vmem_limit_bytes=VMEM_LIMIT)


def _resident(shape):
    return pl.BlockSpec(shape, lambda *_: (0,) * len(shape), pipeline_mode=pl.Buffered(1))


def _ones_rows(width):
    return (lax.broadcasted_iota(jnp.int32, (AUG_ROWS, width), 0) == 0).astype(BF16)


def _cast_job(stacked, layer, col_block=0, n_col_blocks=1, transpose=False):
    return dict(array=stacked, layer=layer, col_block=col_block, n_col_blocks=n_col_blocks, transpose=transpose)


def _cast_specs(jobs, steps):
    in_specs, out_specs, out_shapes = [], [], []
    for job in jobs:
        _, n_rows, n_cols = job["array"].shape
        cols = n_cols // job["n_col_blocks"]
        layer, col_block = job["layer"], job["col_block"]
        if job["transpose"]:
            per_slab = steps * LANES // n_rows
            assert per_slab * n_rows == steps * LANES
            in_specs.append(pl.BlockSpec((None, LANES, cols),
                                         lambda i, l=layer, c=col_block, p=per_slab: (l, i // p, c)))
            out_specs.append(pl.BlockSpec((cols, LANES), lambda i, p=per_slab: (0, i // p)))
            out_shapes.append(jax.ShapeDtypeStruct((cols, n_rows), BF16))
        else:
            rows = n_rows // steps
            assert rows * steps == n_rows and rows % BF16_ROWS == 0
            in_specs.append(pl.BlockSpec((None, rows, cols), lambda i, l=layer, c=col_block: (l, i, c)))
            out_specs.append(pl.BlockSpec((rows, cols), lambda i: (i, 0)))
            out_shapes.append(jax.ShapeDtypeStruct((n_rows, cols), BF16))
    return in_specs, out_specs, out_shapes


def _cast_slabs(src_refs, dst_refs, transposed):
    for src, dst, tr in zip(src_refs, dst_refs, transposed):
        x = src[...]
        dst[...] = (x.T if tr else x).astype(dst.dtype)


def _norm_proj_kernel(x_ref, g_ref, *refs, scales, transposed, sigmoid, cast_transposed):
    n_out, n_cast = len(scales), len(cast_transposed)
    w_refs, out_refs = refs[:n_out], refs[n_out + n_cast:2 * n_out + n_cast]
    _cast_slabs(refs[n_out:n_out + n_cast], refs[2 * n_out + n_cast:], cast_transposed)
    tm = x_ref.shape[0]
    sub = min(PROJ_SUB, tm)
    for r0 in range(0, tm, sub):
        rows = slice(r0, r0 + sub)
        x = x_ref[rows, :]
        ms = jnp.mean(x * x, axis=-1, keepdims=True)
        xn = (x * lax.rsqrt(ms + EPS) * g_ref[...]).astype(BF16)
        for w_ref, o_ref, sc, tr, sg in zip(w_refs, out_refs, scales, transposed, sigmoid):
            if tr:
                r = lax.dot_general(w_ref[...], xn, NT_DIMS, preferred_element_type=F32)
            else:
                r = jnp.dot(xn, w_ref[...], preferred_element_type=F32)
            if sc != 1.0:
                r = r * sc
            if sg:
                r = 0.5 * jnp.tanh(0.5 * r) + 0.5
            if tr:
                o_ref[:, rows] = r.astype(o_ref.dtype)
            else:
                o_ref[rows, :] = r.astype(o_ref.dtype)


def _norm_proj(h, gain, weights, scales, transposed, sigmoid, dtypes, tm, name, to_cast=()):
    t = h.shape[0]
    assert t % tm == 0 and tm % min(PROJ_SUB, tm) == 0
    cast_in, cast_out, cast_shapes = _cast_specs(to_cast, t // tm)
    w_specs, out_specs, out_shapes = [], [], []
    for w, tr, dt in zip(weights, transposed, dtypes):
        w_specs.append(_resident(w.shape))
        if tr:
            n = w.shape[0]
            out_specs.append(pl.BlockSpec((n, tm), lambda i: (0, i)))
            out_shapes.append(jax.ShapeDtypeStruct((n, t), dt))
        else:
            n = w.shape[1]
            out_specs.append(pl.BlockSpec((tm, n), lambda i: (i, 0)))
            out_shapes.append(jax.ShapeDtypeStruct((t, n), dt))
    return pl.pallas_call(
        functools.partial(_norm_proj_kernel, scales=scales, transposed=transposed, sigmoid=sigmoid,
                          cast_transposed=tuple(job["transpose"] for job in to_cast)),
        grid=(t // tm,),
        in_specs=[pl.BlockSpec((tm, D_MODEL), lambda i: (i, 0)), _resident((1, D_MODEL))] + w_specs + cast_in,
        out_specs=out_specs + cast_out,
        out_shape=out_shapes + cast_shapes,
        compiler_params=_compiler_params(("arbitrary",)),
        name=name,
    )(h, gain, *weights, *[job["array"] for job in to_cast])


def _mlstm_kernel(*refs, chunks_per_step, group):
    q_ref, k_ref, o_ref = refs[0:3]
    vt_refs = refs[3:3 + group]
    gt_refs = refs[3 + group:3 + 2 * group]
    (bias_ref, gain_ref, ct0_ref, m0_ref, y_ref, ct_out_ref, m_out_ref,
     ct_ref, m_ref, b_ref, r_ref, cm_ref, rt_ref) = refs[3 + 2 * group:]
    t = MCHUNK

    @pl.when(pl.program_id(1) == 0)
    def _():
        for g in range(group):
            ct_ref[g] = ct0_ref[...]
            m_ref[g] = m0_ref[...]

    neg_inf = jnp.float32(-jnp.inf)
    causal = (lax.broadcasted_iota(jnp.int32, (t, t), 0)
              <= lax.broadcasted_iota(jnp.int32, (t, t), 1))
    upper = causal.astype(BF16)
    lane_t = lax.broadcasted_iota(jnp.int32, (N_HEADS, t), 1)
    bias = bias_ref[...]

    for g in range(group):
        for c in range(chunks_per_step):
            gates = gt_refs[g][:, c * t:(c + 1) * t] + bias
            ig = gates[0:N_HEADS]
            lf = jax.nn.log_sigmoid(gates[N_HEADS:])
            hi = lf.astype(BF16).astype(F32)
            mid = (lf - hi).astype(BF16).astype(F32)
            lo = (lf - hi - mid).astype(BF16).astype(F32)
            parts = jnp.dot(jnp.concatenate([hi, mid, lo], axis=0).astype(BF16), upper,
                            preferred_element_type=F32)
            b = parts[0:N_HEADS] + parts[N_HEADS:2 * N_HEADS] + parts[2 * N_HEADS:]
            r = ig - b
            cm = r
            shift = 1
            while shift < t:
                cm = jnp.maximum(cm, jnp.where(lane_t >= shift, pltpu.roll(cm, shift, 1), neg_inf))
                shift *= 2
            b_ref[g, c] = b
            r_ref[g, c] = r
            cm_ref[g, c] = cm
            rt_ref[g, c] = (r * LOG2E).T

    low = lax.broadcasted_iota(jnp.int32, (t, LANES), 1) < DQK
    low_state = lax.broadcasted_iota(jnp.int32, (VAUG_ROWS, LANES), 1) < DQK
    ones_rows = _ones_rows(t)
    half = t // 2
    tri = causal[:half, :half]

    def chunk_body(c, carry):
        rows = pl.ds(pl.multiple_of(c * t, t), t)
        row_terms = []
        for g in range(group):
            b, r, cm = b_ref[g, c], r_ref[g, c], cm_ref[g, c]
            m_prev = m_ref[g, :, 0:1]
            u = jnp.maximum(m_prev, cm)
            b_end = b[:, t - 1:t]
            a = b_end + r
            m_new = jnp.maximum(b_end + m_prev, jnp.max(a, axis=1, keepdims=True))
            row_terms.append(dict(
                inter=jnp.exp(m_prev - u), e_row=jnp.exp(-(b + u)), u2=u * LOG2E,
                decay=jnp.exp(b_end + m_prev - m_new), w=jnp.exp(a - m_new).astype(BF16), rt=rt_ref[g, c]))
            m_ref[g] = jnp.broadcast_to(m_new, m_ref.shape[1:])

        def pair_scores(g, p):
            q2 = q_ref[g, rows, p * LANES:(p + 1) * LANES]
            k2 = k_ref[g, rows, p * LANES:(p + 1) * LANES]
            zero = jnp.zeros_like(q2)
            qs = jnp.concatenate([jnp.where(low, q2, zero), jnp.where(low, zero, q2)], axis=0)
            return q2, k2, lax.dot_general(k2, qs, NT_DIMS, preferred_element_type=F32)

        seq = [(g, p) for p in range(N_HEADS // 2) for g in range(group)]
        nxt = pair_scores(*seq[0])
        for n, (g, p) in enumerate(seq):
            q2, k2, st = nxt
            terms = row_terms[g]
            heads = (2 * p, 2 * p + 1)
            cts = [ct_ref[g, h] for h in heads]
            vaugs = [jnp.concatenate([vt_refs[g][h * DV:(h + 1) * DV, rows], ones_rows], axis=0) for h in heads]
            inter_nums = [lax.dot_general(ct.astype(BF16), q2, NT_DIMS, preferred_element_type=F32)
                          for ct in cts]
            for i, h in enumerate(heads):
                wv = vaugs[i] * terms["w"][h:h + 1, :]
                upd = jnp.dot(wv, k2, preferred_element_type=F32)
                own = low_state if i == 0 else jnp.logical_not(low_state)
                ct_ref[g, h] = terms["decay"][h:h + 1, :] * cts[i] + jnp.where(own, upd, 0.0)
            if n + 1 < len(seq):
                nxt = pair_scores(*seq[n + 1])
            for i, h in enumerate(heads):
                cols = slice(h * DV, (h + 1) * DV)
                r_col, u_row = terms["rt"][:, h:h + 1], terms["u2"][h:h + 1, :]
                st_h = st[:, i * t:(i + 1) * t]
                d00 = jnp.exp2(jnp.where(tri, r_col[:half] - u_row[:, :half], neg_inf))
                d01 = jnp.exp2(r_col[:half] - u_row[:, half:])
                d11 = jnp.exp2(jnp.where(tri, r_col[half:] - u_row[:, half:], neg_inf))
                s_h = jnp.concatenate(
                    [jnp.concatenate([st_h[:half, :half] * d00, st_h[:half, half:] * d01], axis=1),
                     jnp.concatenate([jnp.zeros((half, half), F32), st_h[half:, half:] * d11], axis=1)],
                    axis=0).astype(BF16)
                num = (terms["inter"][h:h + 1, :] * inter_nums[i]
                       + jnp.dot(vaugs[i], s_h, preferred_element_type=F32))
                inv = 1.0 / jnp.maximum(jnp.abs(num[DV:DV + 1, :]), terms["e_row"][h:h + 1, :])
                hh = num[0:DV, :] * inv
                hh = hh * lax.rsqrt(jnp.mean(hh * hh, axis=0, keepdims=True) + EPS)
                y_ref[g, rows, cols] = (hh.T * gain_ref[:, cols]).astype(BF16) * o_ref[g, rows, cols]
        return carry

    lax.fori_loop(0, chunks_per_step, chunk_body, 0)
    ct_out_ref[...] = ct_ref[...]
    m_out_ref[...] = m_ref[...]


def _mlstm_core(q, k, vt, o, gates_t, bias, gain, ct0, m0, batch, chunks_per_step, group, name):
    tokens = q.shape[0]
    rows = chunks_per_step * MCHUNK
    per_group = batch // group
    steps = tokens // batch // rows
    assert per_group * group == batch and batch * steps * rows == tokens
    ct_shape = (N_HEADS, VAUG_ROWS, LANES)
    m_shape = (N_HEADS, LANES)
    gate_shape = (group, chunks_per_step, N_HEADS, MCHUNK)

    def tok(width):
        return pl.BlockSpec((group, rows, width), lambda b, j: (0, b * steps + j, 0))

    def tok_t(height, g):
        return pl.BlockSpec((height, rows), lambda b, j, g=g: (0, (g * per_group + b) * steps + j))

    def grouped(x):
        return x.reshape(group, tokens // group, x.shape[1])

    y, ct, m = pl.pallas_call(
        functools.partial(_mlstm_kernel, chunks_per_step=chunks_per_step, group=group),
        grid=(per_group, steps),
        in_specs=([tok(N_HEADS * DQK), tok(N_HEADS * DQK), tok(N_HEADS * DV)]
                  + [tok_t(N_HEADS * DV, g) for g in range(group)]
                  + [tok_t(GATE_ROWS, g) for g in range(group)]
                  + [_resident((2 * N_HEADS, 1)), _resident((1, N_HEADS * DV)),
                     _resident(ct_shape), _resident(m_shape)]),
        out_specs=[tok(N_HEADS * DV),
                   pl.BlockSpec((group, None) + ct_shape, lambda b, j: (0, b, 0, 0, 0)),
                   pl.BlockSpec((group, None) + m_shape, lambda b, j: (0, b, 0, 0))],
        out_shape=[jax.ShapeDtypeStruct((group, tokens // group, N_HEADS * DV), BF16),
                   jax.ShapeDtypeStruct((group, per_group) + ct_shape, F32),
                   jax.ShapeDtypeStruct((group, per_group) + m_shape, F32)],
        scratch_shapes=[pltpu.VMEM((group,) + ct_shape, F32), pltpu.VMEM((group,) + m_shape, F32),
                        pltpu.VMEM(gate_shape, F32), pltpu.VMEM(gate_shape, F32), pltpu.VMEM(gate_shape, F32),
                        pltpu.VMEM((group, chunks_per_step, MCHUNK, N_HEADS), F32)],
        compiler_params=_compiler_params(("parallel", "arbitrary")),
        name=name,
    )(grouped(q), grouped(k), grouped(o), *([vt] * group), *([gates_t] * group), bias, gain, ct0, m0)
    return y.reshape(tokens, N_HEADS * DV), ct, m


def _diff_attn_kernel(lam_ref, q_ref, k_ref, vt_ref, km_ref, vtm_ref, gain_ref, o_ref,
                      qs_ref, m_ref, acc_ref, sc_ref, *, lambda_init):
    qi = pl.program_id(1)
    tq, tk = ATT_TQ, ATT_TK
    blocks = tq // tk
    neg_inf = jnp.float32(-jnp.inf)

    low = lax.broadcasted_iota(jnp.int32, (tq, LANES), 1) < DQK
    for h in range(N_HEADS):
        q2 = q_ref[:, h * DV:(h + 1) * DV]
        zero = jnp.zeros_like(q2)
        qs_ref[h] = jnp.concatenate([jnp.where(low, q2, zero), jnp.where(low, zero, q2)], axis=0)

    all_units = [(h, br, blk) for h in range(N_HEADS) for br in range(2) for blk in range(blocks)]

    def unit_cols(br, blk):
        return pl.ds((br * blocks + blk) * tk, tk)

    def run(items, primed, next_items):
        stream = items + next_items

        def scores(item):
            tile, (h, br, blk) = item
            s = lax.dot_general(tile["k"](h), qs_ref[h, unit_cols(br, blk), :], NT_DIMS,
                                preferred_element_type=F32)
            bias = tile["bias"](blk)
            return s if bias is None else s + bias

        pending = [sc_ref[n] if primed else scores(stream[n]) for n in range(ATT_AHEAD)]
        for n, (tile, (h, br, blk)) in enumerate(items):
            s = pending.pop(0)
            ahead = n + ATT_AHEAD
            if ahead < len(items):
                pending.append(scores(stream[ahead]))
            elif ahead < len(stream):
                sc_ref[ahead - len(items)] = scores(stream[ahead])
            cols = unit_cols(br, blk)
            v_aug = jnp.concatenate([tile["vt"](h), tile["ones"]], axis=0)
            s_max = jnp.max(s, axis=0, keepdims=True)
            if tile["first"]:
                m_new = s_max
                acc_ref[h, :, cols] = jnp.dot(v_aug, jnp.exp2(s - m_new).astype(BF16), preferred_element_type=F32)
            else:
                m_old = m_ref[h, :, cols]
                m_new = jnp.maximum(m_old, s_max)
                alpha = jnp.exp2(m_old - m_new)
                p = jnp.exp2(s - m_new).astype(BF16)
                acc_ref[h, :, cols] = alpha * acc_ref[h, :, cols] + jnp.dot(v_aug, p, preferred_element_type=F32)
            m_ref[h, :, cols] = m_new
        assert not pending and len(items) >= ATT_AHEAD

    ones_tile = _ones_rows(tk)

    def frame_tile(start, bias_of):
        return dict(k=lambda h: k_ref[pl.ds(start, tk), h * DV:(h + 1) * DV],
                    vt=lambda h: vt_ref[h * DV:(h + 1) * DV, pl.ds(start, tk)],
                    ones=ones_tile, bias=bias_of, first=False)

    def full_items(start, units):
        tile = frame_tile(start, lambda blk: None)
        return [(tile, u) for u in units]

    chunk_shift = CHUNK.bit_length() - 1
    k_chunk = lax.shift_right_logical(lax.broadcasted_iota(jnp.int32, (tk, tk), 0), chunk_shift)
    q_chunk = lax.shift_right_logical(lax.broadcasted_iota(jnp.int32, (tk, tk), 1), chunk_shift)
    diag_bias = jnp.where(k_chunk <= q_chunk, 0.0, neg_inf)
    meta_row = lax.broadcasted_iota(jnp.int32, (ATT_META_ROWS, tk), 0)
    meta_bias = jnp.where(meta_row >= ATT_META_PAD, 0.0, neg_inf)
    first_bias = [jnp.concatenate([diag_bias if blk == 0 else jnp.zeros_like(diag_bias), meta_bias], axis=0)
                  for blk in range(min(blocks, 2))]
    start0 = pl.multiple_of(qi * tq, tk)
    first_tile = dict(
        k=lambda h: jnp.concatenate([k_ref[pl.ds(start0, tk), h * DV:(h + 1) * DV],
                                     km_ref[:, h * DV:(h + 1) * DV]], axis=0),
        vt=lambda h: jnp.concatenate([vt_ref[h * DV:(h + 1) * DV, pl.ds(start0, tk)],
                                      vtm_ref[h * DV:(h + 1) * DV, :]], axis=1),
        ones=_ones_rows(tk + ATT_META_ROWS), bias=lambda blk: first_bias[min(blk, 1)], first=True)
    items = [(first_tile, u) for u in all_units]
    for d in range(1, blocks):
        tile = frame_tile(pl.multiple_of(qi * tq + d * tk, tk), lambda blk, d=d: diag_bias if blk == d else None)
        items += [(tile, u) for u in all_units if u[2] >= d]
    head_units = all_units[:ATT_AHEAD]
    run(items, primed=False, next_items=full_items(0, head_units))

    n_full = qi * blocks
    last_tile = k_ref.shape[0] // tk - 1

    def full_tile(j, carry):
        nxt = pl.multiple_of(jnp.minimum(j + 1, last_tile) * tk, tk)
        run(full_items(pl.multiple_of(j * tk, tk), all_units), primed=True,
            next_items=full_items(nxt, head_units))
        return carry

    lax.fori_loop(0, n_full, full_tile, 0)

    lv = lam_ref[...]
    lam = (jnp.exp(jnp.sum(lv[0:1] * lv[1:2], axis=-1, keepdims=True))
           - jnp.exp(jnp.sum(lv[2:3] * lv[3:4], axis=-1, keepdims=True)) + lambda_init)
    out_gain = gain_ref[...] * (1.0 - lambda_init)
    for h in range(N_HEADS):
        cols = slice(h * DV, (h + 1) * DV)
        acc = acc_ref[h]
        inv_l = 1.0 / acc[DV:DV + 1, :]
        o_t = acc[0:DV, :tq] * inv_l[:, :tq] - lam * (acc[0:DV, tq:] * inv_l[:, tq:])
        o_t = o_t * lax.rsqrt(jnp.mean(o_t * o_t, axis=0, keepdims=True) + SUBLN_EPS)
        o_ref[:, cols] = (o_t.T * out_gain[:, cols]).astype(o_ref.dtype)


def _diff_attention(q, k, vt, k_meta, vt_meta, lam_vecs, gain, batch, lambda_init):
    t = q.shape[0]
    seq = t // batch
    nq = seq // ATT_TQ
    assert nq * ATT_TQ == seq and ATT_TQ % ATT_TK == 0 and ATT_TK % CHUNK == 0
    width = N_HEADS * DV
    return pl.pallas_call(
        functools.partial(_diff_attn_kernel, lambda_init=lambda_init),
        grid=(batch, nq),
        in_specs=[_resident(lam_vecs.shape),
                  pl.BlockSpec((ATT_TQ, width), lambda b, i: (b * nq + i, 0)),
                  pl.BlockSpec((seq, width), lambda b, i: (b, 0)),
                  pl.BlockSpec((width, seq), lambda b, i: (0, b)),
                  _resident(k_meta.shape),
                  _resident(vt_meta.shape),
                  _resident((1, width))],
        out_specs=pl.BlockSpec((ATT_TQ, width), lambda b, i: (b * nq + i, 0)),
        out_shape=jax.ShapeDtypeStruct((t, width), BF16),
        scratch_shapes=[pltpu.VMEM((N_HEADS, 2 * ATT_TQ, DV), BF16),
                        pltpu.VMEM((N_HEADS, 1, 2 * ATT_TQ), F32),
                        pltpu.VMEM((N_HEADS, VAUG_ROWS, 2 * ATT_TQ), F32),
                        pltpu.VMEM((ATT_AHEAD, ATT_TK, ATT_TK), F32)],
        compiler_params=_compiler_params(("parallel", "arbitrary")),
        name="diff_attention",
    )(lam_vecs, q, k, vt, k_meta, vt_meta, gain)


def _post_mlp_kernel(h_ref, y_ref, wo_ref, g_ref, wu_ref, wd_ref, fg_ref, *refs, final_norm, ff_block,
                     cast_transposed):
    n_cast = len(cast_transposed)
    out_ref = refs[n_cast]
    _cast_slabs(refs[:n_cast], refs[n_cast + 1:], cast_transposed)
    h1 = h_ref[...] + jnp.dot(y_ref[...], wo_ref[...], preferred_element_type=F32)
    ms = jnp.mean(h1 * h1, axis=-1, keepdims=True)
    hn = (h1 * lax.rsqrt(ms + EPS) * g_ref[...]).astype(BF16)
    acc = h1
    for c in range(D_FF // ff_block):
        blk = slice(c * ff_block, (c + 1) * ff_block)
        u = jnp.dot(hn, wu_ref[:, blk], preferred_element_type=F32)
        a = jnp.square(jnp.maximum(u, 0.0)).astype(BF16)
        acc = acc + jnp.dot(a, wd_ref[blk, :], preferred_element_type=F32)
    if final_norm:
        ms = jnp.mean(acc * acc, axis=-1, keepdims=True)
        acc = acc * lax.rsqrt(ms + EPS) * fg_ref[...]
    out_ref[...] = acc


def _post_mlp(h, y, w_out, gain, w_up, w_down, final_gain, final_norm, tm, name, to_cast=()):
    t = h.shape[0]
    assert t % tm == 0
    cast_in, cast_out, cast_shapes = _cast_specs(to_cast, t // tm)
    return pl.pallas_call(
        functools.partial(_post_mlp_kernel, final_norm=final_norm, ff_block=MLP_FF_BLOCK,
                          cast_transposed=tuple(job["transpose"] for job in to_cast)),
        grid=(t // tm,),
        in_specs=[pl.BlockSpec((tm, D_MODEL), lambda i: (i, 0)),
                  pl.BlockSpec((tm, D_MODEL), lambda i: (i, 0)),
                  _resident((D_MODEL, D_MODEL)),
                  _resident((1, D_MODEL)),
                  _resident((D_MODEL, D_FF)),
                  _resident((D_FF, D_MODEL)),
                  _resident((1, D_MODEL))] + cast_in,
        out_specs=[pl.BlockSpec((tm, D_MODEL), lambda i: (i, 0))] + cast_out,
        out_shape=[jax.ShapeDtypeStruct((t, D_MODEL), F32)] + cast_shapes,
        compiler_params=_compiler_params(("arbitrary",)),
        name=name,
    )(h, y, w_out, gain, w_up, w_down, final_gain, *[job["array"] for job in to_cast])


def kernel(x, meta_tokens, norm_gains, mlstm_w_in, mlstm_b_gate, mlstm_head_gain, mlstm_w_out,
           diff_w_in, diff_lambda, diff_head_gain, diff_w_out, mlp_w_up, mlp_w_down, final_gain):
    batch, seq, _ = x.shape
    t = batch * seq
    h = x.reshape(t, D_MODEL)
    h_meta = jnp.concatenate([jnp.zeros((META_ROWS - N_META, D_MODEL), x.dtype), meta_tokens.astype(x.dtype)],
                             axis=0)

    m_qk = N_HEADS * DQK
    m_v = N_HEADS * DV
    w0 = mlstm_w_in[0]
    offs = (0, m_qk, 2 * m_qk, 2 * m_qk + m_v, 2 * m_qk + 2 * m_v)
    w0_parts = [w0[:, offs[3]:offs[4]], w0[:, offs[2]:offs[3]].T, w0[:, offs[0]:offs[1]], w0[:, offs[1]:offs[2]],
                w0[:, offs[4]:].T]
    w0_parts = [w.astype(BF16) for w in w0_parts]
    in0 = dict(gain=norm_gains[0, 0][None], weights=w0_parts, scales=(1.0, 1.0, DQK ** -0.5, 1.0, 1.0),
               transposed=(False, True, False, False, True), sigmoid=(True, False, False, False, False),
               dtypes=(BF16, BF16, BF16, BF16, F32))
    bias = mlstm_b_gate[0][:, None]
    head_gain0 = mlstm_head_gain[0][None]

    o_gate, vt, q, k, gates_t = _norm_proj(h_meta, tm=META_ROWS, name="mlstm_in_proj_meta", **in0)
    ct_zero = jnp.zeros((N_HEADS, VAUG_ROWS, LANES), F32)
    m_zero = jnp.zeros((N_HEADS, LANES), F32)
    y_meta, ct_meta, m_meta = _mlstm_core(q, k, vt, o_gate, gates_t, bias, head_gain0, ct_zero, m_zero, batch=1,
                                          chunks_per_step=1, group=1, name="mlstm_core_meta")

    o_gate, vt, q, k, gates_t, w_out0, w_up0, w_down0 = _norm_proj(
        h, tm=PROJ_TM, name="mlstm_in_proj",
        to_cast=(_cast_job(mlstm_w_out, 0), _cast_job(mlp_w_up, 0), _cast_job(mlp_w_down, 0)), **in0)
    mlp0 = dict(w_out=w_out0, gain=norm_gains[0, 1][None], w_up=w_up0, w_down=w_down0,
                final_gain=final_gain[None], final_norm=False)
    h_meta, = _post_mlp(h_meta, y_meta, tm=META_ROWS, name="mlstm_out_mlp_meta", **mlp0)
    y, _, _ = _mlstm_core(q, k, vt, o_gate, gates_t, bias, head_gain0, ct_meta[0, 0], m_meta[0, 0], batch=batch,
                          chunks_per_step=MLSTM_CHUNKS_PER_STEP, group=MLSTM_GROUP, name="mlstm_core")
    layer1_jobs = (_cast_job(diff_w_in, 0, 0, 3), _cast_job(diff_w_in, 0, 1, 3),
                   _cast_job(diff_w_in, 0, 2, 3, transpose=True),
                   _cast_job(diff_w_out, 0), _cast_job(mlp_w_up, 1), _cast_job(mlp_w_down, 1))
    h, w_q1, w_k1, w_vt1, w_out1, w_up1, w_down1 = _post_mlp(h, y, tm=MLP_TM, name="mlstm_out_mlp",
                                                             to_cast=layer1_jobs, **mlp0)

    lambda_init = 0.8 - 0.6 * math.exp(-0.3 * 1)
    in1 = dict(gain=norm_gains[1, 0][None], weights=[w_q1, w_k1, w_vt1], scales=(DQK ** -0.5 * LOG2E, 1.0, 1.0),
               transposed=(False, False, True), sigmoid=(False,) * 3, dtypes=(BF16, BF16, BF16))
    _, k_meta, vt_meta = _norm_proj(h_meta, tm=META_ROWS, name="diff_in_proj_meta", **in1)
    k_meta = k_meta[META_ROWS - ATT_META_ROWS:]
    vt_meta = vt_meta[:, META_ROWS - ATT_META_ROWS:]
    q, k, vt = _norm_proj(h, tm=PROJ_TM, name="diff_in_proj", **in1)
    y = _diff_attention(q, k, vt, k_meta, vt_meta, diff_lambda[0], diff_head_gain[0][None], batch, lambda_init)
    out, = _post_mlp(h, y, w_out1, norm_gains[1, 1][None], w_up1, w_down1, final_gain[None], final_norm=True,
                     tm=MLP_TM, name="diff_out_mlp")
    return out.reshape(batch, seq, D_MODEL)
```

```python
import functools
import math

import jax
import jax.numpy as jnp
from jax import lax
from jax.experimental import pallas as pl
from jax.experimental.pallas import tpu as pltpu

D_MODEL = 1024
D_FF = 4 * D_MODEL
N_META = 16
CHUNK = 64
N_HEADS = 8
DQK = 64
DV = 128
EPS = 1e-6
SUBLN_EPS = 1e-5
LOG2E = math.log2(math.e)

LANES = 128
BF16_ROWS = 16
AUG_ROWS = BF16_ROWS
VAUG_ROWS = DV + AUG_ROWS
GATE_ROWS = 2 * N_HEADS

MCHUNK = 256
MLSTM_CHUNKS_PER_STEP = 8
MLSTM_GROUP = 1
META_ROWS = MCHUNK
ATT_META_ROWS = CHUNK
ATT_META_PAD = ATT_META_ROWS - N_META
ATT_TQ = 1024
ATT_TK = 256
ATT_AHEAD = 5

PROJ_TM = 1024
PROJ_SUB = 512
MLP_TM = 1024
MLP_FF_BLOCK = 512

VMEM_LIMIT = 56 * 1024 * 1024

F32 = jnp.float32
BF16 = jnp.bfloat16
NT_DIMS = (((1,), (1,)), ((), ()))


def _compiler_params(semantics):
    return pltpu.CompilerParams(dimension_semantics=semantics, vmem_limit_bytes=VMEM_LIMIT)


def _resident(shape):
    return pl.BlockSpec(shape, lambda *_: (0,) * len(shape), pipeline_mode=pl.Buffered(1))


def _ones_rows(width):
    return (lax.broadcasted_iota(jnp.int32, (AUG_ROWS, width), 0) == 0).astype(BF16)


def _cast_job(stacked, layer, col_block=0, n_col_blocks=1, transpose=False):
    return dict(array=stacked, layer=layer, col_block=col_block, n_col_blocks=n_col_blocks, transpose=transpose)


def _cast_specs(jobs, steps):
    in_specs, out_specs, out_shapes = [], [], []
    for job in jobs:
        _, n_rows, n_cols = job["array"].shape
        cols = n_cols // job["n_col_blocks"]
        layer, col_block = job["layer"], job["col_block"]
        if job["transpose"]:
            per_slab = steps * LANES // n_rows
            assert per_slab * n_rows == steps * LANES
            in_specs.append(pl.BlockSpec((None, LANES, cols),
                                         lambda i, l=layer, c=col_block, p=per_slab: (l, i // p, c)))
            out_specs.append(pl.BlockSpec((cols, LANES), lambda i, p=per_slab: (0, i // p)))
            out_shapes.append(jax.ShapeDtypeStruct((cols, n_rows), BF16))
        else:
            rows = n_rows // steps
            assert rows * steps == n_rows and rows % BF16_ROWS == 0
            in_specs.append(pl.BlockSpec((None, rows, cols), lambda i, l=layer, c=col_block: (l, i, c)))
            out_specs.append(pl.BlockSpec((rows, cols), lambda i: (i, 0)))
            out_shapes.append(jax.ShapeDtypeStruct((n_rows, cols), BF16))
    return in_specs, out_specs, out_shapes


def _cast_slabs(src_refs, dst_refs, transposed):
    for src, dst, tr in zip(src_refs, dst_refs, transposed):
        x = src[...]
        dst[...] = (x.T if tr else x).astype(dst.dtype)


def _norm_proj_kernel(x_ref, g_ref, *refs, scales, transposed, sigmoid, cast_transposed):
    n_out, n_cast = len(scales), len(cast_transposed)
    w_refs, out_refs = refs[:n_out], refs[n_out + n_cast:2 * n_out + n_cast]
    _cast_slabs(refs[n_out:n_out + n_cast], refs[2 * n_out + n_cast:], cast_transposed)
    tm = x_ref.shape[0]
    sub = min(PROJ_SUB, tm)
    for r0 in range(0, tm, sub):
        rows = slice(r0, r0 + sub)
        x = x_ref[rows, :]
        ms = jnp.mean(x * x, axis=-1, keepdims=True)
        xn = (x * lax.rsqrt(ms + EPS) * g_ref[...]).astype(BF16)
        for w_ref, o_ref, sc, tr, sg in zip(w_refs, out_refs, scales, transposed, sigmoid):
            if tr:
                r = lax.dot_general(w_ref[...], xn, NT_DIMS, preferred_element_type=F32)
            else:
                r = jnp.dot(xn, w_ref[...], preferred_element_type=F32)
            if sc != 1.0:
                r = r * sc
            if sg:
                r = 0.5 * jnp.tanh(0.5 * r) + 0.5
            if tr:
                o_ref[:, rows] = r.astype(o_ref.dtype)
            else:
                o_ref[rows, :] = r.astype(o_ref.dtype)


def _norm_proj(h, gain, weights, scales, transposed, sigmoid, dtypes, tm, name, to_cast=()):
    t = h.shape[0]
    assert t % tm == 0 and tm % min(PROJ_SUB, tm) == 0
    cast_in, cast_out, cast_shapes = _cast_specs(to_cast, t // tm)
    w_specs, out_specs, out_shapes = [], [], []
    for w, tr, dt in zip(weights, transposed, dtypes):
        w_specs.append(_resident(w.shape))
        if tr:
            n = w.shape[0]
            out_specs.append(pl.BlockSpec((n, tm), lambda i: (0, i)))
            out_shapes.append(jax.ShapeDtypeStruct((n, t), dt))
        else:
            n = w.shape[1]
            out_specs.append(pl.BlockSpec((tm, n), lambda i: (i, 0)))
            out_shapes.append(jax.ShapeDtypeStruct((t, n), dt))
    return pl.pallas_call(
        functools.partial(_norm_proj_kernel, scales=scales, transposed=transposed, sigmoid=sigmoid,
                          cast_transposed=tuple(job["transpose"] for job in to_cast)),
        grid=(t // tm,),
        in_specs=[pl.BlockSpec((tm, D_MODEL), lambda i: (i, 0)), _resident((1, D_MODEL))] + w_specs + cast_in,
        out_specs=out_specs + cast_out,
        out_shape=out_shapes + cast_shapes,
        compiler_params=_compiler_params(("arbitrary",)),
        name=name,
    )(h, gain, *weights, *[job["array"] for job in to_cast])


def _mlstm_kernel(*refs, chunks_per_step, group):
    q_ref, k_ref, o_ref = refs[0:3]
    vt_refs = refs[3:3 + group]
    gt_refs = refs[3 + group:3 + 2 * group]
    (bias_ref, gain_ref, ct0_ref, m0_ref, y_ref, ct_out_ref, m_out_ref,
     ct_ref, m_ref, b_ref, r_ref, cm_ref, rt_ref) = refs[3 + 2 * group:]
    t = MCHUNK

    @pl.when(pl.program_id(1) == 0)
    def _():
        for g in range(group):
            ct_ref[g] = ct0_ref[...]
            m_ref[g] = m0_ref[...]

    neg_inf = jnp.float32(-jnp.inf)
    causal = (lax.broadcasted_iota(jnp.int32, (t, t), 0)
              <= lax.broadcasted_iota(jnp.int32, (t, t), 1))
    upper = causal.astype(BF16)
    lane_t = lax.broadcasted_iota(jnp.int32, (N_HEADS, t), 1)
    bias = bias_ref[...]

    for g in range(group):
        for c in range(chunks_per_step):
            gates = gt_refs[g][:, c * t:(c + 1) * t] + bias
            ig = gates[0:N_HEADS]
            lf = jax.nn.log_sigmoid(gates[N_HEADS:])
            hi = lf.astype(BF16).astype(F32)
            mid = (lf - hi).astype(BF16).astype(F32)
            lo = (lf - hi - mid).astype(BF16).astype(F32)
            parts = jnp.dot(jnp.concatenate([hi, mid, lo], axis=0).astype(BF16), upper,
                            preferred_element_type=F32)
            b = parts[0:N_HEADS] + parts[N_HEADS:2 * N_HEADS] + parts[2 * N_HEADS:]
            r = ig - b
            cm = r
            shift = 1
            while shift < t:
                cm = jnp.maximum(cm, jnp.where(lane_t >= shift, pltpu.roll(cm, shift, 1), neg_inf))
                shift *= 2
            b_ref[g, c] = b
            r_ref[g, c] = r
            cm_ref[g, c] = cm
            rt_ref[g, c] = (r * LOG2E).T

    low = lax.broadcasted_iota(jnp.int32, (t, LANES), 1) < DQK
    low_state = lax.broadcasted_iota(jnp.int32, (VAUG_ROWS, LANES), 1) < DQK
    ones_rows = _ones_rows(t)
    half = t // 2
    tri = causal[:half, :half]

    def chunk_body(c, carry):
        rows = pl.ds(pl.multiple_of(c * t, t), t)
        row_terms = []
        for g in range(group):
            b, r, cm = b_ref[g, c], r_ref[g, c], cm_ref[g, c]
            m_prev = m_ref[g, :, 0:1]
            u = jnp.maximum(m_prev, cm)
            b_end = b[:, t - 1:t]
            a = b_end + r
            m_new = jnp.maximum(b_end + m_prev, jnp.max(a, axis=1, keepdims=True))
            row_terms.append(dict(
                inter=jnp.exp(m_prev - u), e_row=jnp.exp(-(b + u)), u2=u * LOG2E,
                decay=jnp.exp(b_end + m_prev - m_new), w=jnp.exp(a - m_new).astype(BF16), rt=rt_ref[g, c]))
            m_ref[g] = jnp.broadcast_to(m_new, m_ref.shape[1:])

        def pair_scores(g, p):
            q2 = q_ref[g, rows, p * LANES:(p + 1) * LANES]
            k2 = k_ref[g, rows, p * LANES:(p + 1) * LANES]
            zero = jnp.zeros_like(q2)
            qs = jnp.concatenate([jnp.where(low, q2, zero), jnp.where(low, zero, q2)], axis=0)
            return q2, k2, lax.dot_general(k2, qs, NT_DIMS, preferred_element_type=F32)

        seq = [(g, p) for p in range(N_HEADS // 2) for g in range(group)]
        nxt = pair_scores(*seq[0])
        for n, (g, p) in enumerate(seq):
            q2, k2, st = nxt
            terms = row_terms[g]
            heads = (2 * p, 2 * p + 1)
            cts = [ct_ref[g, h] for h in heads]
            vaugs = [jnp.concatenate([vt_refs[g][h * DV:(h + 1) * DV, rows], ones_rows], axis=0) for h in heads]
            inter_nums = [lax.dot_general(ct.astype(BF16), q2, NT_DIMS, preferred_element_type=F32)
                          for ct in cts]
            for i, h in enumerate(heads):
                wv = vaugs[i] * terms["w"][h:h + 1, :]
                upd = jnp.dot(wv, k2, preferred_element_type=F32)
                own = low_state if i == 0 else jnp.logical_not(low_state)
                ct_ref[g, h] = terms["decay"][h:h + 1, :] * cts[i] + jnp.where(own, upd, 0.0)
            if n + 1 < len(seq):
                nxt = pair_scores(*seq[n + 1])
            for i, h in enumerate(heads):
                cols = slice(h * DV, (h + 1) * DV)
                r_col, u_row = terms["rt"][:, h:h + 1], terms["u2"][h:h + 1, :]
                st_h = st[:, i * t:(i + 1) * t]
                d00 = jnp.exp2(jnp.where(tri, r_col[:half] - u_row[:, :half], neg_inf))
                d01 = jnp.exp2(r_col[:half] - u_row[:, half:])
                d11 = jnp.exp2(jnp.where(tri, r_col[half:] - u_row[:, half:], neg_inf))
                s_h = jnp.concatenate(
                    [jnp.concatenate([st_h[:half, :half] * d00, st_h[:half, half:] * d01], axis=1),
                     jnp.concatenate([jnp.zeros((half, half), F32), st_h[half:, half:] * d11], axis=1)],
                    axis=0).astype(BF16)
                num = (terms["inter"][h:h + 1, :] * inter_nums[i]
                       + jnp.dot(vaugs[i], s_h, preferred_element_type=F32))
                inv = 1.0 / jnp.maximum(jnp.abs(num[DV:DV + 1, :]), terms["e_row"][h:h + 1, :])
                hh = num[0:DV, :] * inv
                hh = hh * lax.rsqrt(jnp.mean(hh * hh, axis=0, keepdims=True) + EPS)
                y_ref[g, rows, cols] = (hh.T * gain_ref[:, cols]).astype(BF16) * o_ref[g, rows, cols]
        return carry

    lax.fori_loop(0, chunks_per_step, chunk_body, 0)
    ct_out_ref[...] = ct_ref[...]
    m_out_ref[...] = m_ref[...]


def _mlstm_core(q, k, vt, o, gates_t, bias, gain, ct0, m0, batch, chunks_per_step, group, name):
    tokens = q.shape[0]
    rows = chunks_per_step * MCHUNK
    per_group = batch // group
    steps = tokens // batch // rows
    assert per_group * group == batch and batch * steps * rows == tokens
    ct_shape = (N_HEADS, VAUG_ROWS, LANES)
    m_shape = (N_HEADS, LANES)
    gate_shape = (group, chunks_per_step, N_HEADS, MCHUNK)

    def tok(width):
        return pl.BlockSpec((group, rows, width), lambda b, j: (0, b * steps + j, 0))

    def tok_t(height, g):
        return pl.BlockSpec((height, rows), lambda b, j, g=g: (0, (g * per_group + b) * steps + j))

    def grouped(x):
        return x.reshape(group, tokens // group, x.shape[1])

    y, ct, m = pl.pallas_call(
        functools.partial(_mlstm_kernel, chunks_per_step=chunks_per_step, group=group),
        grid=(per_group, steps),
        in_specs=([tok(N_HEADS * DQK), tok(N_HEADS * DQK), tok(N_HEADS * DV)]
                  + [tok_t(N_HEADS * DV, g) for g in range(group)]
                  + [tok_t(GATE_ROWS, g) for g in range(group)]
                  + [_resident((2 * N_HEADS, 1)), _resident((1, N_HEADS * DV)),
                     _resident(ct_shape), _resident(m_shape)]),
        out_specs=[tok(N_HEADS * DV),
                   pl.BlockSpec((group, None) + ct_shape, lambda b, j: (0, b, 0, 0, 0)),
                   pl.BlockSpec((group, None) + m_shape, lambda b, j: (0, b, 0, 0))],
        out_shape=[jax.ShapeDtypeStruct((group, tokens // group, N_HEADS * DV), BF16),
                   jax.ShapeDtypeStruct((group, per_group) + ct_shape, F32),
                   jax.ShapeDtypeStruct((group, per_group) + m_shape, F32)],
        scratch_shapes=[pltpu.VMEM((group,) + ct_shape, F32), pltpu.VMEM((group,) + m_shape, F32),
                        pltpu.VMEM(gate_shape, F32), pltpu.VMEM(gate_shape, F32), pltpu.VMEM(gate_shape, F32),
                        pltpu.VMEM((group, chunks_per_step, MCHUNK, N_HEADS), F32)],
        compiler_params=_compiler_params(("parallel", "arbitrary")),
        name=name,
    )(grouped(q), grouped(k), grouped(o), *([vt] * group), *([gates_t] * group), bias, gain, ct0, m0)
    return y.reshape(tokens, N_HEADS * DV), ct, m


def _diff_attn_kernel(lam_ref, q_ref, k_ref, vt_ref, km_ref, vtm_ref, gain_ref, o_ref,
                      qs_ref, m_ref, acc_ref, sc_ref, *, lambda_init):
    qi = pl.program_id(1)
    tq, tk = ATT_TQ, ATT_TK
    blocks = tq // tk
    neg_inf = jnp.float32(-jnp.inf)

    low = lax.broadcasted_iota(jnp.int32, (tq, LANES), 1) < DQK
    for h in range(N_HEADS):
        q2 = q_ref[:, h * DV:(h + 1) * DV]
        zero = jnp.zeros_like(q2)
        qs_ref[h] = jnp.concatenate([jnp.where(low, q2, zero), jnp.where(low, zero, q2)], axis=0)

    all_units = [(h, br, blk) for h in range(N_HEADS) for br in range(2) for blk in range(blocks)]

    def unit_cols(br, blk):
        return pl.ds((br * blocks + blk) * tk, tk)

    def run(items, primed, next_items):
        stream = items + next_items

        def scores(item):
            tile, (h, br, blk) = item
            s = lax.dot_general(tile["k"](h), qs_ref[h, unit_cols(br, blk), :], NT_DIMS,
                                preferred_element_type=F32)
            bias = tile["bias"](blk)
            return s if bias is None else s + bias

        pending = [sc_ref[n] if primed else scores(stream[n]) for n in range(ATT_AHEAD)]
        for n, (tile, (h, br, blk)) in enumerate(items):
            s = pending.pop(0)
            ahead = n + ATT_AHEAD
            if ahead < len(items):
                pending.append(scores(stream[ahead]))
            elif ahead < len(stream):
                sc_ref[ahead - len(items)] = scores(stream[ahead])
            cols = unit_cols(br, blk)
            v_aug = jnp.concatenate([tile["vt"](h), tile["ones"]], axis=0)
            s_max = jnp.max(s, axis=0, keepdims=True)
            if tile["first"]:
                m_new = s_max
                acc_ref[h, :, cols] = jnp.dot(v_aug, jnp.exp2(s - m_new).astype(BF16), preferred_element_type=F32)
            else:
                m_old = m_ref[h, :, cols]
                m_new = jnp.maximum(m_old, s_max)
                alpha = jnp.exp2(m_old - m_new)
                p = jnp.exp2(s - m_new).astype(BF16)
                acc_ref[h, :, cols] = alpha * acc_ref[h, :, cols] + jnp.dot(v_aug, p, preferred_element_type=F32)
            m_ref[h, :, cols] = m_new
        assert not pending and len(items) >= ATT_AHEAD

    ones_tile = _ones_rows(tk)

    def frame_tile(start, bias_of):
        return dict(k=lambda h: k_ref[pl.ds(start, tk), h * DV:(h + 1) * DV],
                    vt=lambda h: vt_ref[h * DV:(h + 1) * DV, pl.ds(start, tk)],
                    ones=ones_tile, bias=bias_of, first=False)

    def full_items(start, units):
        tile = frame_tile(start, lambda blk: None)
        return [(tile, u) for u in units]

    chunk_shift = CHUNK.bit_length() - 1
    k_chunk = lax.shift_right_logical(lax.broadcasted_iota(jnp.int32, (tk, tk), 0), chunk_shift)
    q_chunk = lax.shift_right_logical(lax.broadcasted_iota(jnp.int32, (tk, tk), 1), chunk_shift)
    diag_bias = jnp.where(k_chunk <= q_chunk, 0.0, neg_inf)
    meta_row = lax.broadcasted_iota(jnp.int32, (ATT_META_ROWS, tk), 0)
    meta_bias = jnp.where(meta_row >= ATT_META_PAD, 0.0, neg_inf)
    first_bias = [jnp.concatenate([diag_bias if blk == 0 else jnp.zeros_like(diag_bias), meta_bias], axis=0)
                  for blk in range(min(blocks, 2))]
    start0 = pl.multiple_of(qi * tq, tk)
    first_tile = dict(
        k=lambda h: jnp.concatenate([k_ref[pl.ds(start0, tk), h * DV:(h + 1) * DV],
                                     km_ref[:, h * DV:(h + 1) * DV]], axis=0),
        vt=lambda h: jnp.concatenate([vt_ref[h * DV:(h + 1) * DV, pl.ds(start0, tk)],
                                      vtm_ref[h * DV:(h + 1) * DV, :]], axis=1),
        ones=_ones_rows(tk + ATT_META_ROWS), bias=lambda blk: first_bias[min(blk, 1)], first=True)
    items = [(first_tile, u) for u in all_units]
    for d in range(1, blocks):
        tile = frame_tile(pl.multiple_of(qi * tq + d * tk, tk), lambda blk, d=d: diag_bias if blk == d else None)
        items += [(tile, u) for u in all_units if u[2] >= d]
    head_units = all_units[:ATT_AHEAD]
    run(items, primed=False, next_items=full_items(0, head_units))

    n_full = qi * blocks
    last_tile = k_ref.shape[0] // tk - 1

    def full_tile(j, carry):
        nxt = pl.multiple_of(jnp.minimum(j + 1, last_tile) * tk, tk)
        run(full_items(pl.multiple_of(j * tk, tk), all_units), primed=True,
            next_items=full_items(nxt, head_units))
        return carry

    lax.fori_loop(0, n_full, full_tile, 0)

    lv = lam_ref[...]
    lam = (jnp.exp(jnp.sum(lv[0:1] * lv[1:2], axis=-1, keepdims=True))
           - jnp.exp(jnp.sum(lv[2:3] * lv[3:4], axis=-1, keepdims=True)) + lambda_init)
    out_gain = gain_ref[...] * (1.0 - lambda_init)
    for h in range(N_HEADS):
        cols = slice(h * DV, (h + 1) * DV)
        acc = acc_ref[h]
        inv_l = 1.0 / acc[DV:DV + 1, :]
        o_t = acc[0:DV, :tq] * inv_l[:, :tq] - lam * (acc[0:DV, tq:] * inv_l[:, tq:])
        o_t = o_t * lax.rsqrt(jnp.mean(o_t * o_t, axis=0, keepdims=True) + SUBLN_EPS)
        o_ref[:, cols] = (o_t.T * out_gain[:, cols]).astype(o_ref.dtype)


def _diff_attention(q, k, vt, k_meta, vt_meta, lam_vecs, gain, batch, lambda_init):
    t = q.shape[0]
    seq = t // batch
    nq = seq // ATT_TQ
    assert nq * ATT_TQ == seq and ATT_TQ % ATT_TK == 0 and ATT_TK % CHUNK == 0
    width = N_HEADS * DV
    return pl.pallas_call(
        functools.partial(_diff_attn_kernel, lambda_init=lambda_init),
        grid=(batch, nq),
        in_specs=[_resident(lam_vecs.shape),
                  pl.BlockSpec((ATT_TQ, width), lambda b, i: (b * nq + i, 0)),
                  pl.BlockSpec((seq, width), lambda b, i: (b, 0)),
                  pl.BlockSpec((width, seq), lambda b, i: (0, b)),
                  _resident(k_meta.shape),
                  _resident(vt_meta.shape),
                  _resident((1, width))],
        out_specs=pl.BlockSpec((ATT_TQ, width), lambda b, i: (b * nq + i, 0)),
        out_shape=jax.ShapeDtypeStruct((t, width), BF16),
        scratch_shapes=[pltpu.VMEM((N_HEADS, 2 * ATT_TQ, DV), BF16),
                        pltpu.VMEM((N_HEADS, 1, 2 * ATT_TQ), F32),
                        pltpu.VMEM((N_HEADS, VAUG_ROWS, 2 * ATT_TQ), F32),
                        pltpu.VMEM((ATT_AHEAD, ATT_TK, ATT_TK), F32)],
        compiler_params=_compiler_params(("parallel", "arbitrary")),
        name="diff_attention",
    )(lam_vecs, q, k, vt, k_meta, vt_meta, gain)


def _post_mlp_kernel(h_ref, y_ref, wo_ref, g_ref, wu_ref, wd_ref, fg_ref, *refs, final_norm, ff_block,
                     cast_transposed):
    n_cast = len(cast_transposed)
    out_ref = refs[n_cast]
    _cast_slabs(refs[:n_cast], refs[n_cast + 1:], cast_transposed)
    h1 = h_ref[...] + jnp.dot(y_ref[...], wo_ref[...], preferred_element_type=F32)
    ms = jnp.mean(h1 * h1, axis=-1, keepdims=True)
    hn = (h1 * lax.rsqrt(ms + EPS) * g_ref[...]).astype(BF16)
    acc = h1
    for c in range(D_FF // ff_block):
        blk = slice(c * ff_block, (c + 1) * ff_block)
        u = jnp.dot(hn, wu_ref[:, blk], preferred_element_type=F32)
        a = jnp.square(jnp.maximum(u, 0.0)).astype(BF16)
        acc = acc + jnp.dot(a, wd_ref[blk, :], preferred_element_type=F32)
    if final_norm:
        ms = jnp.mean(acc * acc, axis=-1, keepdims=True)
        acc = acc * lax.rsqrt(ms + EPS) * fg_ref[...]
    out_ref[...] = acc


def _post_mlp(h, y, w_out, gain, w_up, w_down, final_gain, final_norm, tm, name, to_cast=()):
    t = h.shape[0]
    assert t % tm == 0
    cast_in, cast_out, cast_shapes = _cast_specs(to_cast, t // tm)
    return pl.pallas_call(
        functools.partial(_post_mlp_kernel, final_norm=final_norm, ff_block=MLP_FF_BLOCK,
                          cast_transposed=tuple(job["transpose"] for job in to_cast)),
        grid=(t // tm,),
        in_specs=[pl.BlockSpec((tm, D_MODEL), lambda i: (i, 0)),
                  pl.BlockSpec((tm, D_MODEL), lambda i: (i, 0)),
                  _resident((D_MODEL, D_MODEL)),
                  _resident((1, D_MODEL)),
                  _resident((D_MODEL, D_FF)),
                  _resident((D_FF, D_MODEL)),
                  _resident((1, D_MODEL))] + cast_in,
        out_specs=[pl.BlockSpec((tm, D_MODEL), lambda i: (i, 0))] + cast_out,
        out_shape=[jax.ShapeDtypeStruct((t, D_MODEL), F32)] + cast_shapes,
        compiler_params=_compiler_params(("arbitrary",)),
        name=name,
    )(h, y, w_out, gain, w_up, w_down, final_gain, *[job["array"] for job in to_cast])


def kernel(x, meta_tokens, norm_gains, mlstm_w_in, mlstm_b_gate, mlstm_head_gain, mlstm_w_out,
           diff_w_in, diff_lambda, diff_head_gain, diff_w_out, mlp_w_up, mlp_w_down, final_gain):
    batch, seq, _ = x.shape
    t = batch * seq
    h = x.reshape(t, D_MODEL)
    h_meta = jnp.concatenate([jnp.zeros((META_ROWS - N_META, D_MODEL), x.dtype), meta_tokens.astype(x.dtype)],
                             axis=0)

    m_qk = N_HEADS * DQK
    m_v = N_HEADS * DV
    w0 = mlstm_w_in[0]
    offs = (0, m_qk, 2 * m_qk, 2 * m_qk + m_v, 2 * m_qk + 2 * m_v)
    w0_parts = [w0[:, offs[3]:offs[4]], w0[:, offs[2]:offs[3]].T, w0[:, offs[0]:offs[1]], w0[:, offs[1]:offs[2]],
                w0[:, offs[4]:].T]
    w0_parts = [w.astype(BF16) for w in w0_parts]
    in0 = dict(gain=norm_gains[0, 0][None], weights=w0_parts, scales=(1.0, 1.0, DQK ** -0.5, 1.0, 1.0),
               transposed=(False, True, False, False, True), sigmoid=(True, False, False, False, False),
               dtypes=(BF16, BF16, BF16, BF16, F32))
    bias = mlstm_b_gate[0][:, None]
    head_gain0 = mlstm_head_gain[0][None]

    o_gate, vt, q, k, gates_t = _norm_proj(h_meta, tm=META_ROWS, name="mlstm_in_proj_meta", **in0)
    ct_zero = jnp.zeros((N_HEADS, VAUG_ROWS, LANES), F32)
    m_zero = jnp.zeros((N_HEADS, LANES), F32)
    y_meta, ct_meta, m_meta = _mlstm_core(q, k, vt, o_gate, gates_t, bias, head_gain0, ct_zero, m_zero, batch=1,
                                          chunks_per_step=1, group=1, name="mlstm_core_meta")

    o_gate, vt, q, k, gates_t, w_out0, w_up0, w_down0 = _norm_proj(
        h, tm=PROJ_TM, name="mlstm_in_proj",
        to_cast=(_cast_job(mlstm_w_out, 0), _cast_job(mlp_w_up, 0), _cast_job(mlp_w_down, 0)), **in0)
    mlp0 = dict(w_out=w_out0, gain=norm_gains[0, 1][None], w_up=w_up0, w_down=w_down0,
                final_gain=final_gain[None], final_norm=False)
    h_meta, = _post_mlp(h_meta, y_meta, tm=META_ROWS, name="mlstm_out_mlp_meta", **mlp0)
    y, _, _ = _mlstm_core(q, k, vt, o_gate, gates_t, bias, head_gain0, ct_meta[0, 0], m_meta[0, 0], batch=batch,
                          chunks_per_step=MLSTM_CHUNKS_PER_STEP, group=MLSTM_GROUP, name="mlstm_core")
    layer1_jobs = (_cast_job(diff_w_in, 0, 0, 3), _cast_job(diff_w_in, 0, 1, 3),
                   _cast_job(diff_w_in, 0, 2, 3, transpose=True),
                   _cast_job(diff_w_out, 0), _cast_job(mlp_w_up, 1), _cast_job(mlp_w_down, 1))
    h, w_q1, w_k1, w_vt1, w_out1, w_up1, w_down1 = _post_mlp(h, y, tm=MLP_TM, name="mlstm_out_mlp",
                                                             to_cast=layer1_jobs, **mlp0)

    lambda_init = 0.8 - 0.6 * math.exp(-0.3 * 1)
    in1 = dict(gain=norm_gains[1, 0][None], weights=[w_q1, w_k1, w_vt1], scales=(DQK ** -0.5 * LOG2E, 1.0, 1.0),
               transposed=(False, False, True), sigmoid=(False,) * 3, dtypes=(BF16, BF16, BF16))
    _, k_meta, vt_meta = _norm_proj(h_meta, tm=META_ROWS, name="diff_in_proj_meta", **in1)
    k_meta = k_meta[META_ROWS - ATT_META_ROWS:]
    vt_meta = vt_meta[:, META_ROWS - ATT_META_ROWS:]
    q, k, vt = _norm_proj(h, tm=PROJ_TM, name="diff_in_proj", **in1)
    y = _diff_attention(q, k, vt, k_meta, vt_meta, diff_lambda[0], diff_head_gain[0][None], batch, lambda_init)
    out, = _post_mlp(h, y, w_out1, norm_gains[1, 1][None], w_up1, w_down1, final_gain[None], final_norm=True,
                     tm=MLP_TM, name="diff_out_mlp")
    return out.reshape(batch, seq, D_MODEL)
```

```python
import functools
import math

import jax
import jax.numpy as jnp
from jax import lax
from jax.experimental import pallas as pl
from jax.experimental.pallas import tpu as pltpu

D_MODEL = 1024
D_FF = 4 * D_MODEL
N_META = 16
CHUNK = 64
N_HEADS = 8
DQK = 64
DV = 128
EPS = 1e-6
SUBLN_EPS = 1e-5
LOG2E = math.log2(math.e)

LANES = 128
BF16_ROWS = 16
AUG_ROWS = BF16_ROWS
VAUG_ROWS = DV + AUG_ROWS
GATE_ROWS = 2 * N_HEADS

MCHUNK = 256
MLSTM_CHUNKS_PER_STEP = 8
MLSTM_GROUP = 1
MLSTM_AHEAD = 2
META_ROWS = MCHUNK
ATT_META_ROWS = CHUNK
ATT_META_PAD = ATT_META_ROWS - N_META
ATT_TQ = 1024
ATT_TK = 256
ATT_AHEAD = 5

PROJ_TM = 1024
PROJ_SUB = 512
MLP_TM = 1024
MLP_FF_BLOCK = 512

VMEM_LIMIT = 56 * 1024 * 1024

F32 = jnp.float32
BF16 = jnp.bfloat16
NT_DIMS = (((1,), (1,)), ((), ()))


def _compiler_params(semantics):
    return pltpu.CompilerParams(dimension_semantics=semantics, vmem_limit_bytes=VMEM_LIMIT)


def _resident(shape):
    return pl.BlockSpec(shape, lambda *_: (0,) * len(shape), pipeline_mode=pl.Buffered(1))


def _ones_rows(width):
    return (lax.broadcasted_iota(jnp.int32, (AUG_ROWS, width), 0) == 0).astype(BF16)


def _cast_job(stacked, layer, col_block=0, n_col_blocks=1, transpose=False):
    return dict(array=stacked, layer=layer, col_block=col_block, n_col_blocks=n_col_blocks, transpose=transpose)


def _cast_specs(jobs, steps):
    in_specs, out_specs, out_shapes = [], [], []
    for job in jobs:
        _, n_rows, n_cols = job["array"].shape
        cols = n_cols // job["n_col_blocks"]
        layer, col_block = job["layer"], job["col_block"]
        if job["transpose"]:
            per_slab = steps * LANES // n_rows
            assert per_slab * n_rows == steps * LANES
            in_specs.append(pl.BlockSpec((None, LANES, cols),
                                         lambda i, l=layer, c=col_block, p=per_slab: (l, i // p, c)))
            out_specs.append(pl.BlockSpec((cols, LANES), lambda i, p=per_slab: (0, i // p)))
            out_shapes.append(jax.ShapeDtypeStruct((cols, n_rows), BF16))
        else:
            rows = n_rows // steps
            assert rows * steps == n_rows and rows % BF16_ROWS == 0
            in_specs.append(pl.BlockSpec((None, rows, cols), lambda i, l=layer, c=col_block: (l, i, c)))
            out_specs.append(pl.BlockSpec((rows, cols), lambda i: (i, 0)))
            out_shapes.append(jax.ShapeDtypeStruct((n_rows, cols), BF16))
    return in_specs, out_specs, out_shapes


def _cast_slabs(src_refs, dst_refs, transposed):
    for src, dst, tr in zip(src_refs, dst_refs, transposed):
        x = src[...]
        dst[...] = (x.T if tr else x).astype(dst.dtype)


def _norm_proj_kernel(x_ref, g_ref, *refs, scales, transposed, sigmoid, pieces, cast_transposed):
    n_w, n_cast, n_out = len(scales), len(cast_transposed), sum(len(p) for p in pieces)
    w_refs, out_refs = refs[:n_w], list(refs[n_w + n_cast:n_w + n_cast + n_out])
    _cast_slabs(refs[n_w:n_w + n_cast], refs[n_w + n_cast + n_out:], cast_transposed)
    tm = x_ref.shape[0]
    sub = min(PROJ_SUB, tm)
    for r0 in range(0, tm, sub):
        rows = slice(r0, r0 + sub)
        x = x_ref[rows, :]
        ms = jnp.mean(x * x, axis=-1, keepdims=True)
        xn = (x * lax.rsqrt(ms + EPS) * g_ref[...]).astype(BF16)
        outs = iter(out_refs)
        for w_ref, sc, tr, sg, sizes in zip(w_refs, scales, transposed, sigmoid, pieces):
            if tr:
                r = lax.dot_general(w_ref[...], xn, NT_DIMS, preferred_element_type=F32)
            else:
                r = jnp.dot(xn, w_ref[...], preferred_element_type=F32)
            if sc != 1.0:
                r = r * sc
            if sg:
                r = 0.5 * jnp.tanh(0.5 * r) + 0.5
            if tr:
                off = 0
                for n in sizes:
                    o_ref = next(outs)
                    o_ref[:, rows] = r[off:off + n].astype(o_ref.dtype)
                    off += n
            else:
                o_ref = next(outs)
                o_ref[rows, :] = r.astype(o_ref.dtype)


def _norm_proj(h, gain, weights, scales, transposed, sigmoid, dtypes, tm, name, to_cast=()):
    t = h.shape[0]
    assert t % tm == 0 and tm % min(PROJ_SUB, tm) == 0
    cast_in, cast_out, cast_shapes = _cast_specs(to_cast, t // tm)
    w_specs, out_specs, out_shapes, pieces = [], [], [], []
    for w, tr, dt in zip(weights, transposed, dtypes):
        w_specs.append(_resident(w.shape))
        if tr:
            parts = dt if isinstance(dt, tuple) else ((w.shape[0], dt),)
            assert sum(n for n, _ in parts) == w.shape[0]
            pieces.append(tuple(n for n, _ in parts))
            for n, part_dt in parts:
                out_specs.append(pl.BlockSpec((n, tm), lambda i: (0, i)))
                out_shapes.append(jax.ShapeDtypeStruct((n, t), part_dt))
        else:
            n = w.shape[1]
            pieces.append((n,))
            out_specs.append(pl.BlockSpec((tm, n), lambda i: (i, 0)))
            out_shapes.append(jax.ShapeDtypeStruct((t, n), dt))
    return pl.pallas_call(
        functools.partial(_norm_proj_kernel, scales=scales, transposed=transposed, sigmoid=sigmoid,
                          pieces=tuple(pieces), cast_transposed=tuple(job["transpose"] for job in to_cast)),
        grid=(t // tm,),
        in_specs=[pl.BlockSpec((tm, D_MODEL), lambda i: (i, 0)), _resident((1, D_MODEL))] + w_specs + cast_in,
        out_specs=out_specs + cast_out,
        out_shape=out_shapes + cast_shapes,
        compiler_params=_compiler_params(("arbitrary",)),
        name=name,
    )(h, gain, *weights, *[job["array"] for job in to_cast])


def _mlstm_kernel(*refs, chunks_per_step, group):
    q_ref, k_ref, o_ref = refs[0:3]
    vt_refs = refs[3:3 + group]
    gt_refs = refs[3 + group:3 + 2 * group]
    (bias_ref, gain_ref, ct0_ref, m0_ref, y_ref, ct_out_ref, m_out_ref,
     ct_ref, m_ref, b_ref, r_ref, cm_ref, rt_ref) = refs[3 + 2 * group:]
    t = MCHUNK

    @pl.when(pl.program_id(1) == 0)
    def _():
        for g in range(group):
            ct_ref[g] = ct0_ref[...]
            m_ref[g] = m0_ref[...]

    neg_inf = jnp.float32(-jnp.inf)
    causal = (lax.broadcasted_iota(jnp.int32, (t, t), 0)
              <= lax.broadcasted_iota(jnp.int32, (t, t), 1))
    upper = causal.astype(BF16)
    lane_t = lax.broadcasted_iota(jnp.int32, (N_HEADS, t), 1)
    bias = bias_ref[...]

    for g in range(group):
        for c in range(chunks_per_step):
            gates = gt_refs[g][:, c * t:(c + 1) * t] + bias
            ig = gates[0:N_HEADS]
            lf = jax.nn.log_sigmoid(gates[N_HEADS:])
            hi = lf.astype(BF16).astype(F32)
            mid = (lf - hi).astype(BF16).astype(F32)
            lo = (lf - hi - mid).astype(BF16).astype(F32)
            parts = jnp.dot(jnp.concatenate([hi, mid, lo], axis=0).astype(BF16), upper,
                            preferred_element_type=F32)
            b = parts[0:N_HEADS] + parts[N_HEADS:2 * N_HEADS] + parts[2 * N_HEADS:]
            r = ig - b
            cm = r
            shift = 1
            while shift < t:
                cm = jnp.maximum(cm, jnp.where(lane_t >= shift, pltpu.roll(cm, shift, 1), neg_inf))
                shift *= 2
            b_ref[g, c] = b
            r_ref[g, c] = r
            cm_ref[g, c] = cm
            rt_ref[g, c] = (r * LOG2E).T

    low = lax.broadcasted_iota(jnp.int32, (t, LANES), 1) < DQK
    low_state = lax.broadcasted_iota(jnp.int32, (VAUG_ROWS, LANES), 1) < DQK
    ones_rows = _ones_rows(t)
    half = t // 2
    tri = causal[:half, :half]

    def chunk_body(c, carry):
        rows = pl.ds(pl.multiple_of(c * t, t), t)
        row_terms = []
        for g in range(group):
            b, r, cm = b_ref[g, c], r_ref[g, c], cm_ref[g, c]
            m_prev = m_ref[g, :, 0:1]
            u = jnp.maximum(m_prev, cm)
            b_end = b[:, t - 1:t]
            a = b_end + r
            m_new = jnp.maximum(b_end + m_prev, jnp.max(a, axis=1, keepdims=True))
            row_terms.append(dict(
                inter=jnp.exp(m_prev - u), e_row=jnp.exp(-(b + u)), u2=u * LOG2E,
                decay=jnp.exp(b_end + m_prev - m_new), w=jnp.exp(a - m_new).astype(BF16), rt=rt_ref[g, c]))
            m_ref[g] = jnp.broadcast_to(m_new, m_ref.shape[1:])

        def qk_tiles(g, h):
            pair = slice(h // 2 * LANES, (h // 2 + 1) * LANES)
            return q_ref[g, rows, pair], k_ref[g, rows, pair]

        def head_scores(g, h):
            q2, k2 = qk_tiles(g, h)
            q_own = jnp.where(low if h % 2 == 0 else jnp.logical_not(low), q2, jnp.zeros_like(q2))
            return lax.dot_general(k2, q_own, NT_DIMS, preferred_element_type=F32)

        seq = [(g, h) for h in range(N_HEADS) for g in range(group)]
        pending = [head_scores(*u) for u in seq[:MLSTM_AHEAD]]
        for n, (g, h) in enumerate(seq):
            st_h = pending.pop(0)
            terms = row_terms[g]
            q2, k2 = qk_tiles(g, h)
            ct = ct_ref[g, h]
            vaug = jnp.concatenate([vt_refs[g][h * DV:(h + 1) * DV, rows], ones_rows], axis=0)
            inter_num = lax.dot_general(ct.astype(BF16), q2, NT_DIMS, preferred_element_type=F32)
            upd = jnp.dot(vaug * terms["w"][h:h + 1, :], k2, preferred_element_type=F32)
            own = low_state if h % 2 == 0 else jnp.logical_not(low_state)
            ct_ref[g, h] = terms["decay"][h:h + 1, :] * ct + jnp.where(own, upd, 0.0)
            if n + MLSTM_AHEAD < len(seq):
                pending.append(head_scores(*seq[n + MLSTM_AHEAD]))
            cols = slice(h * DV, (h + 1) * DV)
            r_col, u_row = terms["rt"][:, h:h + 1], terms["u2"][h:h + 1, :]
            d00 = jnp.exp2(jnp.where(tri, r_col[:half] - u_row[:, :half], neg_inf))
            d01 = jnp.exp2(r_col[:half] - u_row[:, half:])
            d11 = jnp.exp2(jnp.where(tri, r_col[half:] - u_row[:, half:], neg_inf))
            s_h = jnp.concatenate(
                [jnp.concatenate([st_h[:half, :half] * d00, st_h[:half, half:] * d01], axis=1),
                 jnp.concatenate([jnp.zeros((half, half), F32), st_h[half:, half:] * d11], axis=1)],
                axis=0).astype(BF16)
            num = (terms["inter"][h:h + 1, :] * inter_num
                   + jnp.dot(vaug, s_h, preferred_element_type=F32))
            inv = 1.0 / jnp.maximum(jnp.abs(num[DV:DV + 1, :]), terms["e_row"][h:h + 1, :])
            hh = num[0:DV, :] * inv
            hh = hh * lax.rsqrt(jnp.mean(hh * hh, axis=0, keepdims=True) + EPS)
            y_ref[g, rows, cols] = (hh.T * gain_ref[:, cols]).astype(BF16) * o_ref[g, rows, cols]
        return carry

    lax.fori_loop(0, chunks_per_step, chunk_body, 0)
    ct_out_ref[...] = ct_ref[...]
    m_out_ref[...] = m_ref[...]


def _mlstm_core(q, k, vt, o, gates_t, bias, gain, ct0, m0, batch, chunks_per_step, group, name):
    tokens = q.shape[0]
    rows = chunks_per_step * MCHUNK
    per_group = batch // group
    steps = tokens // batch // rows
    assert per_group * group == batch and batch * steps * rows == tokens
    ct_shape = (N_HEADS, VAUG_ROWS, LANES)
    m_shape = (N_HEADS, LANES)
    gate_shape = (group, chunks_per_step, N_HEADS, MCHUNK)

    def tok(width):
        return pl.BlockSpec((group, rows, width), lambda b, j: (0, b * steps + j, 0))

    def tok_t(height, g):
        return pl.BlockSpec((height, rows), lambda b, j, g=g: (0, (g * per_group + b) * steps + j))

    def grouped(x):
        return x.reshape(group, tokens // group, x.shape[1])

    y, ct, m = pl.pallas_call(
        functools.partial(_mlstm_kernel, chunks_per_step=chunks_per_step, group=group),
        grid=(per_group, steps),
        in_specs=([tok(N_HEADS * DQK), tok(N_HEADS * DQK), tok(N_HEADS * DV)]
                  + [tok_t(N_HEADS * DV, g) for g in range(group)]
                  + [tok_t(GATE_ROWS, g) for g in range(group)]
                  + [_resident((2 * N_HEADS, 1)), _resident((1, N_HEADS * DV)),
                     _resident(ct_shape), _resident(m_shape)]),
        out_specs=[tok(N_HEADS * DV),
                   pl.BlockSpec((group, None) + ct_shape, lambda b, j: (0, b, 0, 0, 0)),
                   pl.BlockSpec((group, None) + m_shape, lambda b, j: (0, b, 0, 0))],
        out_shape=[jax.ShapeDtypeStruct((group, tokens // group, N_HEADS * DV), BF16),
                   jax.ShapeDtypeStruct((group, per_group) + ct_shape, F32),
                   jax.ShapeDtypeStruct((group, per_group) + m_shape, F32)],
        scratch_shapes=[pltpu.VMEM((group,) + ct_shape, F32), pltpu.VMEM((group,) + m_shape, F32),
                        pltpu.VMEM(gate_shape, F32), pltpu.VMEM(gate_shape, F32), pltpu.VMEM(gate_shape, F32),
                        pltpu.VMEM((group, chunks_per_step, MCHUNK, N_HEADS), F32)],
        compiler_params=_compiler_params(("parallel", "arbitrary")),
        name=name,
    )(grouped(q), grouped(k), grouped(o), *([vt] * group), *([gates_t] * group), bias, gain, ct0, m0)
    return y.reshape(tokens, N_HEADS * DV), ct, m


def _diff_attn_kernel(lam_ref, q_ref, k_ref, vt_ref, km_ref, vtm_ref, gain_ref, o_ref,
                      qs_ref, m_ref, acc_ref, sc_ref, *, lambda_init):
    qi = pl.program_id(1)
    tq, tk = ATT_TQ, ATT_TK
    blocks = tq // tk
    neg_inf = jnp.float32(-jnp.inf)

    low = lax.broadcasted_iota(jnp.int32, (tq, LANES), 1) < DQK
    for h in range(N_HEADS):
        q2 = q_ref[:, h * DV:(h + 1) * DV]
        zero = jnp.zeros_like(q2)
        qs_ref[h] = jnp.concatenate([jnp.where(low, q2, zero), jnp.where(low, zero, q2)], axis=0)

    all_units = [(h, br, blk) for h in range(N_HEADS) for br in range(2) for blk in range(blocks)]

    def unit_cols(br, blk):
        return pl.ds((br * blocks + blk) * tk, tk)

    def run(items, primed, next_items):
        stream = items + next_items

        def scores(item):
            tile, (h, br, blk) = item
            s = lax.dot_general(tile["k"](h), qs_ref[h, unit_cols(br, blk), :], NT_DIMS,
                                preferred_element_type=F32)
            bias = tile["bias"](blk)
            return s if bias is None else s + bias

        pending = [sc_ref[n] if primed else scores(stream[n]) for n in range(ATT_AHEAD)]
        for n, (tile, (h, br, blk)) in enumerate(items):
            s = pending.pop(0)
            ahead = n + ATT_AHEAD
            if ahead < len(items):
                pending.append(scores(stream[ahead]))
            elif ahead < len(stream):
                sc_ref[ahead - len(items)] = scores(stream[ahead])
            cols = unit_cols(br, blk)
            v_aug = jnp.concatenate([tile["vt"](h), tile["ones"]], axis=0)
            s_max = jnp.max(s, axis=0, keepdims=True)
            if tile["first"]:
                m_new = s_max
                acc_ref[h, :, cols] = jnp.dot(v_aug, jnp.exp2(s - m_new).astype(BF16), preferred_element_type=F32)
            else:
                m_old = m_ref[h, :, cols]
                m_new = jnp.maximum(m_old, s_max)
                alpha = jnp.exp2(m_old - m_new)
                p = jnp.exp2(s - m_new).astype(BF16)
                acc_ref[h, :, cols] = alpha * acc_ref[h, :, cols] + jnp.dot(v_aug, p, preferred_element_type=F32)
            m_ref[h, :, cols] = m_new
        assert not pending and len(items) >= ATT_AHEAD

    ones_tile = _ones_rows(tk)

    def frame_tile(start, bias_of):
        return dict(k=lambda h: k_ref[pl.ds(start, tk), h * DV:(h + 1) * DV],
                    vt=lambda h: vt_ref[h * DV:(h + 1) * DV, pl.ds(start, tk)],
                    ones=ones_tile, bias=bias_of, first=False)

    def full_items(start, units):
        tile = frame_tile(start, lambda blk: None)
        return [(tile, u) for u in units]

    chunk_shift = CHUNK.bit_length() - 1
    k_chunk = lax.shift_right_logical(lax.broadcasted_iota(jnp.int32, (tk, tk), 0), chunk_shift)
    q_chunk = lax.shift_right_logical(lax.broadcasted_iota(jnp.int32, (tk, tk), 1), chunk_shift)
    diag_bias = jnp.where(k_chunk <= q_chunk, 0.0, neg_inf)
    meta_row = lax.broadcasted_iota(jnp.int32, (ATT_META_ROWS, tk), 0)
    meta_bias = jnp.where(meta_row >= ATT_META_PAD, 0.0, neg_inf)
    first_bias = [jnp.concatenate([diag_bias if blk == 0 else jnp.zeros_like(diag_bias), meta_bias], axis=0)
                  for blk in range(min(blocks, 2))]
    start0 = pl.multiple_of(qi * tq, tk)
    first_tile = dict(
        k=lambda h: jnp.concatenate([k_ref[pl.ds(start0, tk), h * DV:(h + 1) * DV],
                                     km_ref[:, h * DV:(h + 1) * DV]], axis=0),
        vt=lambda h: jnp.concatenate([vt_ref[h * DV:(h + 1) * DV, pl.ds(start0, tk)],
                                      vtm_ref[h * DV:(h + 1) * DV, :]], axis=1),
        ones=_ones_rows(tk + ATT_META_ROWS), bias=lambda blk: first_bias[min(blk, 1)], first=True)
    items = [(first_tile, u) for u in all_units]
    for d in range(1, blocks):
        tile = frame_tile(pl.multiple_of(qi * tq + d * tk, tk), lambda blk, d=d: diag_bias if blk == d else None)
        items += [(tile, u) for u in all_units if u[2] >= d]
    head_units = all_units[:ATT_AHEAD]
    run(items, primed=False, next_items=full_items(0, head_units))

    n_full = qi * blocks
    last_tile = k_ref.shape[0] // tk - 1

    def full_tile(j, carry):
        nxt = pl.multiple_of(jnp.minimum(j + 1, last_tile) * tk, tk)
        run(full_items(pl.multiple_of(j * tk, tk), all_units), primed=True,
            next_items=full_items(nxt, head_units))
        return carry

    lax.fori_loop(0, n_full, full_tile, 0)

    lv = lam_ref[...]
    lam = (jnp.exp(jnp.sum(lv[0:1] * lv[1:2], axis=-1, keepdims=True))
           - jnp.exp(jnp.sum(lv[2:3] * lv[3:4], axis=-1, keepdims=True)) + lambda_init)
    out_gain = gain_ref[...] * (1.0 - lambda_init)
    for h in range(N_HEADS):
        cols = slice(h * DV, (h + 1) * DV)
        acc = acc_ref[h]
        inv_l = 1.0 / acc[DV:DV + 1, :]
        o_t = acc[0:DV, :tq] * inv_l[:, :tq] - lam * (acc[0:DV, tq:] * inv_l[:, tq:])
        o_t = o_t * lax.rsqrt(jnp.mean(o_t * o_t, axis=0, keepdims=True) + SUBLN_EPS)
        o_ref[:, cols] = (o_t.T * out_gain[:, cols]).astype(o_ref.dtype)


def _diff_attention(q, k, vt, k_meta, vt_meta, lam_vecs, gain, batch, lambda_init):
    t = q.shape[0]
    seq = t // batch
    nq = seq // ATT_TQ
    assert nq * ATT_TQ == seq and ATT_TQ % ATT_TK == 0 and ATT_TK % CHUNK == 0
    width = N_HEADS * DV
    return pl.pallas_call(
        functools.partial(_diff_attn_kernel, lambda_init=lambda_init),
        grid=(batch, nq),
        in_specs=[_resident(lam_vecs.shape),
                  pl.BlockSpec((ATT_TQ, width), lambda b, i: (b * nq + i, 0)),
                  pl.BlockSpec((seq, width), lambda b, i: (b, 0)),
                  pl.BlockSpec((width, seq), lambda b, i: (0, b)),
                  _resident(k_meta.shape),
                  _resident(vt_meta.shape),
                  _resident((1, width))],
        out_specs=pl.BlockSpec((ATT_TQ, width), lambda b, i: (b * nq + i, 0)),
        out_shape=jax.ShapeDtypeStruct((t, width), BF16),
        scratch_shapes=[pltpu.VMEM((N_HEADS, 2 * ATT_TQ, DV), BF16),
                        pltpu.VMEM((N_HEADS, 1, 2 * ATT_TQ), F32),
                        pltpu.VMEM((N_HEADS, VAUG_ROWS, 2 * ATT_TQ), F32),
                        pltpu.VMEM((ATT_AHEAD, ATT_TK, ATT_TK), F32)],
        compiler_params=_compiler_params(("parallel", "arbitrary")),
        name="diff_attention",
    )(lam_vecs, q, k, vt, k_meta, vt_meta, gain)


def _post_mlp_kernel(h_ref, y_ref, wo_ref, g_ref, wu_ref, wd_ref, fg_ref, *refs, final_norm, ff_block,
                     cast_transposed):
    n_cast = len(cast_transposed)
    out_ref = refs[n_cast]
    _cast_slabs(refs[:n_cast], refs[n_cast + 1:], cast_transposed)
    h1 = h_ref[...] + jnp.dot(y_ref[...], wo_ref[...], preferred_element_type=F32)
    ms = jnp.mean(h1 * h1, axis=-1, keepdims=True)
    hn = (h1 * lax.rsqrt(ms + EPS) * g_ref[...]).astype(BF16)
    acc = h1
    for c in range(D_FF // ff_block):
        blk = slice(c * ff_block, (c + 1) * ff_block)
        u = jnp.dot(hn, wu_ref[:, blk], preferred_element_type=F32)
        a = jnp.square(jnp.maximum(u, 0.0)).astype(BF16)
        acc = acc + jnp.dot(a, wd_ref[blk, :], preferred_element_type=F32)
    if final_norm:
        ms = jnp.mean(acc * acc, axis=-1, keepdims=True)
        acc = acc * lax.rsqrt(ms + EPS) * fg_ref[...]
    out_ref[...] = acc


def _post_mlp(h, y, w_out, gain, w_up, w_down, final_gain, final_norm, tm, name, to_cast=()):
    t = h.shape[0]
    assert t % tm == 0
    cast_in, cast_out, cast_shapes = _cast_specs(to_cast, t // tm)
    return pl.pallas_call(
        functools.partial(_post_mlp_kernel, final_norm=final_norm, ff_block=MLP_FF_BLOCK,
                          cast_transposed=tuple(job["transpose"] for job in to_cast)),
        grid=(t // tm,),
        in_specs=[pl.BlockSpec((tm, D_MODEL), lambda i: (i, 0)),
                  pl.BlockSpec((tm, D_MODEL), lambda i: (i, 0)),
                  _resident((D_MODEL, D_MODEL)),
                  _resident((1, D_MODEL)),
                  _resident((D_MODEL, D_FF)),
                  _resident((D_FF, D_MODEL)),
                  _resident((1, D_MODEL))] + cast_in,
        out_specs=[pl.BlockSpec((tm, D_MODEL), lambda i: (i, 0))] + cast_out,
        out_shape=[jax.ShapeDtypeStruct((t, D_MODEL), F32)] + cast_shapes,
        compiler_params=_compiler_params(("arbitrary",)),
        name=name,
    )(h, y, w_out, gain, w_up, w_down, final_gain, *[job["array"] for job in to_cast])


def kernel(x, meta_tokens, norm_gains, mlstm_w_in, mlstm_b_gate, mlstm_head_gain, mlstm_w_out,
           diff_w_in, diff_lambda, diff_head_gain, diff_w_out, mlp_w_up, mlp_w_down, final_gain):
    batch, seq, _ = x.shape
    t = batch * seq
    h = x.reshape(t, D_MODEL)
    h_meta = jnp.concatenate([jnp.zeros((META_ROWS - N_META, D_MODEL), x.dtype), meta_tokens.astype(x.dtype)],
                             axis=0)

    m_qk = N_HEADS * DQK
    m_v = N_HEADS * DV
    w0 = mlstm_w_in[0]
    offs = (0, m_qk, 2 * m_qk, 2 * m_qk + m_v, 2 * m_qk + 2 * m_v)
    w0_parts = [w0[:, offs[3]:offs[4]],
                jnp.concatenate([w0[:, offs[2]:offs[3]], w0[:, offs[4]:]], axis=1).T,
                w0[:, offs[0]:offs[1]], w0[:, offs[1]:offs[2]]]
    w0_parts = [w.astype(BF16) for w in w0_parts]
    in0 = dict(gain=norm_gains[0, 0][None], weights=w0_parts, scales=(1.0, 1.0, DQK ** -0.5, 1.0),
               transposed=(False, True, False, False), sigmoid=(True, False, False, False),
               dtypes=(BF16, ((m_v, BF16), (GATE_ROWS, F32)), BF16, BF16))
    bias = mlstm_b_gate[0][:, None]
    head_gain0 = mlstm_head_gain[0][None]

    o_gate, vt, gates_t, q, k = _norm_proj(h_meta, tm=META_ROWS, name="mlstm_in_proj_meta", **in0)
    ct_zero = jnp.zeros((N_HEADS, VAUG_ROWS, LANES), F32)
    m_zero = jnp.zeros((N_HEADS, LANES), F32)
    y_meta, ct_meta, m_meta = _mlstm_core(q, k, vt, o_gate, gates_t, bias, head_gain0, ct_zero, m_zero, batch=1,
                                          chunks_per_step=1, group=1, name="mlstm_core_meta")

    o_gate, vt, gates_t, q, k, w_out0, w_up0, w_down0 = _norm_proj(
        h, tm=PROJ_TM, name="mlstm_in_proj",
        to_cast=(_cast_job(mlstm_w_out, 0), _cast_job(mlp_w_up, 0), _cast_job(mlp_w_down, 0)), **in0)
    mlp0 = dict(w_out=w_out0, gain=norm_gains[0, 1][None], w_up=w_up0, w_down=w_down0,
                final_gain=final_gain[None], final_norm=False)
    h_meta, = _post_mlp(h_meta, y_meta, tm=META_ROWS, name="mlstm_out_mlp_meta", **mlp0)
    y, _, _ = _mlstm_core(q, k, vt, o_gate, gates_t, bias, head_gain0, ct_meta[0, 0], m_meta[0, 0], batch=batch,
                          chunks_per_step=MLSTM_CHUNKS_PER_STEP, group=MLSTM_GROUP, name="mlstm_core")
    layer1_jobs = (_cast_job(diff_w_in, 0, 0, 3), _cast_job(diff_w_in, 0, 1, 3),
                   _cast_job(diff_w_in, 0, 2, 3, transpose=True),
                   _cast_job(diff_w_out, 0), _cast_job(mlp_w_up, 1), _cast_job(mlp_w_down, 1))
    h, w_q1, w_k1, w_vt1, w_out1, w_up1, w_down1 = _post_mlp(h, y, tm=MLP_TM, name="mlstm_out_mlp",
                                                             to_cast=layer1_jobs, **mlp0)

    lambda_init = 0.8 - 0.6 * math.exp(-0.3 * 1)
    in1 = dict(gain=norm_gains[1, 0][None], weights=[w_q1, w_k1, w_vt1], scales=(DQK ** -0.5 * LOG2E, 1.0, 1.0),
               transposed=(False, False, True), sigmoid=(False,) * 3, dtypes=(BF16, BF16, BF16))
    _, k_meta, vt_meta = _norm_proj(h_meta, tm=META_ROWS, name="diff_in_proj_meta", **in1)
    k_meta = k_meta[META_ROWS - ATT_META_ROWS:]
    vt_meta = vt_meta[:, META_ROWS - ATT_META_ROWS:]
    q, k, vt = _norm_proj(h, tm=PROJ_TM, name="diff_in_proj", **in1)
    y = _diff_attention(q, k, vt, k_meta, vt_meta, diff_lambda[0], diff_head_gain[0][None], batch, lambda_init)
    out, = _post_mlp(h, y, w_out1, norm_gains[1, 1][None], w_up1, w_down1, final_gain[None], final_norm=True,
                     tm=MLP_TM, name="diff_out_mlp")
    return out.reshape(batch, seq, D_MODEL)
```

```python
import functools
import math

import jax
import jax.numpy as jnp
from jax import lax
from jax.experimental import pallas as pl
from jax.experimental.pallas import tpu as pltpu

D_MODEL = 1024
D_FF = 4 * D_MODEL
N_META = 16
CHUNK = 64
N_HEADS = 8
DQK = 64
DV = 128
EPS = 1e-6
SUBLN_EPS = 1e-5
LOG2E = math.log2(math.e)

LANES = 128
BF16_ROWS = 16
AUG_ROWS = BF16_ROWS
VAUG_ROWS = DV + AUG_ROWS
GATE_ROWS = 2 * N_HEADS

MCHUNK = 256
MLSTM_CHUNKS_PER_STEP = 8
MLSTM_GROUP = 1
MLSTM_AHEAD = 3
META_ROWS = MCHUNK
ATT_META_ROWS = N_META
ATT_META_PAD = ATT_META_ROWS - N_META
ATT_TQ = 1024
ATT_TK = 256
ATT_AHEAD = 5

PROJ_TM = 1024
PROJ_SUB = 512
MLP_TM = 1024
MLP_FF_BLOCK = 512

VMEM_LIMIT = 56 * 1024 * 1024

F32 = jnp.float32
BF16 = jnp.bfloat16
NT_DIMS = (((1,), (1,)), ((), ()))


def _compiler_params(semantics):
    return pltpu.CompilerParams(dimension_semantics=semantics, vmem_limit_bytes=VMEM_LIMIT)


def _resident(shape):
    return pl.BlockSpec(shape, lambda *_: (0,) * len(shape), pipeline_mode=pl.Buffered(1))


def _ones_rows(width):
    return (lax.broadcasted_iota(jnp.int32, (AUG_ROWS, width), 0) == 0).astype(BF16)


def _cast_job(stacked, layer, col_block=0, n_col_blocks=1, transpose=False):
    return dict(array=stacked, layer=layer, col_block=col_block, n_col_blocks=n_col_blocks, transpose=transpose)


def _cast_specs(jobs, steps):
    in_specs, out_specs, out_shapes = [], [], []
    for job in jobs:
        _, n_rows, n_cols = job["array"].shape
        cols = n_cols // job["n_col_blocks"]
        layer, col_block = job["layer"], job["col_block"]
        if job["transpose"]:
            per_slab = steps * LANES // n_rows
            assert per_slab * n_rows == steps * LANES
            in_specs.append(pl.BlockSpec((None, LANES, cols),
                                         lambda i, l=layer, c=col_block, p=per_slab: (l, i // p, c)))
            out_specs.append(pl.BlockSpec((cols, LANES), lambda i, p=per_slab: (0, i // p)))
            out_shapes.append(jax.ShapeDtypeStruct((cols, n_rows), BF16))
        else:
            rows = n_rows // steps
            assert rows * steps == n_rows and rows % BF16_ROWS == 0
            in_specs.append(pl.BlockSpec((None, rows, cols), lambda i, l=layer, c=col_block: (l, i, c)))
            out_specs.append(pl.BlockSpec((rows, cols), lambda i: (i, 0)))
            out_shapes.append(jax.ShapeDtypeStruct((n_rows, cols), BF16))
    return in_specs, out_specs, out_shapes


def _cast_slabs(src_refs, dst_refs, transposed):
    for src, dst, tr in zip(src_refs, dst_refs, transposed):
        x = src[...]
        dst[...] = (x.T if tr else x).astype(dst.dtype)


def _norm_proj_kernel(x_ref, g_ref, *refs, scales, transposed, sigmoid, pieces, cast_transposed):
    n_w, n_cast, n_out = len(scales), len(cast_transposed), sum(len(p) for p in pieces)
    w_refs, out_refs = refs[:n_w], list(refs[n_w + n_cast:n_w + n_cast + n_out])
    _cast_slabs(refs[n_w:n_w + n_cast], refs[n_w + n_cast + n_out:], cast_transposed)
    tm = x_ref.shape[0]
    sub = min(PROJ_SUB, tm)
    for r0 in range(0, tm, sub):
        rows = slice(r0, r0 + sub)
        x = x_ref[rows, :]
        ms = jnp.mean(x * x, axis=-1, keepdims=True)
        xn = (x * lax.rsqrt(ms + EPS) * g_ref[...]).astype(BF16)
        outs = iter(out_refs)
        for w_ref, sc, tr, sg, sizes in zip(w_refs, scales, transposed, sigmoid, pieces):
            if tr:
                r = lax.dot_general(w_ref[...], xn, NT_DIMS, preferred_element_type=F32)
            else:
                r = jnp.dot(xn, w_ref[...], preferred_element_type=F32)
            if sc != 1.0:
                r = r * sc
            if sg:
                r = 0.5 * jnp.tanh(0.5 * r) + 0.5
            if tr:
                off = 0
                for n in sizes:
                    o_ref = next(outs)
                    o_ref[:, rows] = r[off:off + n].astype(o_ref.dtype)
                    off += n
            else:
                o_ref = next(outs)
                o_ref[rows, :] = r.astype(o_ref.dtype)


def _norm_proj(h, gain, weights, scales, transposed, sigmoid, dtypes, tm, name, to_cast=()):
    t = h.shape[0]
    assert t % tm == 0 and tm % min(PROJ_SUB, tm) == 0
    cast_in, cast_out, cast_shapes = _cast_specs(to_cast, t // tm)
    w_specs, out_specs, out_shapes, pieces = [], [], [], []
    for w, tr, dt in zip(weights, transposed, dtypes):
        w_specs.append(_resident(w.shape))
        if tr:
            parts = dt if isinstance(dt, tuple) else ((w.shape[0], dt),)
            assert sum(n for n, _ in parts) == w.shape[0]
            pieces.append(tuple(n for n, _ in parts))
            for n, part_dt in parts:
                out_specs.append(pl.BlockSpec((n, tm), lambda i: (0, i)))
                out_shapes.append(jax.ShapeDtypeStruct((n, t), part_dt))
        else:
            n = w.shape[1]
            pieces.append((n,))
            out_specs.append(pl.BlockSpec((tm, n), lambda i: (i, 0)))
            out_shapes.append(jax.ShapeDtypeStruct((t, n), dt))
    return pl.pallas_call(
        functools.partial(_norm_proj_kernel, scales=scales, transposed=transposed, sigmoid=sigmoid,
                          pieces=tuple(pieces), cast_transposed=tuple(job["transpose"] for job in to_cast)),
        grid=(t // tm,),
        in_specs=[pl.BlockSpec((tm, D_MODEL), lambda i: (i, 0)), _resident((1, D_MODEL))] + w_specs + cast_in,
        out_specs=out_specs + cast_out,
        out_shape=out_shapes + cast_shapes,
        compiler_params=_compiler_params(("arbitrary",)),
        name=name,
    )(h, gain, *weights, *[job["array"] for job in to_cast])


def _mlstm_kernel(*refs, chunks_per_step, group):
    q_ref, k_ref, o_ref = refs[0:3]
    vt_refs = refs[3:3 + group]
    gt_refs = refs[3 + group:3 + 2 * group]
    (bias_ref, gain_ref, ct0_ref, m0_ref, y_ref, ct_out_ref, m_out_ref,
     ct_ref, m_ref, b_ref, r_ref, cm_ref, rt_ref) = refs[3 + 2 * group:]
    t = MCHUNK

    @pl.when(pl.program_id(1) == 0)
    def _():
        for g in range(group):
            ct_ref[g] = ct0_ref[...]
            m_ref[g] = m0_ref[...]

    neg_inf = jnp.float32(-jnp.inf)
    causal = (lax.broadcasted_iota(jnp.int32, (t, t), 0)
              <= lax.broadcasted_iota(jnp.int32, (t, t), 1))
    upper = causal.astype(BF16)
    lane_t = lax.broadcasted_iota(jnp.int32, (N_HEADS, t), 1)
    bias = bias_ref[...]

    for g in range(group):
        for c in range(chunks_per_step):
            gates = gt_refs[g][:, c * t:(c + 1) * t] + bias
            ig = gates[0:N_HEADS]
            lf = jax.nn.log_sigmoid(gates[N_HEADS:])
            hi = lf.astype(BF16).astype(F32)
            mid = (lf - hi).astype(BF16).astype(F32)
            lo = (lf - hi - mid).astype(BF16).astype(F32)
            parts = jnp.dot(jnp.concatenate([hi, mid, lo], axis=0).astype(BF16), upper,
                            preferred_element_type=F32)
            b = parts[0:N_HEADS] + parts[N_HEADS:2 * N_HEADS] + parts[2 * N_HEADS:]
            r = ig - b
            cm = r
            shift = 1
            while shift < t:
                cm = jnp.maximum(cm, jnp.where(lane_t >= shift, pltpu.roll(cm, shift, 1), neg_inf))
                shift *= 2
            b_ref[g, c] = b
            r_ref[g, c] = r
            cm_ref[g, c] = cm
            rt_ref[g, c] = (r * LOG2E).T

    low = lax.broadcasted_iota(jnp.int32, (t, LANES), 1) < DQK
    low_state = lax.broadcasted_iota(jnp.int32, (VAUG_ROWS, LANES), 1) < DQK
    ones_rows = _ones_rows(t)
    half = t // 2
    tri = causal[:half, :half]

    def chunk_body(c, carry):
        rows = pl.ds(pl.multiple_of(c * t, t), t)
        row_terms = []
        for g in range(group):
            b, r, cm = b_ref[g, c], r_ref[g, c], cm_ref[g, c]
            m_prev = m_ref[g, :, 0:1]
            u = jnp.maximum(m_prev, cm)
            b_end = b[:, t - 1:t]
            a = b_end + r
            m_new = jnp.maximum(b_end + m_prev, jnp.max(a, axis=1, keepdims=True))
            row_terms.append(dict(
                inter=jnp.exp(m_prev - u), e_row=jnp.exp(-(b + u)), u2=u * LOG2E,
                decay=jnp.exp(b_end + m_prev - m_new), w=jnp.exp(a - m_new).astype(BF16), rt=rt_ref[g, c]))
            m_ref[g] = jnp.broadcast_to(m_new, m_ref.shape[1:])

        def qk_tiles(g, h):
            pair = slice(h // 2 * LANES, (h // 2 + 1) * LANES)
            return q_ref[g, rows, pair], k_ref[g, rows, pair]

        def head_scores(g, h):
            q2, k2 = qk_tiles(g, h)
            q_own = jnp.where(low if h % 2 == 0 else jnp.logical_not(low), q2, jnp.zeros_like(q2))
            return lax.dot_general(k2, q_own, NT_DIMS, preferred_element_type=F32)

        seq = [(g, h) for h in range(N_HEADS) for g in range(group)]
        pending = [head_scores(*u) for u in seq[:MLSTM_AHEAD]]
        for n, (g, h) in enumerate(seq):
            st_h = pending.pop(0)
            terms = row_terms[g]
            q2, k2 = qk_tiles(g, h)
            ct = ct_ref[g, h]
            vaug = jnp.concatenate([vt_refs[g][h * DV:(h + 1) * DV, rows], ones_rows], axis=0)
            inter_num = lax.dot_general(ct.astype(BF16), q2, NT_DIMS, preferred_element_type=F32)
            upd = jnp.dot(vaug * terms["w"][h:h + 1, :], k2, preferred_element_type=F32)
            own = low_state if h % 2 == 0 else jnp.logical_not(low_state)
            ct_ref[g, h] = terms["decay"][h:h + 1, :] * ct + jnp.where(own, upd, 0.0)
            if n + MLSTM_AHEAD < len(seq):
                pending.append(head_scores(*seq[n + MLSTM_AHEAD]))
            cols = slice(h * DV, (h + 1) * DV)
            r_col, u_row = terms["rt"][:, h:h + 1], terms["u2"][h:h + 1, :]
            d00 = jnp.exp2(jnp.where(tri, r_col[:half] - u_row[:, :half], neg_inf))
            d01 = jnp.exp2(r_col[:half] - u_row[:, half:])
            d11 = jnp.exp2(jnp.where(tri, r_col[half:] - u_row[:, half:], neg_inf))
            s_h = jnp.concatenate(
                [jnp.concatenate([st_h[:half, :half] * d00, st_h[:half, half:] * d01], axis=1),
                 jnp.concatenate([jnp.zeros((half, half), F32), st_h[half:, half:] * d11], axis=1)],
                axis=0).astype(BF16)
            num = (terms["inter"][h:h + 1, :] * inter_num
                   + jnp.dot(vaug, s_h, preferred_element_type=F32))
            inv = 1.0 / jnp.maximum(jnp.abs(num[DV:DV + 1, :]), terms["e_row"][h:h + 1, :])
            hh = num[0:DV, :] * inv
            hh = hh * lax.rsqrt(jnp.mean(hh * hh, axis=0, keepdims=True) + EPS)
            y_ref[g, rows, cols] = (hh.T * gain_ref[:, cols]).astype(BF16) * o_ref[g, rows, cols]
        return carry

    lax.fori_loop(0, chunks_per_step, chunk_body, 0)
    ct_out_ref[...] = ct_ref[...]
    m_out_ref[...] = m_ref[...]


def _mlstm_core(q, k, vt, o, gates_t, bias, gain, ct0, m0, batch, chunks_per_step, group, name):
    tokens = q.shape[0]
    rows = chunks_per_step * MCHUNK
    per_group = batch // group
    steps = tokens // batch // rows
    assert per_group * group == batch and batch * steps * rows == tokens
    ct_shape = (N_HEADS, VAUG_ROWS, LANES)
    m_shape = (N_HEADS, LANES)
    gate_shape = (group, chunks_per_step, N_HEADS, MCHUNK)

    def tok(width):
        return pl.BlockSpec((group, rows, width), lambda b, j: (0, b * steps + j, 0))

    def tok_t(height, g):
        return pl.BlockSpec((height, rows), lambda b, j, g=g: (0, (g * per_group + b) * steps + j))

    def grouped(x):
        return x.reshape(group, tokens // group, x.shape[1])

    y, ct, m = pl.pallas_call(
        functools.partial(_mlstm_kernel, chunks_per_step=chunks_per_step, group=group),
        grid=(per_group, steps),
        in_specs=([tok(N_HEADS * DQK), tok(N_HEADS * DQK), tok(N_HEADS * DV)]
                  + [tok_t(N_HEADS * DV, g) for g in range(group)]
                  + [tok_t(GATE_ROWS, g) for g in range(group)]
                  + [_resident((2 * N_HEADS, 1)), _resident((1, N_HEADS * DV)),
                     _resident(ct_shape), _resident(m_shape)]),
        out_specs=[tok(N_HEADS * DV),
                   pl.BlockSpec((group, None) + ct_shape, lambda b, j: (0, b, 0, 0, 0)),
                   pl.BlockSpec((group, None) + m_shape, lambda b, j: (0, b, 0, 0))],
        out_shape=[jax.ShapeDtypeStruct((group, tokens // group, N_HEADS * DV), BF16),
                   jax.ShapeDtypeStruct((group, per_group) + ct_shape, F32),
                   jax.ShapeDtypeStruct((group, per_group) + m_shape, F32)],
        scratch_shapes=[pltpu.VMEM((group,) + ct_shape, F32), pltpu.VMEM((group,) + m_shape, F32),
                        pltpu.VMEM(gate_shape, F32), pltpu.VMEM(gate_shape, F32), pltpu.VMEM(gate_shape, F32),
                        pltpu.VMEM((group, chunks_per_step, MCHUNK, N_HEADS), F32)],
        compiler_params=_compiler_params(("parallel", "arbitrary")),
        name=name,
    )(grouped(q), grouped(k), grouped(o), *([vt] * group), *([gates_t] * group), bias, gain, ct0, m0)
    return y.reshape(tokens, N_HEADS * DV), ct, m


def _diff_attn_kernel(lam_ref, q_ref, k_ref, vt_ref, km_ref, vtm_ref, gain_ref, o_ref,
                      qs_ref, m_ref, acc_ref, sc_ref, *, lambda_init):
    qi = pl.program_id(1)
    tq, tk = ATT_TQ, ATT_TK
    blocks = tq // tk
    neg_inf = jnp.float32(-jnp.inf)

    low = lax.broadcasted_iota(jnp.int32, (tq, LANES), 1) < DQK
    for h in range(N_HEADS):
        q2 = q_ref[:, h * DV:(h + 1) * DV]
        zero = jnp.zeros_like(q2)
        qs_ref[h] = jnp.concatenate([jnp.where(low, q2, zero), jnp.where(low, zero, q2)], axis=0)

    all_units = [(h, br, blk) for h in range(N_HEADS) for br in range(2) for blk in range(blocks)]

    def unit_cols(br, blk):
        return pl.ds((br * blocks + blk) * tk, tk)

    def run(items, primed, next_items):
        stream = items + next_items

        def scores(item):
            tile, (h, br, blk) = item
            s = lax.dot_general(tile["k"](h), qs_ref[h, unit_cols(br, blk), :], NT_DIMS,
                                preferred_element_type=F32)
            bias = tile["bias"](blk)
            return s if bias is None else s + bias

        pending = [sc_ref[n] if primed else scores(stream[n]) for n in range(ATT_AHEAD)]
        for n, (tile, (h, br, blk)) in enumerate(items):
            s = pending.pop(0)
            ahead = n + ATT_AHEAD
            if ahead < len(items):
                pending.append(scores(stream[ahead]))
            elif ahead < len(stream):
                sc_ref[ahead - len(items)] = scores(stream[ahead])
            cols = unit_cols(br, blk)
            v_aug = jnp.concatenate([tile["vt"](h), tile["ones"]], axis=0)
            s_max = jnp.max(s, axis=0, keepdims=True)
            if tile["first"]:
                m_new = s_max
                acc_ref[h, :, cols] = jnp.dot(v_aug, jnp.exp2(s - m_new).astype(BF16), preferred_element_type=F32)
            else:
                m_old = m_ref[h, :, cols]
                m_new = jnp.maximum(m_old, s_max)
                alpha = jnp.exp2(m_old - m_new)
                p = jnp.exp2(s - m_new).astype(BF16)
                acc_ref[h, :, cols] = alpha * acc_ref[h, :, cols] + jnp.dot(v_aug, p, preferred_element_type=F32)
            m_ref[h, :, cols] = m_new
        assert not pending and len(items) >= ATT_AHEAD

    ones_tile = _ones_rows(tk)

    def frame_tile(start, bias_of):
        return dict(k=lambda h: k_ref[pl.ds(start, tk), h * DV:(h + 1) * DV],
                    vt=lambda h: vt_ref[h * DV:(h + 1) * DV, pl.ds(start, tk)],
                    ones=ones_tile, bias=bias_of, first=False)

    def full_items(start, units):
        tile = frame_tile(start, lambda blk: None)
        return [(tile, u) for u in units]

    chunk_shift = CHUNK.bit_length() - 1
    k_chunk = lax.shift_right_logical(lax.broadcasted_iota(jnp.int32, (tk, tk), 0), chunk_shift)
    q_chunk = lax.shift_right_logical(lax.broadcasted_iota(jnp.int32, (tk, tk), 1), chunk_shift)
    diag_bias = jnp.where(k_chunk <= q_chunk, 0.0, neg_inf)
    meta_row = lax.broadcasted_iota(jnp.int32, (ATT_META_ROWS, tk), 0)
    meta_bias = jnp.where(meta_row >= ATT_META_PAD, 0.0, neg_inf)
    first_bias = [jnp.concatenate([diag_bias if blk == 0 else jnp.zeros_like(diag_bias), meta_bias], axis=0)
                  for blk in range(min(blocks, 2))]
    start0 = pl.multiple_of(qi * tq, tk)
    first_tile = dict(
        k=lambda h: jnp.concatenate([k_ref[pl.ds(start0, tk), h * DV:(h + 1) * DV],
                                     km_ref[:, h * DV:(h + 1) * DV]], axis=0),
        vt=lambda h: jnp.concatenate([vt_ref[h * DV:(h + 1) * DV, pl.ds(start0, tk)],
                                      vtm_ref[h * DV:(h + 1) * DV, :]], axis=1),
        ones=_ones_rows(tk + ATT_META_ROWS), bias=lambda blk: first_bias[min(blk, 1)], first=True)
    items = [(first_tile, u) for u in all_units]
    for d in range(1, blocks):
        tile = frame_tile(pl.multiple_of(qi * tq + d * tk, tk), lambda blk, d=d: diag_bias if blk == d else None)
        items += [(tile, u) for u in all_units if u[2] >= d]
    head_units = all_units[:ATT_AHEAD]
    run(items, primed=False, next_items=full_items(0, head_units))

    n_full = qi * blocks
    last_tile = k_ref.shape[0] // tk - 1

    def full_tile(j, carry):
        nxt = pl.multiple_of(jnp.minimum(j + 1, last_tile) * tk, tk)
        run(full_items(pl.multiple_of(j * tk, tk), all_units), primed=True,
            next_items=full_items(nxt, head_units))
        return carry

    lax.fori_loop(0, n_full, full_tile, 0)

    lv = lam_ref[...]
    lam = (jnp.exp(jnp.sum(lv[0:1] * lv[1:2], axis=-1, keepdims=True))
           - jnp.exp(jnp.sum(lv[2:3] * lv[3:4], axis=-1, keepdims=True)) + lambda_init)
    out_gain = gain_ref[...] * (1.0 - lambda_init)
    for h in range(N_HEADS):
        cols = slice(h * DV, (h + 1) * DV)
        acc = acc_ref[h]
        inv_l = 1.0 / acc[DV:DV + 1, :]
        o_t = acc[0:DV, :tq] * inv_l[:, :tq] - lam * (acc[0:DV, tq:] * inv_l[:, tq:])
        o_t = o_t * lax.rsqrt(jnp.mean(o_t * o_t, axis=0, keepdims=True) + SUBLN_EPS)
        o_ref[:, cols] = (o_t.T * out_gain[:, cols]).astype(o_ref.dtype)


def _diff_attention(q, k, vt, k_meta, vt_meta, lam_vecs, gain, batch, lambda_init):
    t = q.shape[0]
    seq = t // batch
    nq = seq // ATT_TQ
    assert nq * ATT_TQ == seq and ATT_TQ % ATT_TK == 0 and ATT_TK % CHUNK == 0
    width = N_HEADS * DV
    return pl.pallas_call(
        functools.partial(_diff_attn_kernel, lambda_init=lambda_init),
        grid=(batch, nq),
        in_specs=[_resident(lam_vecs.shape),
                  pl.BlockSpec((ATT_TQ, width), lambda b, i: (b * nq + i, 0)),
                  pl.BlockSpec((seq, width), lambda b, i: (b, 0)),
                  pl.BlockSpec((width, seq), lambda b, i: (0, b)),
                  _resident(k_meta.shape),
                  _resident(vt_meta.shape),
                  _resident((1, width))],
        out_specs=pl.BlockSpec((ATT_TQ, width), lambda b, i: (b * nq + i, 0)),
        out_shape=jax.ShapeDtypeStruct((t, width), BF16),
        scratch_shapes=[pltpu.VMEM((N_HEADS, 2 * ATT_TQ, DV), BF16),
                        pltpu.VMEM((N_HEADS, 1, 2 * ATT_TQ), F32),
                        pltpu.VMEM((N_HEADS, VAUG_ROWS, 2 * ATT_TQ), F32),
                        pltpu.VMEM((ATT_AHEAD, ATT_TK, ATT_TK), F32)],
        compiler_params=_compiler_params(("parallel", "arbitrary")),
        name="diff_attention",
    )(lam_vecs, q, k, vt, k_meta, vt_meta, gain)


def _post_mlp_kernel(h_ref, y_ref, wo_ref, g_ref, wu_ref, wd_ref, fg_ref, *refs, final_norm, ff_block,
                     cast_transposed):
    n_cast = len(cast_transposed)
    out_ref = refs[n_cast]
    _cast_slabs(refs[:n_cast], refs[n_cast + 1:], cast_transposed)
    h1 = h_ref[...] + jnp.dot(y_ref[...], wo_ref[...], preferred_element_type=F32)
    ms = jnp.mean(h1 * h1, axis=-1, keepdims=True)
    hn = (h1 * lax.rsqrt(ms + EPS) * g_ref[...]).astype(BF16)
    acc = h1
    for c in range(D_FF // ff_block):
        blk = slice(c * ff_block, (c + 1) * ff_block)
        u = jnp.dot(hn, wu_ref[:, blk], preferred_element_type=F32)
        a = jnp.square(jnp.maximum(u, 0.0)).astype(BF16)
        acc = acc + jnp.dot(a, wd_ref[blk, :], preferred_element_type=F32)
    if final_norm:
        ms = jnp.mean(acc * acc, axis=-1, keepdims=True)
        acc = acc * lax.rsqrt(ms + EPS) * fg_ref[...]
    out_ref[...] = acc


def _post_mlp(h, y, w_out, gain, w_up, w_down, final_gain, final_norm, tm, name, to_cast=()):
    t = h.shape[0]
    assert t % tm == 0
    cast_in, cast_out, cast_shapes = _cast_specs(to_cast, t // tm)
    return pl.pallas_call(
        functools.partial(_post_mlp_kernel, final_norm=final_norm, ff_block=MLP_FF_BLOCK,
                          cast_transposed=tuple(job["transpose"] for job in to_cast)),
        grid=(t // tm,),
        in_specs=[pl.BlockSpec((tm, D_MODEL), lambda i: (i, 0)),
                  pl.BlockSpec((tm, D_MODEL), lambda i: (i, 0)),
                  _resident((D_MODEL, D_MODEL)),
                  _resident((1, D_MODEL)),
                  _resident((D_MODEL, D_FF)),
                  _resident((D_FF, D_MODEL)),
                  _resident((1, D_MODEL))] + cast_in,
        out_specs=[pl.BlockSpec((tm, D_MODEL), lambda i: (i, 0))] + cast_out,
        out_shape=[jax.ShapeDtypeStruct((t, D_MODEL), F32)] + cast_shapes,
        compiler_params=_compiler_params(("arbitrary",)),
        name=name,
    )(h, y, w_out, gain, w_up, w_down, final_gain, *[job["array"] for job in to_cast])


def kernel(x, meta_tokens, norm_gains, mlstm_w_in, mlstm_b_gate, mlstm_head_gain, mlstm_w_out,
           diff_w_in, diff_lambda, diff_head_gain, diff_w_out, mlp_w_up, mlp_w_down, final_gain):
    batch, seq, _ = x.shape
    t = batch * seq
    h = x.reshape(t, D_MODEL)
    h_meta = jnp.concatenate([jnp.zeros((META_ROWS - N_META, D_MODEL), x.dtype), meta_tokens.astype(x.dtype)],
                             axis=0)

    m_qk = N_HEADS * DQK
    m_v = N_HEADS * DV
    w0 = mlstm_w_in[0]
    offs = (0, m_qk, 2 * m_qk, 2 * m_qk + m_v, 2 * m_qk + 2 * m_v)
    w0_parts = [w0[:, offs[3]:offs[4]],
                jnp.concatenate([w0[:, offs[2]:offs[3]], w0[:, offs[4]:]], axis=1).T,
                w0[:, offs[0]:offs[1]], w0[:, offs[1]:offs[2]]]
    w0_parts = [w.astype(BF16) for w in w0_parts]
    in0 = dict(gain=norm_gains[0, 0][None], weights=w0_parts, scales=(1.0, 1.0, DQK ** -0.5, 1.0),
               transposed=(False, True, False, False), sigmoid=(True, False, False, False),
               dtypes=(BF16, ((m_v, BF16), (GATE_ROWS, F32)), BF16, BF16))
    bias = mlstm_b_gate[0][:, None]
    head_gain0 = mlstm_head_gain[0][None]

    o_gate, vt, gates_t, q, k = _norm_proj(h_meta, tm=META_ROWS, name="mlstm_in_proj_meta", **in0)
    ct_zero = jnp.zeros((N_HEADS, VAUG_ROWS, LANES), F32)
    m_zero = jnp.zeros((N_HEADS, LANES), F32)
    y_meta, ct_meta, m_meta = _mlstm_core(q, k, vt, o_gate, gates_t, bias, head_gain0, ct_zero, m_zero, batch=1,
                                          chunks_per_step=1, group=1, name="mlstm_core_meta")

    o_gate, vt, gates_t, q, k, w_out0, w_up0, w_down0 = _norm_proj(
        h, tm=PROJ_TM, name="mlstm_in_proj",
        to_cast=(_cast_job(mlstm_w_out, 0), _cast_job(mlp_w_up, 0), _cast_job(mlp_w_down, 0)), **in0)
    mlp0 = dict(w_out=w_out0, gain=norm_gains[0, 1][None], w_up=w_up0, w_down=w_down0,
                final_gain=final_gain[None], final_norm=False)
    h_meta, = _post_mlp(h_meta, y_meta, tm=META_ROWS, name="mlstm_out_mlp_meta", **mlp0)
    y, _, _ = _mlstm_core(q, k, vt, o_gate, gates_t, bias, head_gain0, ct_meta[0, 0], m_meta[0, 0], batch=batch,
                          chunks_per_step=MLSTM_CHUNKS_PER_STEP, group=MLSTM_GROUP, name="mlstm_core")
    layer1_jobs = (_cast_job(diff_w_in, 0, 0, 3), _cast_job(diff_w_in, 0, 1, 3),
                   _cast_job(diff_w_in, 0, 2, 3, transpose=True),
                   _cast_job(diff_w_out, 0), _cast_job(mlp_w_up, 1), _cast_job(mlp_w_down, 1))
    h, w_q1, w_k1, w_vt1, w_out1, w_up1, w_down1 = _post_mlp(h, y, tm=MLP_TM, name="mlstm_out_mlp",
                                                             to_cast=layer1_jobs, **mlp0)

    lambda_init = 0.8 - 0.6 * math.exp(-0.3 * 1)
    in1 = dict(gain=norm_gains[1, 0][None], weights=[w_q1, w_k1, w_vt1], scales=(DQK ** -0.5 * LOG2E, 1.0, 1.0),
               transposed=(False, False, True), sigmoid=(False,) * 3, dtypes=(BF16, BF16, BF16))
    _, k_meta, vt_meta = _norm_proj(h_meta, tm=META_ROWS, name="diff_in_proj_meta", **in1)
    k_meta = k_meta[META_ROWS - ATT_META_ROWS:]
    vt_meta = vt_meta[:, META_ROWS - ATT_META_ROWS:]
    q, k, vt = _norm_proj(h, tm=PROJ_TM, name="diff_in_proj", **in1)
    y = _diff_attention(q, k, vt, k_meta, vt_meta, diff_lambda[0], diff_head_gain[0][None], batch, lambda_init)
    out, = _post_mlp(h, y, w_out1, norm_gains[1, 1][None], w_up1, w_down1, final_gain[None], final_norm=True,
                     tm=MLP_TM, name="diff_out_mlp")
    return out.reshape(batch, seq, D_MODEL)
```

```python
import functools
import math

import jax
import jax.numpy as jnp
from jax import lax
from jax.experimental import pallas as pl
from jax.experimental.pallas import tpu as pltpu

D_MODEL = 1024
D_FF = 4 * D_MODEL
N_META = 16
CHUNK = 64
N_HEADS = 8
DQK = 64
DV = 128
EPS = 1e-6
SUBLN_EPS = 1e-5
LOG2E = math.log2(math.e)

LANES = 128
BF16_ROWS = 16
AUG_ROWS = BF16_ROWS
VAUG_ROWS = DV + AUG_ROWS
GATE_ROWS = 2 * N_HEADS

MCHUNK = 256
MLSTM_CHUNKS_PER_STEP = 8
MLSTM_GROUP = 1
MLSTM_AHEAD = 3
META_ROWS = MCHUNK
ATT_TQ = 1024
ATT_TK = 256
ATT_AHEAD = 5

PROJ_TM = 1024
PROJ_SUB = 512
MLP_TM = 1024
MLP_FF_BLOCK = 512

VMEM_LIMIT = 56 * 1024 * 1024

F32 = jnp.float32
BF16 = jnp.bfloat16
NT_DIMS = (((1,), (1,)), ((), ()))


def _compiler_params(semantics):
    return pltpu.CompilerParams(dimension_semantics=semantics, vmem_limit_bytes=VMEM_LIMIT)


def _resident(shape):
    return pl.BlockSpec(shape, lambda *_: (0,) * len(shape), pipeline_mode=pl.Buffered(1))


def _ones_rows(width):
    return (lax.broadcasted_iota(jnp.int32, (AUG_ROWS, width), 0) == 0).astype(BF16)


def _cast_job(stacked, layer, col_block=0, n_col_blocks=1, transpose=False):
    return dict(array=stacked, layer=layer, col_block=col_block, n_col_blocks=n_col_blocks, transpose=transpose)


def _cast_specs(jobs, steps):
    in_specs, out_specs, out_shapes = [], [], []
    for job in jobs:
        _, n_rows, n_cols = job["array"].shape
        cols = n_cols // job["n_col_blocks"]
        layer, col_block = job["layer"], job["col_block"]
        if job["transpose"]:
            per_slab = steps * LANES // n_rows
            assert per_slab * n_rows == steps * LANES
            in_specs.append(pl.BlockSpec((None, LANES, cols),
                                         lambda i, l=layer, c=col_block, p=per_slab: (l, i // p, c)))
            out_specs.append(pl.BlockSpec((cols, LANES), lambda i, p=per_slab: (0, i // p)))
            out_shapes.append(jax.ShapeDtypeStruct((cols, n_rows), BF16))
        else:
            rows = n_rows // steps
            assert rows * steps == n_rows and rows % BF16_ROWS == 0
            in_specs.append(pl.BlockSpec((None, rows, cols), lambda i, l=layer, c=col_block: (l, i, c)))
            out_specs.append(pl.BlockSpec((rows, cols), lambda i: (i, 0)))
            out_shapes.append(jax.ShapeDtypeStruct((n_rows, cols), BF16))
    return in_specs, out_specs, out_shapes


def _cast_slabs(src_refs, dst_refs, transposed):
    for src, dst, tr in zip(src_refs, dst_refs, transposed):
        x = src[...]
        dst[...] = (x.T if tr else x).astype(dst.dtype)


def _norm_proj_kernel(x_ref, g_ref, *refs, scales, transposed, sigmoid, pieces, cast_transposed):
    n_w, n_cast, n_out = len(scales), len(cast_transposed), sum(len(p) for p in pieces)
    w_refs, out_refs = refs[:n_w], list(refs[n_w + n_cast:n_w + n_cast + n_out])
    _cast_slabs(refs[n_w:n_w + n_cast], refs[n_w + n_cast + n_out:], cast_transposed)
    tm = x_ref.shape[0]
    sub = min(PROJ_SUB, tm)
    for r0 in range(0, tm, sub):
        rows = slice(r0, r0 + sub)
        x = x_ref[rows, :]
        ms = jnp.mean(x * x, axis=-1, keepdims=True)
        xn = (x * lax.rsqrt(ms + EPS) * g_ref[...]).astype(BF16)
        outs = iter(out_refs)
        for w_ref, sc, tr, sg, sizes in zip(w_refs, scales, transposed, sigmoid, pieces):
            if tr:
                r = lax.dot_general(w_ref[...], xn, NT_DIMS, preferred_element_type=F32)
            else:
                r = jnp.dot(xn, w_ref[...], preferred_element_type=F32)
            if sc != 1.0:
                r = r * sc
            if sg:
                r = 0.5 * jnp.tanh(0.5 * r) + 0.5
            if tr:
                off = 0
                for n in sizes:
                    o_ref = next(outs)
                    o_ref[:, rows] = r[off:off + n].astype(o_ref.dtype)
                    off += n
            else:
                o_ref = next(outs)
                o_ref[rows, :] = r.astype(o_ref.dtype)


def _norm_proj(h, gain, weights, scales, transposed, sigmoid, dtypes, tm, name, to_cast=()):
    t = h.shape[0]
    assert t % tm == 0 and tm % min(PROJ_SUB, tm) == 0
    cast_in, cast_out, cast_shapes = _cast_specs(to_cast, t // tm)
    w_specs, out_specs, out_shapes, pieces = [], [], [], []
    for w, tr, dt in zip(weights, transposed, dtypes):
        w_specs.append(_resident(w.shape))
        if tr:
            parts = dt if isinstance(dt, tuple) else ((w.shape[0], dt),)
            assert sum(n for n, _ in parts) == w.shape[0]
            pieces.append(tuple(n for n, _ in parts))
            for n, part_dt in parts:
                out_specs.append(pl.BlockSpec((n, tm), lambda i: (0, i)))
                out_shapes.append(jax.ShapeDtypeStruct((n, t), part_dt))
        else:
            n = w.shape[1]
            pieces.append((n,))
            out_specs.append(pl.BlockSpec((tm, n), lambda i: (i, 0)))
            out_shapes.append(jax.ShapeDtypeStruct((t, n), dt))
    return pl.pallas_call(
        functools.partial(_norm_proj_kernel, scales=scales, transposed=transposed, sigmoid=sigmoid,
                          pieces=tuple(pieces), cast_transposed=tuple(job["transpose"] for job in to_cast)),
        grid=(t // tm,),
        in_specs=[pl.BlockSpec((tm, D_MODEL), lambda i: (i, 0)), _resident((1, D_MODEL))] + w_specs + cast_in,
        out_specs=out_specs + cast_out,
        out_shape=out_shapes + cast_shapes,
        compiler_params=_compiler_params(("arbitrary",)),
        name=name,
    )(h, gain, *weights, *[job["array"] for job in to_cast])


def _mlstm_kernel(*refs, chunks_per_step, group):
    q_ref, k_ref, o_ref = refs[0:3]
    vt_refs = refs[3:3 + group]
    gt_refs = refs[3 + group:3 + 2 * group]
    (bias_ref, gain_ref, ct0_ref, m0_ref, y_ref, ct_out_ref, m_out_ref,
     ct_ref, m_ref, b_ref, r_ref, cm_ref, rt_ref) = refs[3 + 2 * group:]
    t = MCHUNK

    @pl.when(pl.program_id(1) == 0)
    def _():
        for g in range(group):
            ct_ref[g] = ct0_ref[...]
            m_ref[g] = m0_ref[...]

    neg_inf = jnp.float32(-jnp.inf)
    causal = (lax.broadcasted_iota(jnp.int32, (t, t), 0)
              <= lax.broadcasted_iota(jnp.int32, (t, t), 1))
    upper = causal.astype(BF16)
    lane_t = lax.broadcasted_iota(jnp.int32, (N_HEADS, t), 1)
    bias = bias_ref[...]

    for g in range(group):
        for c in range(chunks_per_step):
            gates = gt_refs[g][:, c * t:(c + 1) * t] + bias
            ig = gates[0:N_HEADS]
            lf = jax.nn.log_sigmoid(gates[N_HEADS:])
            hi = lf.astype(BF16).astype(F32)
            mid = (lf - hi).astype(BF16).astype(F32)
            lo = (lf - hi - mid).astype(BF16).astype(F32)
            parts = jnp.dot(jnp.concatenate([hi, mid, lo], axis=0).astype(BF16), upper,
                            preferred_element_type=F32)
            b = parts[0:N_HEADS] + parts[N_HEADS:2 * N_HEADS] + parts[2 * N_HEADS:]
            r = ig - b
            cm = r
            shift = 1
            while shift < t:
                cm = jnp.maximum(cm, jnp.where(lane_t >= shift, pltpu.roll(cm, shift, 1), neg_inf))
                shift *= 2
            b_ref[g, c] = b
            r_ref[g, c] = r
            cm_ref[g, c] = cm
            rt_ref[g, c] = (r * LOG2E).T

    low = lax.broadcasted_iota(jnp.int32, (t, LANES), 1) < DQK
    low_state = lax.broadcasted_iota(jnp.int32, (VAUG_ROWS, LANES), 1) < DQK
    ones_rows = _ones_rows(t)
    half = t // 2
    tri = causal[:half, :half]

    def chunk_body(c, carry):
        rows = pl.ds(pl.multiple_of(c * t, t), t)
        row_terms = []
        for g in range(group):
            b, r, cm = b_ref[g, c], r_ref[g, c], cm_ref[g, c]
            m_prev = m_ref[g, :, 0:1]
            u = jnp.maximum(m_prev, cm)
            b_end = b[:, t - 1:t]
            a = b_end + r
            m_new = jnp.maximum(b_end + m_prev, jnp.max(a, axis=1, keepdims=True))
            row_terms.append(dict(
                inter=jnp.exp(m_prev - u), e_row=jnp.exp(-(b + u)), u2=u * LOG2E,
                decay=jnp.exp(b_end + m_prev - m_new), w=jnp.exp(a - m_new).astype(BF16), rt=rt_ref[g, c]))
            m_ref[g] = jnp.broadcast_to(m_new, m_ref.shape[1:])

        def qk_tiles(g, h):
            pair = slice(h // 2 * LANES, (h // 2 + 1) * LANES)
            return q_ref[g, rows, pair], k_ref[g, rows, pair]

        def head_scores(g, h):
            q2, k2 = qk_tiles(g, h)
            q_own = jnp.where(low if h % 2 == 0 else jnp.logical_not(low), q2, jnp.zeros_like(q2))
            return lax.dot_general(k2, q_own, NT_DIMS, preferred_element_type=F32)

        seq = [(g, h) for h in range(N_HEADS) for g in range(group)]
        pending = [head_scores(*u) for u in seq[:MLSTM_AHEAD]]
        for n, (g, h) in enumerate(seq):
            st_h = pending.pop(0)
            terms = row_terms[g]
            q2, k2 = qk_tiles(g, h)
            ct = ct_ref[g, h]
            vaug = jnp.concatenate([vt_refs[g][h * DV:(h + 1) * DV, rows], ones_rows], axis=0)
            inter_num = lax.dot_general(ct.astype(BF16), q2, NT_DIMS, preferred_element_type=F32)
            upd = jnp.dot(vaug * terms["w"][h:h + 1, :], k2, preferred_element_type=F32)
            own = low_state if h % 2 == 0 else jnp.logical_not(low_state)
            ct_ref[g, h] = terms["decay"][h:h + 1, :] * ct + jnp.where(own, upd, 0.0)
            if n + MLSTM_AHEAD < len(seq):
                pending.append(head_scores(*seq[n + MLSTM_AHEAD]))
            cols = slice(h * DV, (h + 1) * DV)
            r_col, u_row = terms["rt"][:, h:h + 1], terms["u2"][h:h + 1, :]
            d00 = jnp.exp2(jnp.where(tri, r_col[:half] - u_row[:, :half], neg_inf))
            d01 = jnp.exp2(r_col[:half] - u_row[:, half:])
            d11 = jnp.exp2(jnp.where(tri, r_col[half:] - u_row[:, half:], neg_inf))
            s_h = jnp.concatenate(
                [jnp.concatenate([st_h[:half, :half] * d00, st_h[:half, half:] * d01], axis=1),
                 jnp.concatenate([jnp.zeros((half, half), F32), st_h[half:, half:] * d11], axis=1)],
                axis=0).astype(BF16)
            num = (terms["inter"][h:h + 1, :] * inter_num
                   + jnp.dot(vaug, s_h, preferred_element_type=F32))
            inv = 1.0 / jnp.maximum(jnp.abs(num[DV:DV + 1, :]), terms["e_row"][h:h + 1, :])
            hh = num[0:DV, :] * inv
            hh = hh * lax.rsqrt(jnp.mean(hh * hh, axis=0, keepdims=True) + EPS)
            y_ref[g, rows, cols] = (hh.T * gain_ref[:, cols]).astype(BF16) * o_ref[g, rows, cols]
        return carry

    lax.fori_loop(0, chunks_per_step, chunk_body, 0)
    ct_out_ref[...] = ct_ref[...]
    m_out_ref[...] = m_ref[...]


def _mlstm_core(q, k, vt, o, gates_t, bias, gain, ct0, m0, batch, chunks_per_step, group, name):
    tokens = q.shape[0]
    rows = chunks_per_step * MCHUNK
    per_group = batch // group
    steps = tokens // batch // rows
    assert per_group * group == batch and batch * steps * rows == tokens
    ct_shape = (N_HEADS, VAUG_ROWS, LANES)
    m_shape = (N_HEADS, LANES)
    gate_shape = (group, chunks_per_step, N_HEADS, MCHUNK)

    def tok(width):
        return pl.BlockSpec((group, rows, width), lambda b, j: (0, b * steps + j, 0))

    def tok_t(height, g):
        return pl.BlockSpec((height, rows), lambda b, j, g=g: (0, (g * per_group + b) * steps + j))

    def grouped(x):
        return x.reshape(group, tokens // group, x.shape[1])

    y, ct, m = pl.pallas_call(
        functools.partial(_mlstm_kernel, chunks_per_step=chunks_per_step, group=group),
        grid=(per_group, steps),
        in_specs=([tok(N_HEADS * DQK), tok(N_HEADS * DQK), tok(N_HEADS * DV)]
                  + [tok_t(N_HEADS * DV, g) for g in range(group)]
                  + [tok_t(GATE_ROWS, g) for g in range(group)]
                  + [_resident((2 * N_HEADS, 1)), _resident((1, N_HEADS * DV)),
                     _resident(ct_shape), _resident(m_shape)]),
        out_specs=[tok(N_HEADS * DV),
                   pl.BlockSpec((group, None) + ct_shape, lambda b, j: (0, b, 0, 0, 0)),
                   pl.BlockSpec((group, None) + m_shape, lambda b, j: (0, b, 0, 0))],
        out_shape=[jax.ShapeDtypeStruct((group, tokens // group, N_HEADS * DV), BF16),
                   jax.ShapeDtypeStruct((group, per_group) + ct_shape, F32),
                   jax.ShapeDtypeStruct((group, per_group) + m_shape, F32)],
        scratch_shapes=[pltpu.VMEM((group,) + ct_shape, F32), pltpu.VMEM((group,) + m_shape, F32),
                        pltpu.VMEM(gate_shape, F32), pltpu.VMEM(gate_shape, F32), pltpu.VMEM(gate_shape, F32),
                        pltpu.VMEM((group, chunks_per_step, MCHUNK, N_HEADS), F32)],
        compiler_params=_compiler_params(("parallel", "arbitrary")),
        name=name,
    )(grouped(q), grouped(k), grouped(o), *([vt] * group), *([gates_t] * group), bias, gain, ct0, m0)
    return y.reshape(tokens, N_HEADS * DV), ct, m


def _diff_attn_kernel(lam_ref, q_ref, k_ref, vt_ref, km_ref, vtm_ref, gain_ref, o_ref,
                      qs_ref, m_ref, acc_ref, sc_ref, *, lambda_init):
    qi = pl.program_id(1)
    tq, tk = ATT_TQ, ATT_TK
    blocks = tq // tk
    neg_inf = jnp.float32(-jnp.inf)

    low = lax.broadcasted_iota(jnp.int32, (tq, LANES), 1) < DQK
    for h in range(N_HEADS):
        q2 = q_ref[:, h * DV:(h + 1) * DV]
        zero = jnp.zeros_like(q2)
        qs_ref[h] = jnp.concatenate([jnp.where(low, q2, zero), jnp.where(low, zero, q2)], axis=0)

    all_units = [(h, br, blk) for h in range(N_HEADS) for br in range(2) for blk in range(blocks)]

    def unit_cols(br, blk):
        return pl.ds((br * blocks + blk) * tk, tk)

    def run(items, primed, next_items):
        stream = items + next_items

        def scores(item):
            tile, (h, br, blk) = item
            s = lax.dot_general(tile["k"](h), qs_ref[h, unit_cols(br, blk), :], NT_DIMS,
                                preferred_element_type=F32)
            bias = tile["bias"](blk)
            return s if bias is None else s + bias

        pending = [sc_ref[n] if primed else scores(stream[n]) for n in range(ATT_AHEAD)]
        for n, (tile, (h, br, blk)) in enumerate(items):
            s = pending.pop(0)
            ahead = n + ATT_AHEAD
            if ahead < len(items):
                pending.append(scores(stream[ahead]))
            elif ahead < len(stream):
                sc_ref[ahead - len(items)] = scores(stream[ahead])
            cols = unit_cols(br, blk)
            v_aug = jnp.concatenate([tile["vt"](h), tile["ones"]], axis=0)
            s_max = jnp.max(s, axis=0, keepdims=True)
            if tile["first"]:
                m_new = s_max
                acc_ref[h, :, cols] = jnp.dot(v_aug, jnp.exp2(s - m_new).astype(BF16), preferred_element_type=F32)
            else:
                m_old = m_ref[h, :, cols]
                m_new = jnp.maximum(m_old, s_max)
                alpha = jnp.exp2(m_old - m_new)
                p = jnp.exp2(s - m_new).astype(BF16)
                acc_ref[h, :, cols] = alpha * acc_ref[h, :, cols] + jnp.dot(v_aug, p, preferred_element_type=F32)
            m_ref[h, :, cols] = m_new
        assert not pending and len(items) >= ATT_AHEAD

    ones_tile = _ones_rows(tk)

    def frame_tile(start, bias_of):
        return dict(k=lambda h: k_ref[pl.ds(start, tk), h * DV:(h + 1) * DV],
                    vt=lambda h: vt_ref[h * DV:(h + 1) * DV, pl.ds(start, tk)],
                    ones=ones_tile, bias=bias_of, first=False)

    def full_items(start, units):
        tile = frame_tile(start, lambda blk: None)
        return [(tile, u) for u in units]

    chunk_shift = CHUNK.bit_length() - 1
    k_chunk = lax.shift_right_logical(lax.broadcasted_iota(jnp.int32, (tk, tk), 0), chunk_shift)
    q_chunk = lax.shift_right_logical(lax.broadcasted_iota(jnp.int32, (tk, tk), 1), chunk_shift)
    diag_bias = jnp.where(k_chunk <= q_chunk, 0.0, neg_inf)
    first_bias = jnp.concatenate([diag_bias, jnp.zeros((N_META, tk), F32)], axis=0)
    start0 = pl.multiple_of(qi * tq, tk)
    first_tile = dict(
        k=lambda h: jnp.concatenate([k_ref[pl.ds(start0, tk), h * DV:(h + 1) * DV],
                                     km_ref[:, h * DV:(h + 1) * DV]], axis=0),
        vt=lambda h: jnp.concatenate([vt_ref[h * DV:(h + 1) * DV, pl.ds(start0, tk)],
                                      vtm_ref[h * DV:(h + 1) * DV, :]], axis=1),
        ones=_ones_rows(tk + N_META), bias=lambda blk: first_bias if blk == 0 else None, first=True)
    items = [(first_tile, u) for u in all_units]
    for d in range(1, blocks):
        tile = frame_tile(pl.multiple_of(qi * tq + d * tk, tk), lambda blk, d=d: diag_bias if blk == d else None)
        items += [(tile, u) for u in all_units if u[2] >= d]
    head_units = all_units[:ATT_AHEAD]
    run(items, primed=False, next_items=full_items(0, head_units))

    n_full = qi * blocks
    last_tile = k_ref.shape[0] // tk - 1

    def full_tile(j, carry):
        nxt = pl.multiple_of(jnp.minimum(j + 1, last_tile) * tk, tk)
        run(full_items(pl.multiple_of(j * tk, tk), all_units), primed=True,
            next_items=full_items(nxt, head_units))
        return carry

    lax.fori_loop(0, n_full, full_tile, 0)

    lv = lam_ref[...]
    lam = (jnp.exp(jnp.sum(lv[0:1] * lv[1:2], axis=-1, keepdims=True))
           - jnp.exp(jnp.sum(lv[2:3] * lv[3:4], axis=-1, keepdims=True)) + lambda_init)
    out_gain = gain_ref[...] * (1.0 - lambda_init)
    for h in range(N_HEADS):
        cols = slice(h * DV, (h + 1) * DV)
        acc = acc_ref[h]
        inv_l = 1.0 / acc[DV:DV + 1, :]
        o_t = acc[0:DV, :tq] * inv_l[:, :tq] - lam * (acc[0:DV, tq:] * inv_l[:, tq:])
        o_t = o_t * lax.rsqrt(jnp.mean(o_t * o_t, axis=0, keepdims=True) + SUBLN_EPS)
        o_ref[:, cols] = (o_t.T * out_gain[:, cols]).astype(o_ref.dtype)


def _diff_attention(q, k, vt, k_meta, vt_meta, lam_vecs, gain, batch, lambda_init):
    t = q.shape[0]
    seq = t // batch
    nq = seq // ATT_TQ
    assert nq * ATT_TQ == seq and ATT_TQ % ATT_TK == 0 and ATT_TK % CHUNK == 0
    assert k_meta.shape[0] == N_META and N_META % BF16_ROWS == 0
    width = N_HEADS * DV
    return pl.pallas_call(
        functools.partial(_diff_attn_kernel, lambda_init=lambda_init),
        grid=(batch, nq),
        in_specs=[_resident(lam_vecs.shape),
                  pl.BlockSpec((ATT_TQ, width), lambda b, i: (b * nq + i, 0)),
                  pl.BlockSpec((seq, width), lambda b, i: (b, 0)),
                  pl.BlockSpec((width, seq), lambda b, i: (0, b)),
                  _resident(k_meta.shape),
                  _resident(vt_meta.shape),
                  _resident((1, width))],
        out_specs=pl.BlockSpec((ATT_TQ, width), lambda b, i: (b * nq + i, 0)),
        out_shape=jax.ShapeDtypeStruct((t, width), BF16),
        scratch_shapes=[pltpu.VMEM((N_HEADS, 2 * ATT_TQ, DV), BF16),
                        pltpu.VMEM((N_HEADS, 1, 2 * ATT_TQ), F32),
                        pltpu.VMEM((N_HEADS, VAUG_ROWS, 2 * ATT_TQ), F32),
                        pltpu.VMEM((ATT_AHEAD, ATT_TK, ATT_TK), F32)],
        compiler_params=_compiler_params(("parallel", "arbitrary")),
        name="diff_attention",
    )(lam_vecs, q, k, vt, k_meta, vt_meta, gain)


def _post_mlp_kernel(h_ref, y_ref, wo_ref, g_ref, wu_ref, wd_ref, fg_ref, *refs, final_norm, ff_block,
                     cast_transposed):
    n_cast = len(cast_transposed)
    out_ref = refs[n_cast]
    _cast_slabs(refs[:n_cast], refs[n_cast + 1:], cast_transposed)
    h1 = h_ref[...] + jnp.dot(y_ref[...], wo_ref[...], preferred_element_type=F32)
    ms = jnp.mean(h1 * h1, axis=-1, keepdims=True)
    hn = (h1 * lax.rsqrt(ms + EPS) * g_ref[...]).astype(BF16)
    acc = h1
    for c in range(D_FF // ff_block):
        blk = slice(c * ff_block, (c + 1) * ff_block)
        u = jnp.dot(hn, wu_ref[:, blk], preferred_element_type=F32)
        a = jnp.square(jnp.maximum(u, 0.0)).astype(BF16)
        acc = acc + jnp.dot(a, wd_ref[blk, :], preferred_element_type=F32)
    if final_norm:
        ms = jnp.mean(acc * acc, axis=-1, keepdims=True)
        acc = acc * lax.rsqrt(ms + EPS) * fg_ref[...]
    out_ref[...] = acc


def _post_mlp(h, y, w_out, gain, w_up, w_down, final_gain, final_norm, tm, name, to_cast=()):
    t = h.shape[0]
    assert t % tm == 0
    cast_in, cast_out, cast_shapes = _cast_specs(to_cast, t // tm)
    return pl.pallas_call(
        functools.partial(_post_mlp_kernel, final_norm=final_norm, ff_block=MLP_FF_BLOCK,
                          cast_transposed=tuple(job["transpose"] for job in to_cast)),
        grid=(t // tm,),
        in_specs=[pl.BlockSpec((tm, D_MODEL), lambda i: (i, 0)),
                  pl.BlockSpec((tm, D_MODEL), lambda i: (i, 0)),
                  _resident((D_MODEL, D_MODEL)),
                  _resident((1, D_MODEL)),
                  _resident((D_MODEL, D_FF)),
                  _resident((D_FF, D_MODEL)),
                  _resident((1, D_MODEL))] + cast_in,
        out_specs=[pl.BlockSpec((tm, D_MODEL), lambda i: (i, 0))] + cast_out,
        out_shape=[jax.ShapeDtypeStruct((t, D_MODEL), F32)] + cast_shapes,
        compiler_params=_compiler_params(("arbitrary",)),
        name=name,
    )(h, y, w_out, gain, w_up, w_down, final_gain, *[job["array"] for job in to_cast])


def kernel(x, meta_tokens, norm_gains, mlstm_w_in, mlstm_b_gate, mlstm_head_gain, mlstm_w_out,
           diff_w_in, diff_lambda, diff_head_gain, diff_w_out, mlp_w_up, mlp_w_down, final_gain):
    batch, seq, _ = x.shape
    t = batch * seq
    h = x.reshape(t, D_MODEL)
    h_meta = jnp.concatenate([jnp.zeros((META_ROWS - N_META, D_MODEL), x.dtype), meta_tokens.astype(x.dtype)],
                             axis=0)

    m_qk = N_HEADS * DQK
    m_v = N_HEADS * DV
    w0 = mlstm_w_in[0]
    offs = (0, m_qk, 2 * m_qk, 2 * m_qk + m_v, 2 * m_qk + 2 * m_v)
    w0_parts = [w0[:, offs[3]:offs[4]],
                jnp.concatenate([w0[:, offs[2]:offs[3]], w0[:, offs[4]:]], axis=1).T,
                w0[:, offs[0]:offs[1]], w0[:, offs[1]:offs[2]]]
    w0_parts = [w.astype(BF16) for w in w0_parts]
    in0 = dict(gain=norm_gains[0, 0][None], weights=w0_parts, scales=(1.0, 1.0, DQK ** -0.5, 1.0),
               transposed=(False, True, False, False), sigmoid=(True, False, False, False),
               dtypes=(BF16, ((m_v, BF16), (GATE_ROWS, F32)), BF16, BF16))
    bias = mlstm_b_gate[0][:, None]
    head_gain0 = mlstm_head_gain[0][None]

    o_gate, vt, gates_t, q, k = _norm_proj(h_meta, tm=META_ROWS, name="mlstm_in_proj_meta", **in0)
    ct_zero = jnp.zeros((N_HEADS, VAUG_ROWS, LANES), F32)
    m_zero = jnp.zeros((N_HEADS, LANES), F32)
    y_meta, ct_meta, m_meta = _mlstm_core(q, k, vt, o_gate, gates_t, bias, head_gain0, ct_zero, m_zero, batch=1,
                                          chunks_per_step=1, group=1, name="mlstm_core_meta")

    o_gate, vt, gates_t, q, k, w_out0, w_up0, w_down0 = _norm_proj(
        h, tm=PROJ_TM, name="mlstm_in_proj",
        to_cast=(_cast_job(mlstm_w_out, 0), _cast_job(mlp_w_up, 0), _cast_job(mlp_w_down, 0)), **in0)
    mlp0 = dict(w_out=w_out0, gain=norm_gains[0, 1][None], w_up=w_up0, w_down=w_down0,
                final_gain=final_gain[None], final_norm=False)
    h_meta, = _post_mlp(h_meta, y_meta, tm=META_ROWS, name="mlstm_out_mlp_meta", **mlp0)
    y, _, _ = _mlstm_core(q, k, vt, o_gate, gates_t, bias, head_gain0, ct_meta[0, 0], m_meta[0, 0], batch=batch,
                          chunks_per_step=MLSTM_CHUNKS_PER_STEP, group=MLSTM_GROUP, name="mlstm_core")
    layer1_jobs = (_cast_job(diff_w_in, 0, 0, 3), _cast_job(diff_w_in, 0, 1, 3),
                   _cast_job(diff_w_in, 0, 2, 3, transpose=True),
                   _cast_job(diff_w_out, 0), _cast_job(mlp_w_up, 1), _cast_job(mlp_w_down, 1))
    h, w_q1, w_k1, w_vt1, w_out1, w_up1, w_down1 = _post_mlp(h, y, tm=MLP_TM, name="mlstm_out_mlp",
                                                             to_cast=layer1_jobs, **mlp0)

    lambda_init = 0.8 - 0.6 * math.exp(-0.3 * 1)
    in1 = dict(gain=norm_gains[1, 0][None], weights=[w_q1, w_k1, w_vt1], scales=(DQK ** -0.5 * LOG2E, 1.0, 1.0),
               transposed=(False, False, True), sigmoid=(False,) * 3, dtypes=(BF16, BF16, BF16))
    _, k_meta, vt_meta = _norm_proj(h_meta, tm=META_ROWS, name="diff_in_proj_meta", **in1)
    k_meta = k_meta[META_ROWS - N_META:]
    vt_meta = vt_meta[:, META_ROWS - N_META:]
    q, k, vt = _norm_proj(h, tm=PROJ_TM, name="diff_in_proj", **in1)
    y = _diff_attention(q, k, vt, k_meta, vt_meta, diff_lambda[0], diff_head_gain[0][None], batch, lambda_init)
    out, = _post_mlp(h, y, w_out1, norm_gains[1, 1][None], w_up1, w_down1, final_gain[None], final_norm=True,
                     tm=MLP_TM, name="diff_out_mlp")
    return out.reshape(batch, seq, D_MODEL)
```

```python
import functools
import math

import jax
import jax.numpy as jnp
from jax import lax
from jax.experimental import pallas as pl
from jax.experimental.pallas import tpu as pltpu

D_MODEL = 1024
D_FF = 4 * D_MODEL
N_META = 16
CHUNK = 64
N_HEADS = 8
DQK = 64
DV = 128
EPS = 1e-6
SUBLN_EPS = 1e-5
LOG2E = math.log2(math.e)

LANES = 128
BF16_ROWS = 16
AUG_ROWS = BF16_ROWS
VAUG_ROWS = DV + AUG_ROWS
GATE_ROWS = 2 * N_HEADS

MCHUNK = 256
MLSTM_CHUNKS_PER_STEP = 8
MLSTM_GROUP = 1
MLSTM_AHEAD = 4
META_ROWS = MCHUNK
ATT_TQ = 1024
ATT_TK = 256
ATT_AHEAD = 5

PROJ_TM = 1024
PROJ_SUB = 512
MLP_TM = 1024
MLP_FF_BLOCK = 512

VMEM_LIMIT = 56 * 1024 * 1024

F32 = jnp.float32
BF16 = jnp.bfloat16
NT_DIMS = (((1,), (1,)), ((), ()))


def _compiler_params(semantics):
    return pltpu.CompilerParams(dimension_semantics=semantics, vmem_limit_bytes=VMEM_LIMIT)


def _resident(shape):
    return pl.BlockSpec(shape, lambda *_: (0,) * len(shape), pipeline_mode=pl.Buffered(1))


def _ones_rows(width):
    return (lax.broadcasted_iota(jnp.int32, (AUG_ROWS, width), 0) == 0).astype(BF16)


def _cast_job(stacked, layer, col_block=0, n_col_blocks=1, transpose=False):
    return dict(array=stacked, layer=layer, col_block=col_block, n_col_blocks=n_col_blocks, transpose=transpose)


def _cast_specs(jobs, steps):
    in_specs, out_specs, out_shapes = [], [], []
    for job in jobs:
        _, n_rows, n_cols = job["array"].shape
        cols = n_cols // job["n_col_blocks"]
        layer, col_block = job["layer"], job["col_block"]
        if job["transpose"]:
            per_slab = steps * LANES // n_rows
            assert per_slab * n_rows == steps * LANES
            in_specs.append(pl.BlockSpec((None, LANES, cols),
                                         lambda i, l=layer, c=col_block, p=per_slab: (l, i // p, c)))
            out_specs.append(pl.BlockSpec((cols, LANES), lambda i, p=per_slab: (0, i // p)))
            out_shapes.append(jax.ShapeDtypeStruct((cols, n_rows), BF16))
        else:
            rows = n_rows // steps
            assert rows * steps == n_rows and rows % BF16_ROWS == 0
            in_specs.append(pl.BlockSpec((None, rows, cols), lambda i, l=layer, c=col_block: (l, i, c)))
            out_specs.append(pl.BlockSpec((rows, cols), lambda i: (i, 0)))
            out_shapes.append(jax.ShapeDtypeStruct((n_rows, cols), BF16))
    return in_specs, out_specs, out_shapes


def _cast_slabs(src_refs, dst_refs, transposed):
    for src, dst, tr in zip(src_refs, dst_refs, transposed):
        x = src[...]
        dst[...] = (x.T if tr else x).astype(dst.dtype)


def _norm_proj_kernel(x_ref, g_ref, *refs, scales, transposed, sigmoid, pieces, cast_transposed):
    n_w, n_cast, n_out = len(scales), len(cast_transposed), sum(len(p) for p in pieces)
    w_refs, out_refs = refs[:n_w], list(refs[n_w + n_cast:n_w + n_cast + n_out])
    _cast_slabs(refs[n_w:n_w + n_cast], refs[n_w + n_cast + n_out:], cast_transposed)
    tm = x_ref.shape[0]
    sub = min(PROJ_SUB, tm)
    for r0 in range(0, tm, sub):
        rows = slice(r0, r0 + sub)
        x = x_ref[rows, :]
        ms = jnp.mean(x * x, axis=-1, keepdims=True)
        xn = (x * lax.rsqrt(ms + EPS) * g_ref[...]).astype(BF16)
        outs = iter(out_refs)
        for w_ref, sc, tr, sg, sizes in zip(w_refs, scales, transposed, sigmoid, pieces):
            if tr:
                r = lax.dot_general(w_ref[...], xn, NT_DIMS, preferred_element_type=F32)
            else:
                r = jnp.dot(xn, w_ref[...], preferred_element_type=F32)
            if sc != 1.0:
                r = r * sc
            if sg:
                r = 0.5 * jnp.tanh(0.5 * r) + 0.5
            if tr:
                off = 0
                for n in sizes:
                    o_ref = next(outs)
                    o_ref[:, rows] = r[off:off + n].astype(o_ref.dtype)
                    off += n
            else:
                o_ref = next(outs)
                o_ref[rows, :] = r.astype(o_ref.dtype)


def _norm_proj(h, gain, weights, scales, transposed, sigmoid, dtypes, tm, name, to_cast=()):
    t = h.shape[0]
    assert t % tm == 0 and tm % min(PROJ_SUB, tm) == 0
    cast_in, cast_out, cast_shapes = _cast_specs(to_cast, t // tm)
    w_specs, out_specs, out_shapes, pieces = [], [], [], []
    for w, tr, dt in zip(weights, transposed, dtypes):
        w_specs.append(_resident(w.shape))
        if tr:
            parts = dt if isinstance(dt, tuple) else ((w.shape[0], dt),)
            assert sum(n for n, _ in parts) == w.shape[0]
            pieces.append(tuple(n for n, _ in parts))
            for n, part_dt in parts:
                out_specs.append(pl.BlockSpec((n, tm), lambda i: (0, i)))
                out_shapes.append(jax.ShapeDtypeStruct((n, t), part_dt))
        else:
            n = w.shape[1]
            pieces.append((n,))
            out_specs.append(pl.BlockSpec((tm, n), lambda i: (i, 0)))
            out_shapes.append(jax.ShapeDtypeStruct((t, n), dt))
    return pl.pallas_call(
        functools.partial(_norm_proj_kernel, scales=scales, transposed=transposed, sigmoid=sigmoid,
                          pieces=tuple(pieces), cast_transposed=tuple(job["transpose"] for job in to_cast)),
        grid=(t // tm,),
        in_specs=[pl.BlockSpec((tm, D_MODEL), lambda i: (i, 0)), _resident((1, D_MODEL))] + w_specs + cast_in,
        out_specs=out_specs + cast_out,
        out_shape=out_shapes + cast_shapes,
        compiler_params=_compiler_params(("arbitrary",)),
        name=name,
    )(h, gain, *weights, *[job["array"] for job in to_cast])


def _mlstm_kernel(*refs, chunks_per_step, group):
    q_ref, k_ref, o_ref = refs[0:3]
    vt_refs = refs[3:3 + group]
    gt_refs = refs[3 + group:3 + 2 * group]
    (bias_ref, gain_ref, ct0_ref, m0_ref, y_ref, ct_out_ref, m_out_ref,
     ct_ref, m_ref, b_ref, r_ref, cm_ref, rt_ref) = refs[3 + 2 * group:]
    t = MCHUNK

    @pl.when(pl.program_id(1) == 0)
    def _():
        for g in range(group):
            ct_ref[g] = ct0_ref[...]
            m_ref[g] = m0_ref[...]

    neg_inf = jnp.float32(-jnp.inf)
    causal = (lax.broadcasted_iota(jnp.int32, (t, t), 0)
              <= lax.broadcasted_iota(jnp.int32, (t, t), 1))
    upper = causal.astype(BF16)
    lane_t = lax.broadcasted_iota(jnp.int32, (N_HEADS, t), 1)
    bias = bias_ref[...]

    for g in range(group):
        for c in range(chunks_per_step):
            gates = gt_refs[g][:, c * t:(c + 1) * t] + bias
            ig = gates[0:N_HEADS]
            lf = jax.nn.log_sigmoid(gates[N_HEADS:])
            hi = lf.astype(BF16).astype(F32)
            mid = (lf - hi).astype(BF16).astype(F32)
            lo = (lf - hi - mid).astype(BF16).astype(F32)
            parts = jnp.dot(jnp.concatenate([hi, mid, lo], axis=0).astype(BF16), upper,
                            preferred_element_type=F32)
            b = parts[0:N_HEADS] + parts[N_HEADS:2 * N_HEADS] + parts[2 * N_HEADS:]
            r = ig - b
            cm = r
            shift = 1
            while shift < t:
                cm = jnp.maximum(cm, jnp.where(lane_t >= shift, pltpu.roll(cm, shift, 1), neg_inf))
                shift *= 2
            b_ref[g, c] = b
            r_ref[g, c] = r
            cm_ref[g, c] = cm
            rt_ref[g, c] = (r * LOG2E).T

    low = lax.broadcasted_iota(jnp.int32, (t, LANES), 1) < DQK
    low_state = lax.broadcasted_iota(jnp.int32, (VAUG_ROWS, LANES), 1) < DQK
    ones_rows = _ones_rows(t)
    half = t // 2
    tri = causal[:half, :half]

    def chunk_body(c, carry):
        rows = pl.ds(pl.multiple_of(c * t, t), t)
        row_terms = []
        for g in range(group):
            b, r, cm = b_ref[g, c], r_ref[g, c], cm_ref[g, c]
            m_prev = m_ref[g, :, 0:1]
            u = jnp.maximum(m_prev, cm)
            b_end = b[:, t - 1:t]
            a = b_end + r
            m_new = jnp.maximum(b_end + m_prev, jnp.max(a, axis=1, keepdims=True))
            row_terms.append(dict(
                inter=jnp.exp(m_prev - u), e_row=jnp.exp(-(b + u)), u2=u * LOG2E,
                decay=jnp.exp(b_end + m_prev - m_new), w=jnp.exp(a - m_new).astype(BF16), rt=rt_ref[g, c]))
            m_ref[g] = jnp.broadcast_to(m_new, m_ref.shape[1:])

        def qk_tiles(g, h):
            pair = slice(h // 2 * LANES, (h // 2 + 1) * LANES)
            return q_ref[g, rows, pair], k_ref[g, rows, pair]

        def head_scores(g, h):
            q2, k2 = qk_tiles(g, h)
            q_own = jnp.where(low if h % 2 == 0 else jnp.logical_not(low), q2, jnp.zeros_like(q2))
            return lax.dot_general(k2, q_own, NT_DIMS, preferred_element_type=F32)

        seq = [(g, h) for h in range(N_HEADS) for g in range(group)]
        pending = [head_scores(*u) for u in seq[:MLSTM_AHEAD]]
        for n, (g, h) in enumerate(seq):
            st_h = pending.pop(0)
            terms = row_terms[g]
            q2, k2 = qk_tiles(g, h)
            ct = ct_ref[g, h]
            vaug = jnp.concatenate([vt_refs[g][h * DV:(h + 1) * DV, rows], ones_rows], axis=0)
            inter_num = lax.dot_general(ct.astype(BF16), q2, NT_DIMS, preferred_element_type=F32)
            upd = jnp.dot(vaug * terms["w"][h:h + 1, :], k2, preferred_element_type=F32)
            own = low_state if h % 2 == 0 else jnp.logical_not(low_state)
            ct_ref[g, h] = terms["decay"][h:h + 1, :] * ct + jnp.where(own, upd, 0.0)
            if n + MLSTM_AHEAD < len(seq):
                pending.append(head_scores(*seq[n + MLSTM_AHEAD]))
            cols = slice(h * DV, (h + 1) * DV)
            r_col, u_row = terms["rt"][:, h:h + 1], terms["u2"][h:h + 1, :]
            d00 = jnp.exp2(jnp.where(tri, r_col[:half] - u_row[:, :half], neg_inf))
            d01 = jnp.exp2(r_col[:half] - u_row[:, half:])
            d11 = jnp.exp2(jnp.where(tri, r_col[half:] - u_row[:, half:], neg_inf))
            s_h = jnp.concatenate(
                [jnp.concatenate([st_h[:half, :half] * d00, st_h[:half, half:] * d01], axis=1),
                 jnp.concatenate([jnp.zeros((half, half), F32), st_h[half:, half:] * d11], axis=1)],
                axis=0).astype(BF16)
            num = (terms["inter"][h:h + 1, :] * inter_num
                   + jnp.dot(vaug, s_h, preferred_element_type=F32))
            inv = 1.0 / jnp.maximum(jnp.abs(num[DV:DV + 1, :]), terms["e_row"][h:h + 1, :])
            hh = num[0:DV, :] * inv
            hh = hh * lax.rsqrt(jnp.mean(hh * hh, axis=0, keepdims=True) + EPS)
            y_ref[g, rows, cols] = (hh.T * gain_ref[:, cols]).astype(BF16) * o_ref[g, rows, cols]
        return carry

    lax.fori_loop(0, chunks_per_step, chunk_body, 0)
    ct_out_ref[...] = ct_ref[...]
    m_out_ref[...] = m_ref[...]


def _mlstm_core(q, k, vt, o, gates_t, bias, gain, ct0, m0, batch, chunks_per_step, group, name):
    tokens = q.shape[0]
    rows = chunks_per_step * MCHUNK
    per_group = batch // group
    steps = tokens // batch // rows
    assert per_group * group == batch and batch * steps * rows == tokens
    ct_shape = (N_HEADS, VAUG_ROWS, LANES)
    m_shape = (N_HEADS, LANES)
    gate_shape = (group, chunks_per_step, N_HEADS, MCHUNK)

    def tok(width):
        return pl.BlockSpec((group, rows, width), lambda b, j: (0, b * steps + j, 0))

    def tok_t(height, g):
        return pl.BlockSpec((height, rows), lambda b, j, g=g: (0, (g * per_group + b) * steps + j))

    def grouped(x):
        return x.reshape(group, tokens // group, x.shape[1])

    y, ct, m = pl.pallas_call(
        functools.partial(_mlstm_kernel, chunks_per_step=chunks_per_step, group=group),
        grid=(per_group, steps),
        in_specs=([tok(N_HEADS * DQK), tok(N_HEADS * DQK), tok(N_HEADS * DV)]
                  + [tok_t(N_HEADS * DV, g) for g in range(group)]
                  + [tok_t(GATE_ROWS, g) for g in range(group)]
                  + [_resident((2 * N_HEADS, 1)), _resident((1, N_HEADS * DV)),
                     _resident(ct_shape), _resident(m_shape)]),
        out_specs=[tok(N_HEADS * DV),
                   pl.BlockSpec((group, None) + ct_shape, lambda b, j: (0, b, 0, 0, 0)),
                   pl.BlockSpec((group, None) + m_shape, lambda b, j: (0, b, 0, 0))],
        out_shape=[jax.ShapeDtypeStruct((group, tokens // group, N_HEADS * DV), BF16),
                   jax.ShapeDtypeStruct((group, per_group) + ct_shape, F32),
                   jax.ShapeDtypeStruct((group, per_group) + m_shape, F32)],
        scratch_shapes=[pltpu.VMEM((group,) + ct_shape, F32), pltpu.VMEM((group,) + m_shape, F32),
                        pltpu.VMEM(gate_shape, F32), pltpu.VMEM(gate_shape, F32), pltpu.VMEM(gate_shape, F32),
                        pltpu.VMEM((group, chunks_per_step, MCHUNK, N_HEADS), F32)],
        compiler_params=_compiler_params(("parallel", "arbitrary")),
        name=name,
    )(grouped(q), grouped(k), grouped(o), *([vt] * group), *([gates_t] * group), bias, gain, ct0, m0)
    return y.reshape(tokens, N_HEADS * DV), ct, m


def _diff_attn_kernel(lam_ref, q_ref, k_ref, vt_ref, km_ref, vtm_ref, gain_ref, o_ref,
                      qs_ref, m_ref, acc_ref, sc_ref, *, lambda_init):
    qi = pl.program_id(1)
    tq, tk = ATT_TQ, ATT_TK
    blocks = tq // tk
    neg_inf = jnp.float32(-jnp.inf)

    low = lax.broadcasted_iota(jnp.int32, (tq, LANES), 1) < DQK
    for h in range(N_HEADS):
        q2 = q_ref[:, h * DV:(h + 1) * DV]
        zero = jnp.zeros_like(q2)
        qs_ref[h] = jnp.concatenate([jnp.where(low, q2, zero), jnp.where(low, zero, q2)], axis=0)

    all_units = [(h, br, blk) for h in range(N_HEADS) for br in range(2) for blk in range(blocks)]

    def unit_cols(br, blk):
        return pl.ds((br * blocks + blk) * tk, tk)

    def run(items, primed, next_items):
        stream = items + next_items

        def scores(item):
            tile, (h, br, blk) = item
            s = lax.dot_general(tile["k"](h), qs_ref[h, unit_cols(br, blk), :], NT_DIMS,
                                preferred_element_type=F32)
            bias = tile["bias"](blk)
            return s if bias is None else s + bias

        pending = [sc_ref[n] if primed else scores(stream[n]) for n in range(ATT_AHEAD)]
        for n, (tile, (h, br, blk)) in enumerate(items):
            s = pending.pop(0)
            ahead = n + ATT_AHEAD
            if ahead < len(items):
                pending.append(scores(stream[ahead]))
            elif ahead < len(stream):
                sc_ref[ahead - len(items)] = scores(stream[ahead])
            cols = unit_cols(br, blk)
            v_aug = jnp.concatenate([tile["vt"](h), tile["ones"]], axis=0)
            s_max = jnp.max(s, axis=0, keepdims=True)
            if tile["first"]:
                m_new = s_max
                acc_ref[h, :, cols] = jnp.dot(v_aug, jnp.exp2(s - m_new).astype(BF16), preferred_element_type=F32)
            else:
                m_old = m_ref[h, :, cols]
                m_new = jnp.maximum(m_old, s_max)
                alpha = jnp.exp2(m_old - m_new)
                p = jnp.exp2(s - m_new).astype(BF16)
                acc_ref[h, :, cols] = alpha * acc_ref[h, :, cols] + jnp.dot(v_aug, p, preferred_element_type=F32)
            m_ref[h, :, cols] = m_new
        assert not pending and len(items) >= ATT_AHEAD

    ones_tile = _ones_rows(tk)

    def frame_tile(start, bias_of):
        return dict(k=lambda h: k_ref[pl.ds(start, tk), h * DV:(h + 1) * DV],
                    vt=lambda h: vt_ref[h * DV:(h + 1) * DV, pl.ds(start, tk)],
                    ones=ones_tile, bias=bias_of, first=False)

    def full_items(start, units):
        tile = frame_tile(start, lambda blk: None)
        return [(tile, u) for u in units]

    chunk_shift = CHUNK.bit_length() - 1
    k_chunk = lax.shift_right_logical(lax.broadcasted_iota(jnp.int32, (tk, tk), 0), chunk_shift)
    q_chunk = lax.shift_right_logical(lax.broadcasted_iota(jnp.int32, (tk, tk), 1), chunk_shift)
    diag_bias = jnp.where(k_chunk <= q_chunk, 0.0, neg_inf)
    first_bias = jnp.concatenate([diag_bias, jnp.zeros((N_META, tk), F32)], axis=0)
    start0 = pl.multiple_of(qi * tq, tk)
    first_tile = dict(
        k=lambda h: jnp.concatenate([k_ref[pl.ds(start0, tk), h * DV:(h + 1) * DV],
                                     km_ref[:, h * DV:(h + 1) * DV]], axis=0),
        vt=lambda h: jnp.concatenate([vt_ref[h * DV:(h + 1) * DV, pl.ds(start0, tk)],
                                      vtm_ref[h * DV:(h + 1) * DV, :]], axis=1),
        ones=_ones_rows(tk + N_META), bias=lambda blk: first_bias if blk == 0 else None, first=True)
    items = [(first_tile, u) for u in all_units]
    for d in range(1, blocks):
        tile = frame_tile(pl.multiple_of(qi * tq + d * tk, tk), lambda blk, d=d: diag_bias if blk == d else None)
        items += [(tile, u) for u in all_units if u[2] >= d]
    head_units = all_units[:ATT_AHEAD]
    run(items, primed=False, next_items=full_items(0, head_units))

    n_full = qi * blocks
    last_tile = k_ref.shape[0] // tk - 1

    def full_tile(j, carry):
        nxt = pl.multiple_of(jnp.minimum(j + 1, last_tile) * tk, tk)
        run(full_items(pl.multiple_of(j * tk, tk), all_units), primed=True,
            next_items=full_items(nxt, head_units))
        return carry

    lax.fori_loop(0, n_full, full_tile, 0)

    lv = lam_ref[...]
    lam = (jnp.exp(jnp.sum(lv[0:1] * lv[1:2], axis=-1, keepdims=True))
           - jnp.exp(jnp.sum(lv[2:3] * lv[3:4], axis=-1, keepdims=True)) + lambda_init)
    out_gain = gain_ref[...] * (1.0 - lambda_init)
    for h in range(N_HEADS):
        cols = slice(h * DV, (h + 1) * DV)
        acc = acc_ref[h]
        inv_l = 1.0 / acc[DV:DV + 1, :]
        o_t = acc[0:DV, :tq] * inv_l[:, :tq] - lam * (acc[0:DV, tq:] * inv_l[:, tq:])
        o_t = o_t * lax.rsqrt(jnp.mean(o_t * o_t, axis=0, keepdims=True) + SUBLN_EPS)
        o_ref[:, cols] = (o_t.T * out_gain[:, cols]).astype(o_ref.dtype)


def _diff_attention(q, k, vt, k_meta, vt_meta, lam_vecs, gain, batch, lambda_init):
    t = q.shape[0]
    seq = t // batch
    nq = seq // ATT_TQ
    assert nq * ATT_TQ == seq and ATT_TQ % ATT_TK == 0 and ATT_TK % CHUNK == 0
    assert k_meta.shape[0] == N_META and N_META % BF16_ROWS == 0
    width = N_HEADS * DV
    return pl.pallas_call(
        functools.partial(_diff_attn_kernel, lambda_init=lambda_init),
        grid=(batch, nq),
        in_specs=[_resident(lam_vecs.shape),
                  pl.BlockSpec((ATT_TQ, width), lambda b, i: (b * nq + i, 0)),
                  pl.BlockSpec((seq, width), lambda b, i: (b, 0)),
                  pl.BlockSpec((width, seq), lambda b, i: (0, b)),
                  _resident(k_meta.shape),
                  _resident(vt_meta.shape),
                  _resident((1, width))],
        out_specs=pl.BlockSpec((ATT_TQ, width), lambda b, i: (b * nq + i, 0)),
        out_shape=jax.ShapeDtypeStruct((t, width), BF16),
        scratch_shapes=[pltpu.VMEM((N_HEADS, 2 * ATT_TQ, DV), BF16),
                        pltpu.VMEM((N_HEADS, 1, 2 * ATT_TQ), F32),
                        pltpu.VMEM((N_HEADS, VAUG_ROWS, 2 * ATT_TQ), F32),
                        pltpu.VMEM((ATT_AHEAD, ATT_TK, ATT_TK), F32)],
        compiler_params=_compiler_params(("parallel", "arbitrary")),
        name="diff_attention",
    )(lam_vecs, q, k, vt, k_meta, vt_meta, gain)


def _post_mlp_kernel(h_ref, y_ref, wo_ref, g_ref, wu_ref, wd_ref, fg_ref, *refs, final_norm, ff_block,
                     cast_transposed):
    n_cast = len(cast_transposed)
    out_ref = refs[n_cast]
    _cast_slabs(refs[:n_cast], refs[n_cast + 1:], cast_transposed)
    h1 = h_ref[...] + jnp.dot(y_ref[...], wo_ref[...], preferred_element_type=F32)
    ms = jnp.mean(h1 * h1, axis=-1, keepdims=True)
    hn = (h1 * lax.rsqrt(ms + EPS) * g_ref[...]).astype(BF16)
    acc = h1
    for c in range(D_FF // ff_block):
        blk = slice(c * ff_block, (c + 1) * ff_block)
        u = jnp.dot(hn, wu_ref[:, blk], preferred_element_type=F32)
        a = jnp.square(jnp.maximum(u, 0.0)).astype(BF16)
        acc = acc + jnp.dot(a, wd_ref[blk, :], preferred_element_type=F32)
    if final_norm:
        ms = jnp.mean(acc * acc, axis=-1, keepdims=True)
        acc = acc * lax.rsqrt(ms + EPS) * fg_ref[...]
    out_ref[...] = acc


def _post_mlp(h, y, w_out, gain, w_up, w_down, final_gain, final_norm, tm, name, to_cast=()):
    t = h.shape[0]
    assert t % tm == 0
    cast_in, cast_out, cast_shapes = _cast_specs(to_cast, t // tm)
    return pl.pallas_call(
        functools.partial(_post_mlp_kernel, final_norm=final_norm, ff_block=MLP_FF_BLOCK,
                          cast_transposed=tuple(job["transpose"] for job in to_cast)),
        grid=(t // tm,),
        in_specs=[pl.BlockSpec((tm, D_MODEL), lambda i: (i, 0)),
                  pl.BlockSpec((tm, D_MODEL), lambda i: (i, 0)),
                  _resident((D_MODEL, D_MODEL)),
                  _resident((1, D_MODEL)),
                  _resident((D_MODEL, D_FF)),
                  _resident((D_FF, D_MODEL)),
                  _resident((1, D_MODEL))] + cast_in,
        out_specs=[pl.BlockSpec((tm, D_MODEL), lambda i: (i, 0))] + cast_out,
        out_shape=[jax.ShapeDtypeStruct((t, D_MODEL), F32)] + cast_shapes,
        compiler_params=_compiler_params(("arbitrary",)),
        name=name,
    )(h, y, w_out, gain, w_up, w_down, final_gain, *[job["array"] for job in to_cast])


def kernel(x, meta_tokens, norm_gains, mlstm_w_in, mlstm_b_gate, mlstm_head_gain, mlstm_w_out,
           diff_w_in, diff_lambda, diff_head_gain, diff_w_out, mlp_w_up, mlp_w_down, final_gain):
    batch, seq, _ = x.shape
    t = batch * seq
    h = x.reshape(t, D_MODEL)
    h_meta = jnp.concatenate([jnp.zeros((META_ROWS - N_META, D_MODEL), x.dtype), meta_tokens.astype(x.dtype)],
                             axis=0)

    m_qk = N_HEADS * DQK
    m_v = N_HEADS * DV
    w0 = mlstm_w_in[0]
    offs = (0, m_qk, 2 * m_qk, 2 * m_qk + m_v, 2 * m_qk + 2 * m_v)
    w0_parts = [w0[:, offs[3]:offs[4]],
                jnp.concatenate([w0[:, offs[2]:offs[3]], w0[:, offs[4]:]], axis=1).T,
                w0[:, offs[0]:offs[1]], w0[:, offs[1]:offs[2]]]
    w0_parts = [w.astype(BF16) for w in w0_parts]
    in0 = dict(gain=norm_gains[0, 0][None], weights=w0_parts, scales=(1.0, 1.0, DQK ** -0.5, 1.0),
               transposed=(False, True, False, False), sigmoid=(True, False, False, False),
               dtypes=(BF16, ((m_v, BF16), (GATE_ROWS, F32)), BF16, BF16))
    bias = mlstm_b_gate[0][:, None]
    head_gain0 = mlstm_head_gain[0][None]

    o_gate, vt, gates_t, q, k = _norm_proj(h_meta, tm=META_ROWS, name="mlstm_in_proj_meta", **in0)
    ct_zero = jnp.zeros((N_HEADS, VAUG_ROWS, LANES), F32)
    m_zero = jnp.zeros((N_HEADS, LANES), F32)
    y_meta, ct_meta, m_meta = _mlstm_core(q, k, vt, o_gate, gates_t, bias, head_gain0, ct_zero, m_zero, batch=1,
                                          chunks_per_step=1, group=1, name="mlstm_core_meta")

    o_gate, vt, gates_t, q, k, w_out0, w_up0, w_down0 = _norm_proj(
        h, tm=PROJ_TM, name="mlstm_in_proj",
        to_cast=(_cast_job(mlstm_w_out, 0), _cast_job(mlp_w_up, 0), _cast_job(mlp_w_down, 0)), **in0)
    mlp0 = dict(w_out=w_out0, gain=norm_gains[0, 1][None], w_up=w_up0, w_down=w_down0,
                final_gain=final_gain[None], final_norm=False)
    h_meta, = _post_mlp(h_meta, y_meta, tm=META_ROWS, name="mlstm_out_mlp_meta", **mlp0)
    y, _, _ = _mlstm_core(q, k, vt, o_gate, gates_t, bias, head_gain0, ct_meta[0, 0], m_meta[0, 0], batch=batch,
                          chunks_per_step=MLSTM_CHUNKS_PER_STEP, group=MLSTM_GROUP, name="mlstm_core")
    layer1_jobs = (_cast_job(diff_w_in, 0, 0, 3), _cast_job(diff_w_in, 0, 1, 3),
                   _cast_job(diff_w_in, 0, 2, 3, transpose=True),
                   _cast_job(diff_w_out, 0), _cast_job(mlp_w_up, 1), _cast_job(mlp_w_down, 1))
    h, w_q1, w_k1, w_vt1, w_out1, w_up1, w_down1 = _post_mlp(h, y, tm=MLP_TM, name="mlstm_out_mlp",
                                                             to_cast=layer1_jobs, **mlp0)

    lambda_init = 0.8 - 0.6 * math.exp(-0.3 * 1)
    in1 = dict(gain=norm_gains[1, 0][None], weights=[w_q1, w_k1, w_vt1], scales=(DQK ** -0.5 * LOG2E, 1.0, 1.0),
               transposed=(False, False, True), sigmoid=(False,) * 3, dtypes=(BF16, BF16, BF16))
    _, k_meta, vt_meta = _norm_proj(h_meta, tm=META_ROWS, name="diff_in_proj_meta", **in1)
    k_meta = k_meta[META_ROWS - N_META:]
    vt_meta = vt_meta[:, META_ROWS - N_META:]
    q, k, vt = _norm_proj(h, tm=PROJ_TM, name="diff_in_proj", **in1)
    y = _diff_attention(q, k, vt, k_meta, vt_meta, diff_lambda[0], diff_head_gain[0][None], batch, lambda_init)
    out, = _post_mlp(h, y, w_out1, norm_gains[1, 1][None], w_up1, w_down1, final_gain[None], final_norm=True,
                     tm=MLP_TM, name="diff_out_mlp")
    return out.reshape(batch, seq, D_MODEL)
```

```python
import functools
import math

import jax
import jax.numpy as jnp
from jax import lax
from jax.experimental import pallas as pl
from jax.experimental.pallas import tpu as pltpu

D_MODEL = 1024
D_FF = 4 * D_MODEL
N_META = 16
CHUNK = 64
N_HEADS = 8
DQK = 64
DV = 128
EPS = 1e-6
SUBLN_EPS = 1e-5
LOG2E = math.log2(math.e)

LANES = 128
BF16_ROWS = 16
AUG_ROWS = BF16_ROWS
VAUG_ROWS = DV + AUG_ROWS
GATE_ROWS = 2 * N_HEADS

MCHUNK = 256
MLSTM_CHUNKS_PER_STEP = 8
MLSTM_GROUP = 1
MLSTM_AHEAD = 6
META_ROWS = MCHUNK
ATT_TQ = 1024
ATT_TK = 256
ATT_AHEAD = 5

PROJ_TM = 1024
PROJ_SUB = 512
MLP_TM = 1024
MLP_FF_BLOCK = 512

VMEM_LIMIT = 56 * 1024 * 1024

F32 = jnp.float32
BF16 = jnp.bfloat16
NT_DIMS = (((1,), (1,)), ((), ()))


def _compiler_params(semantics):
    return pltpu.CompilerParams(dimension_semantics=semantics, vmem_limit_bytes=VMEM_LIMIT)


def _resident(shape):
    return pl.BlockSpec(shape, lambda *_: (0,) * len(shape), pipeline_mode=pl.Buffered(1))


def _ones_rows(width):
    return (lax.broadcasted_iota(jnp.int32, (AUG_ROWS, width), 0) == 0).astype(BF16)


def _cast_job(stacked, layer, col_block=0, n_col_blocks=1, transpose=False):
    return dict(array=stacked, layer=layer, col_block=col_block, n_col_blocks=n_col_blocks, transpose=transpose)


def _cast_specs(jobs, steps):
    in_specs, out_specs, out_shapes = [], [], []
    for job in jobs:
        _, n_rows, n_cols = job["array"].shape
        cols = n_cols // job["n_col_blocks"]
        layer, col_block = job["layer"], job["col_block"]
        if job["transpose"]:
            per_slab = steps * LANES // n_rows
            assert per_slab * n_rows == steps * LANES
            in_specs.append(pl.BlockSpec((None, LANES, cols),
                                         lambda i, l=layer, c=col_block, p=per_slab: (l, i // p, c)))
            out_specs.append(pl.BlockSpec((cols, LANES), lambda i, p=per_slab: (0, i // p)))
            out_shapes.append(jax.ShapeDtypeStruct((cols, n_rows), BF16))
        else:
            rows = n_rows // steps
            assert rows * steps == n_rows and rows % BF16_ROWS == 0
            in_specs.append(pl.BlockSpec((None, rows, cols), lambda i, l=layer, c=col_block: (l, i, c)))
            out_specs.append(pl.BlockSpec((rows, cols), lambda i: (i, 0)))
            out_shapes.append(jax.ShapeDtypeStruct((n_rows, cols), BF16))
    return in_specs, out_specs, out_shapes


def _cast_slabs(src_refs, dst_refs, transposed):
    for src, dst, tr in zip(src_refs, dst_refs, transposed):
        x = src[...]
        dst[...] = (x.T if tr else x).astype(dst.dtype)


def _norm_proj_kernel(x_ref, g_ref, *refs, scales, transposed, sigmoid, pieces, cast_transposed):
    n_w, n_cast, n_out = len(scales), len(cast_transposed), sum(len(p) for p in pieces)
    w_refs, out_refs = refs[:n_w], list(refs[n_w + n_cast:n_w + n_cast + n_out])
    _cast_slabs(refs[n_w:n_w + n_cast], refs[n_w + n_cast + n_out:], cast_transposed)
    tm = x_ref.shape[0]
    sub = min(PROJ_SUB, tm)
    for r0 in range(0, tm, sub):
        rows = slice(r0, r0 + sub)
        x = x_ref[rows, :]
        ms = jnp.mean(x * x, axis=-1, keepdims=True)
        xn = (x * lax.rsqrt(ms + EPS) * g_ref[...]).astype(BF16)
        outs = iter(out_refs)
        for w_ref, sc, tr, sg, sizes in zip(w_refs, scales, transposed, sigmoid, pieces):
            if tr:
                r = lax.dot_general(w_ref[...], xn, NT_DIMS, preferred_element_type=F32)
            else:
                r = jnp.dot(xn, w_ref[...], preferred_element_type=F32)
            if sc != 1.0:
                r = r * sc
            if sg:
                r = 0.5 * jnp.tanh(0.5 * r) + 0.5
            if tr:
                off = 0
                for n in sizes:
                    o_ref = next(outs)
                    o_ref[:, rows] = r[off:off + n].astype(o_ref.dtype)
                    off += n
            else:
                o_ref = next(outs)
                o_ref[rows, :] = r.astype(o_ref.dtype)


def _norm_proj(h, gain, weights, scales, transposed, sigmoid, dtypes, tm, name, to_cast=()):
    t = h.shape[0]
    assert t % tm == 0 and tm % min(PROJ_SUB, tm) == 0
    cast_in, cast_out, cast_shapes = _cast_specs(to_cast, t // tm)
    w_specs, out_specs, out_shapes, pieces = [], [], [], []
    for w, tr, dt in zip(weights, transposed, dtypes):
        w_specs.append(_resident(w.shape))
        if tr:
            parts = dt if isinstance(dt, tuple) else ((w.shape[0], dt),)
            assert sum(n for n, _ in parts) == w.shape[0]
            pieces.append(tuple(n for n, _ in parts))
            for n, part_dt in parts:
                out_specs.append(pl.BlockSpec((n, tm), lambda i: (0, i)))
                out_shapes.append(jax.ShapeDtypeStruct((n, t), part_dt))
        else:
            n = w.shape[1]
            pieces.append((n,))
            out_specs.append(pl.BlockSpec((tm, n), lambda i: (i, 0)))
            out_shapes.append(jax.ShapeDtypeStruct((t, n), dt))
    return pl.pallas_call(
        functools.partial(_norm_proj_kernel, scales=scales, transposed=transposed, sigmoid=sigmoid,
                          pieces=tuple(pieces), cast_transposed=tuple(job["transpose"] for job in to_cast)),
        grid=(t // tm,),
        in_specs=[pl.BlockSpec((tm, D_MODEL), lambda i: (i, 0)), _resident((1, D_MODEL))] + w_specs + cast_in,
        out_specs=out_specs + cast_out,
        out_shape=out_shapes + cast_shapes,
        compiler_params=_compiler_params(("arbitrary",)),
        name=name,
    )(h, gain, *weights, *[job["array"] for job in to_cast])


def _mlstm_kernel(*refs, chunks_per_step, group):
    q_ref, k_ref, o_ref = refs[0:3]
    vt_refs = refs[3:3 + group]
    gt_refs = refs[3 + group:3 + 2 * group]
    (bias_ref, gain_ref, ct0_ref, m0_ref, y_ref, ct_out_ref, m_out_ref,
     ct_ref, m_ref, b_ref, r_ref, cm_ref, rt_ref) = refs[3 + 2 * group:]
    t = MCHUNK

    @pl.when(pl.program_id(1) == 0)
    def _():
        for g in range(group):
            ct_ref[g] = ct0_ref[...]
            m_ref[g] = m0_ref[...]

    neg_inf = jnp.float32(-jnp.inf)
    causal = (lax.broadcasted_iota(jnp.int32, (t, t), 0)
              <= lax.broadcasted_iota(jnp.int32, (t, t), 1))
    upper = causal.astype(BF16)
    lane_t = lax.broadcasted_iota(jnp.int32, (N_HEADS, t), 1)
    bias = bias_ref[...]

    for g in range(group):
        for c in range(chunks_per_step):
            gates = gt_refs[g][:, c * t:(c + 1) * t] + bias
            ig = gates[0:N_HEADS]
            lf = jax.nn.log_sigmoid(gates[N_HEADS:])
            hi = lf.astype(BF16).astype(F32)
            mid = (lf - hi).astype(BF16).astype(F32)
            lo = (lf - hi - mid).astype(BF16).astype(F32)
            parts = jnp.dot(jnp.concatenate([hi, mid, lo], axis=0).astype(BF16), upper,
                            preferred_element_type=F32)
            b = parts[0:N_HEADS] + parts[N_HEADS:2 * N_HEADS] + parts[2 * N_HEADS:]
            r = ig - b
            cm = r
            shift = 1
            while shift < t:
                cm = jnp.maximum(cm, jnp.where(lane_t >= shift, pltpu.roll(cm, shift, 1), neg_inf))
                shift *= 2
            b_ref[g, c] = b
            r_ref[g, c] = r
            cm_ref[g, c] = cm
            rt_ref[g, c] = (r * LOG2E).T

    low = lax.broadcasted_iota(jnp.int32, (t, LANES), 1) < DQK
    low_state = lax.broadcasted_iota(jnp.int32, (VAUG_ROWS, LANES), 1) < DQK
    ones_rows = _ones_rows(t)
    half = t // 2
    tri = causal[:half, :half]

    def chunk_body(c, carry):
        rows = pl.ds(pl.multiple_of(c * t, t), t)
        row_terms = []
        for g in range(group):
            b, r, cm = b_ref[g, c], r_ref[g, c], cm_ref[g, c]
            m_prev = m_ref[g, :, 0:1]
            u = jnp.maximum(m_prev, cm)
            b_end = b[:, t - 1:t]
            a = b_end + r
            m_new = jnp.maximum(b_end + m_prev, jnp.max(a, axis=1, keepdims=True))
            row_terms.append(dict(
                inter=jnp.exp(m_prev - u), e_row=jnp.exp(-(b + u)), u2=u * LOG2E,
                decay=jnp.exp(b_end + m_prev - m_new), w=jnp.exp(a - m_new).astype(BF16), rt=rt_ref[g, c]))
            m_ref[g] = jnp.broadcast_to(m_new, m_ref.shape[1:])

        def qk_tiles(g, h):
            pair = slice(h // 2 * LANES, (h // 2 + 1) * LANES)
            return q_ref[g, rows, pair], k_ref[g, rows, pair]

        def head_scores(g, h):
            q2, k2 = qk_tiles(g, h)
            q_own = jnp.where(low if h % 2 == 0 else jnp.logical_not(low), q2, jnp.zeros_like(q2))
            return lax.dot_general(k2, q_own, NT_DIMS, preferred_element_type=F32)

        seq = [(g, h) for h in range(N_HEADS) for g in range(group)]
        pending = [head_scores(*u) for u in seq[:MLSTM_AHEAD]]
        for n, (g, h) in enumerate(seq):
            st_h = pending.pop(0)
            terms = row_terms[g]
            q2, k2 = qk_tiles(g, h)
            ct = ct_ref[g, h]
            vaug = jnp.concatenate([vt_refs[g][h * DV:(h + 1) * DV, rows], ones_rows], axis=0)
            inter_num = lax.dot_general(ct.astype(BF16), q2, NT_DIMS, preferred_element_type=F32)
            upd = jnp.dot(vaug * terms["w"][h:h + 1, :], k2, preferred_element_type=F32)
            own = low_state if h % 2 == 0 else jnp.logical_not(low_state)
            ct_ref[g, h] = terms["decay"][h:h + 1, :] * ct + jnp.where(own, upd, 0.0)
            if n + MLSTM_AHEAD < len(seq):
                pending.append(head_scores(*seq[n + MLSTM_AHEAD]))
            cols = slice(h * DV, (h + 1) * DV)
            r_col, u_row = terms["rt"][:, h:h + 1], terms["u2"][h:h + 1, :]
            d00 = jnp.exp2(jnp.where(tri, r_col[:half] - u_row[:, :half], neg_inf))
            d01 = jnp.exp2(r_col[:half] - u_row[:, half:])
            d11 = jnp.exp2(jnp.where(tri, r_col[half:] - u_row[:, half:], neg_inf))
            s_h = jnp.concatenate(
                [jnp.concatenate([st_h[:half, :half] * d00, st_h[:half, half:] * d01], axis=1),
                 jnp.concatenate([jnp.zeros((half, half), F32), st_h[half:, half:] * d11], axis=1)],
                axis=0).astype(BF16)
            num = (terms["inter"][h:h + 1, :] * inter_num
                   + jnp.dot(vaug, s_h, preferred_element_type=F32))
            inv = 1.0 / jnp.maximum(jnp.abs(num[DV:DV + 1, :]), terms["e_row"][h:h + 1, :])
            hh = num[0:DV, :] * inv
            hh = hh * lax.rsqrt(jnp.mean(hh * hh, axis=0, keepdims=True) + EPS)
            y_ref[g, rows, cols] = (hh.T * gain_ref[:, cols]).astype(BF16) * o_ref[g, rows, cols]
        return carry

    lax.fori_loop(0, chunks_per_step, chunk_body, 0)
    ct_out_ref[...] = ct_ref[...]
    m_out_ref[...] = m_ref[...]


def _mlstm_core(q, k, vt, o, gates_t, bias, gain, ct0, m0, batch, chunks_per_step, group, name):
    tokens = q.shape[0]
    rows = chunks_per_step * MCHUNK
    per_group = batch // group
    steps = tokens // batch // rows
    assert per_group * group == batch and batch * steps * rows == tokens
    ct_shape = (N_HEADS, VAUG_ROWS, LANES)
    m_shape = (N_HEADS, LANES)
    gate_shape = (group, chunks_per_step, N_HEADS, MCHUNK)

    def tok(width):
        return pl.BlockSpec((group, rows, width), lambda b, j: (0, b * steps + j, 0))

    def tok_t(height, g):
        return pl.BlockSpec((height, rows), lambda b, j, g=g: (0, (g * per_group + b) * steps + j))

    def grouped(x):
        return x.reshape(group, tokens // group, x.shape[1])

    y, ct, m = pl.pallas_call(
        functools.partial(_mlstm_kernel, chunks_per_step=chunks_per_step, group=group),
        grid=(per_group, steps),
        in_specs=([tok(N_HEADS * DQK), tok(N_HEADS * DQK), tok(N_HEADS * DV)]
                  + [tok_t(N_HEADS * DV, g) for g in range(group)]
                  + [tok_t(GATE_ROWS, g) for g in range(group)]
                  + [_resident((2 * N_HEADS, 1)), _resident((1, N_HEADS * DV)),
                     _resident(ct_shape), _resident(m_shape)]),
        out_specs=[tok(N_HEADS * DV),
                   pl.BlockSpec((group, None) + ct_shape, lambda b, j: (0, b, 0, 0, 0)),
                   pl.BlockSpec((group, None) + m_shape, lambda b, j: (0, b, 0, 0))],
        out_shape=[jax.ShapeDtypeStruct((group, tokens // group, N_HEADS * DV), BF16),
                   jax.ShapeDtypeStruct((group, per_group) + ct_shape, F32),
                   jax.ShapeDtypeStruct((group, per_group) + m_shape, F32)],
        scratch_shapes=[pltpu.VMEM((group,) + ct_shape, F32), pltpu.VMEM((group,) + m_shape, F32),
                        pltpu.VMEM(gate_shape, F32), pltpu.VMEM(gate_shape, F32), pltpu.VMEM(gate_shape, F32),
                        pltpu.VMEM((group, chunks_per_step, MCHUNK, N_HEADS), F32)],
        compiler_params=_compiler_params(("parallel", "arbitrary")),
        name=name,
    )(grouped(q), grouped(k), grouped(o), *([vt] * group), *([gates_t] * group), bias, gain, ct0, m0)
    return y.reshape(tokens, N_HEADS * DV), ct, m


def _diff_attn_kernel(lam_ref, q_ref, k_ref, vt_ref, km_ref, vtm_ref, gain_ref, o_ref,
                      qs_ref, m_ref, acc_ref, sc_ref, *, lambda_init):
    qi = pl.program_id(1)
    tq, tk = ATT_TQ, ATT_TK
    blocks = tq // tk
    neg_inf = jnp.float32(-jnp.inf)

    low = lax.broadcasted_iota(jnp.int32, (tq, LANES), 1) < DQK
    for h in range(N_HEADS):
        q2 = q_ref[:, h * DV:(h + 1) * DV]
        zero = jnp.zeros_like(q2)
        qs_ref[h] = jnp.concatenate([jnp.where(low, q2, zero), jnp.where(low, zero, q2)], axis=0)

    all_units = [(h, br, blk) for h in range(N_HEADS) for br in range(2) for blk in range(blocks)]

    def unit_cols(br, blk):
        return pl.ds((br * blocks + blk) * tk, tk)

    def run(items, primed, next_items):
        stream = items + next_items

        def scores(item):
            tile, (h, br, blk) = item
            s = lax.dot_general(tile["k"](h), qs_ref[h, unit_cols(br, blk), :], NT_DIMS,
                                preferred_element_type=F32)
            bias = tile["bias"](blk)
            return s if bias is None else s + bias

        pending = [sc_ref[n] if primed else scores(stream[n]) for n in range(ATT_AHEAD)]
        for n, (tile, (h, br, blk)) in enumerate(items):
            s = pending.pop(0)
            ahead = n + ATT_AHEAD
            if ahead < len(items):
                pending.append(scores(stream[ahead]))
            elif ahead < len(stream):
                sc_ref[ahead - len(items)] = scores(stream[ahead])
            cols = unit_cols(br, blk)
            v_aug = jnp.concatenate([tile["vt"](h), tile["ones"]], axis=0)
            s_max = jnp.max(s, axis=0, keepdims=True)
            if tile["first"]:
                m_new = s_max
                acc_ref[h, :, cols] = jnp.dot(v_aug, jnp.exp2(s - m_new).astype(BF16), preferred_element_type=F32)
            else:
                m_old = m_ref[h, :, cols]
                m_new = jnp.maximum(m_old, s_max)
                alpha = jnp.exp2(m_old - m_new)
                p = jnp.exp2(s - m_new).astype(BF16)
                acc_ref[h, :, cols] = alpha * acc_ref[h, :, cols] + jnp.dot(v_aug, p, preferred_element_type=F32)
            m_ref[h, :, cols] = m_new
        assert not pending and len(items) >= ATT_AHEAD

    ones_tile = _ones_rows(tk)

    def frame_tile(start, bias_of):
        return dict(k=lambda h: k_ref[pl.ds(start, tk), h * DV:(h + 1) * DV],
                    vt=lambda h: vt_ref[h * DV:(h + 1) * DV, pl.ds(start, tk)],
                    ones=ones_tile, bias=bias_of, first=False)

    def full_items(start, units):
        tile = frame_tile(start, lambda blk: None)
        return [(tile, u) for u in units]

    chunk_shift = CHUNK.bit_length() - 1
    k_chunk = lax.shift_right_logical(lax.broadcasted_iota(jnp.int32, (tk, tk), 0), chunk_shift)
    q_chunk = lax.shift_right_logical(lax.broadcasted_iota(jnp.int32, (tk, tk), 1), chunk_shift)
    diag_bias = jnp.where(k_chunk <= q_chunk, 0.0, neg_inf)
    first_bias = jnp.concatenate([diag_bias, jnp.zeros((N_META, tk), F32)], axis=0)
    start0 = pl.multiple_of(qi * tq, tk)
    first_tile = dict(
        k=lambda h: jnp.concatenate([k_ref[pl.ds(start0, tk), h * DV:(h + 1) * DV],
                                     km_ref[:, h * DV:(h + 1) * DV]], axis=0),
        vt=lambda h: jnp.concatenate([vt_ref[h * DV:(h + 1) * DV, pl.ds(start0, tk)],
                                      vtm_ref[h * DV:(h + 1) * DV, :]], axis=1),
        ones=_ones_rows(tk + N_META), bias=lambda blk: first_bias if blk == 0 else None, first=True)
    items = [(first_tile, u) for u in all_units]
    for d in range(1, blocks):
        tile = frame_tile(pl.multiple_of(qi * tq + d * tk, tk), lambda blk, d=d: diag_bias if blk == d else None)
        items += [(tile, u) for u in all_units if u[2] >= d]
    head_units = all_units[:ATT_AHEAD]
    run(items, primed=False, next_items=full_items(0, head_units))

    n_full = qi * blocks
    last_tile = k_ref.shape[0] // tk - 1

    def full_tile(j, carry):
        nxt = pl.multiple_of(jnp.minimum(j + 1, last_tile) * tk, tk)
        run(full_items(pl.multiple_of(j * tk, tk), all_units), primed=True,
            next_items=full_items(nxt, head_units))
        return carry

    lax.fori_loop(0, n_full, full_tile, 0)

    lv = lam_ref[...]
    lam = (jnp.exp(jnp.sum(lv[0:1] * lv[1:2], axis=-1, keepdims=True))
           - jnp.exp(jnp.sum(lv[2:3] * lv[3:4], axis=-1, keepdims=True)) + lambda_init)
    out_gain = gain_ref[...] * (1.0 - lambda_init)
    for h in range(N_HEADS):
        cols = slice(h * DV, (h + 1) * DV)
        acc = acc_ref[h]
        inv_l = 1.0 / acc[DV:DV + 1, :]
        o_t = acc[0:DV, :tq] * inv_l[:, :tq] - lam * (acc[0:DV, tq:] * inv_l[:, tq:])
        o_t = o_t * lax.rsqrt(jnp.mean(o_t * o_t, axis=0, keepdims=True) + SUBLN_EPS)
        o_ref[:, cols] = (o_t.T * out_gain[:, cols]).astype(o_ref.dtype)


def _diff_attention(q, k, vt, k_meta, vt_meta, lam_vecs, gain, batch, lambda_init):
    t = q.shape[0]
    seq = t // batch
    nq = seq // ATT_TQ
    assert nq * ATT_TQ == seq and ATT_TQ % ATT_TK == 0 and ATT_TK % CHUNK == 0
    assert k_meta.shape[0] == N_META and N_META % BF16_ROWS == 0
    width = N_HEADS * DV
    return pl.pallas_call(
        functools.partial(_diff_attn_kernel, lambda_init=lambda_init),
        grid=(batch, nq),
        in_specs=[_resident(lam_vecs.shape),
                  pl.BlockSpec((ATT_TQ, width), lambda b, i: (b * nq + i, 0)),
                  pl.BlockSpec((seq, width), lambda b, i: (b, 0)),
                  pl.BlockSpec((width, seq), lambda b, i: (0, b)),
                  _resident(k_meta.shape),
                  _resident(vt_meta.shape),
                  _resident((1, width))],
        out_specs=pl.BlockSpec((ATT_TQ, width), lambda b, i: (b * nq + i, 0)),
        out_shape=jax.ShapeDtypeStruct((t, width), BF16),
        scratch_shapes=[pltpu.VMEM((N_HEADS, 2 * ATT_TQ, DV), BF16),
                        pltpu.VMEM((N_HEADS, 1, 2 * ATT_TQ), F32),
                        pltpu.VMEM((N_HEADS, VAUG_ROWS, 2 * ATT_TQ), F32),
                        pltpu.VMEM((ATT_AHEAD, ATT_TK, ATT_TK), F32)],
        compiler_params=_compiler_params(("parallel", "arbitrary")),
        name="diff_attention",
    )(lam_vecs, q, k, vt, k_meta, vt_meta, gain)


def _post_mlp_kernel(h_ref, y_ref, wo_ref, g_ref, wu_ref, wd_ref, fg_ref, *refs, final_norm, ff_block,
                     cast_transposed):
    n_cast = len(cast_transposed)
    out_ref = refs[n_cast]
    _cast_slabs(refs[:n_cast], refs[n_cast + 1:], cast_transposed)
    h1 = h_ref[...] + jnp.dot(y_ref[...], wo_ref[...], preferred_element_type=F32)
    ms = jnp.mean(h1 * h1, axis=-1, keepdims=True)
    hn = (h1 * lax.rsqrt(ms + EPS) * g_ref[...]).astype(BF16)
    acc = h1
    for c in range(D_FF // ff_block):
        blk = slice(c * ff_block, (c + 1) * ff_block)
        u = jnp.dot(hn, wu_ref[:, blk], preferred_element_type=F32)
        a = jnp.square(jnp.maximum(u, 0.0)).astype(BF16)
        acc = acc + jnp.dot(a, wd_ref[blk, :], preferred_element_type=F32)
    if final_norm:
        ms = jnp.mean(acc * acc, axis=-1, keepdims=True)
        acc = acc * lax.rsqrt(ms + EPS) * fg_ref[...]
    out_ref[...] = acc


def _post_mlp(h, y, w_out, gain, w_up, w_down, final_gain, final_norm, tm, name, to_cast=()):
    t = h.shape[0]
    assert t % tm == 0
    cast_in, cast_out, cast_shapes = _cast_specs(to_cast, t // tm)
    return pl.pallas_call(
        functools.partial(_post_mlp_kernel, final_norm=final_norm, ff_block=MLP_FF_BLOCK,
                          cast_transposed=tuple(job["transpose"] for job in to_cast)),
        grid=(t // tm,),
        in_specs=[pl.BlockSpec((tm, D_MODEL), lambda i: (i, 0)),
                  pl.BlockSpec((tm, D_MODEL), lambda i: (i, 0)),
                  _resident((D_MODEL, D_MODEL)),
                  _resident((1, D_MODEL)),
                  _resident((D_MODEL, D_FF)),
                  _resident((D_FF, D_MODEL)),
                  _resident((1, D_MODEL))] + cast_in,
        out_specs=[pl.BlockSpec((tm, D_MODEL), lambda i: (i, 0))] + cast_out,
        out_shape=[jax.ShapeDtypeStruct((t, D_MODEL), F32)] + cast_shapes,
        compiler_params=_compiler_params(("arbitrary",)),
        name=name,
    )(h, y, w_out, gain, w_up, w_down, final_gain, *[job["array"] for job in to_cast])


def kernel(x, meta_tokens, norm_gains, mlstm_w_in, mlstm_b_gate, mlstm_head_gain, mlstm_w_out,
           diff_w_in, diff_lambda, diff_head_gain, diff_w_out, mlp_w_up, mlp_w_down, final_gain):
    batch, seq, _ = x.shape
    t = batch * seq
    h = x.reshape(t, D_MODEL)
    h_meta = jnp.concatenate([jnp.zeros((META_ROWS - N_META, D_MODEL), x.dtype), meta_tokens.astype(x.dtype)],
                             axis=0)

    m_qk = N_HEADS * DQK
    m_v = N_HEADS * DV
    w0 = mlstm_w_in[0]
    offs = (0, m_qk, 2 * m_qk, 2 * m_qk + m_v, 2 * m_qk + 2 * m_v)
    w0_parts = [w0[:, offs[3]:offs[4]],
                jnp.concatenate([w0[:, offs[2]:offs[3]], w0[:, offs[4]:]], axis=1).T,
                w0[:, offs[0]:offs[1]], w0[:, offs[1]:offs[2]]]
    w0_parts = [w.astype(BF16) for w in w0_parts]
    in0 = dict(gain=norm_gains[0, 0][None], weights=w0_parts, scales=(1.0, 1.0, DQK ** -0.5, 1.0),
               transposed=(False, True, False, False), sigmoid=(True, False, False, False),
               dtypes=(BF16, ((m_v, BF16), (GATE_ROWS, F32)), BF16, BF16))
    bias = mlstm_b_gate[0][:, None]
    head_gain0 = mlstm_head_gain[0][None]

    o_gate, vt, gates_t, q, k = _norm_proj(h_meta, tm=META_ROWS, name="mlstm_in_proj_meta", **in0)
    ct_zero = jnp.zeros((N_HEADS, VAUG_ROWS, LANES), F32)
    m_zero = jnp.zeros((N_HEADS, LANES), F32)
    y_meta, ct_meta, m_meta = _mlstm_core(q, k, vt, o_gate, gates_t, bias, head_gain0, ct_zero, m_zero, batch=1,
                                          chunks_per_step=1, group=1, name="mlstm_core_meta")

    o_gate, vt, gates_t, q, k, w_out0, w_up0, w_down0 = _norm_proj(
        h, tm=PROJ_TM, name="mlstm_in_proj",
        to_cast=(_cast_job(mlstm_w_out, 0), _cast_job(mlp_w_up, 0), _cast_job(mlp_w_down, 0)), **in0)
    mlp0 = dict(w_out=w_out0, gain=norm_gains[0, 1][None], w_up=w_up0, w_down=w_down0,
                final_gain=final_gain[None], final_norm=False)
    h_meta, = _post_mlp(h_meta, y_meta, tm=META_ROWS, name="mlstm_out_mlp_meta", **mlp0)
    y, _, _ = _mlstm_core(q, k, vt, o_gate, gates_t, bias, head_gain0, ct_meta[0, 0], m_meta[0, 0], batch=batch,
                          chunks_per_step=MLSTM_CHUNKS_PER_STEP, group=MLSTM_GROUP, name="mlstm_core")
    layer1_jobs = (_cast_job(diff_w_in, 0, 0, 3), _cast_job(diff_w_in, 0, 1, 3),
                   _cast_job(diff_w_in, 0, 2, 3, transpose=True),
                   _cast_job(diff_w_out, 0), _cast_job(mlp_w_up, 1), _cast_job(mlp_w_down, 1))
    h, w_q1, w_k1, w_vt1, w_out1, w_up1, w_down1 = _post_mlp(h, y, tm=MLP_TM, name="mlstm_out_mlp",
                                                             to_cast=layer1_jobs, **mlp0)

    lambda_init = 0.8 - 0.6 * math.exp(-0.3 * 1)
    in1 = dict(gain=norm_gains[1, 0][None], weights=[w_q1, w_k1, w_vt1], scales=(DQK ** -0.5 * LOG2E, 1.0, 1.0),
               transposed=(False, False, True), sigmoid=(False,) * 3, dtypes=(BF16, BF16, BF16))
    _, k_meta, vt_meta = _norm_proj(h_meta, tm=META_ROWS, name="diff_in_proj_meta", **in1)
    k_meta = k_meta[META_ROWS - N_META:]
    vt_meta = vt_meta[:, META_ROWS - N_META:]
    q, k, vt = _norm_proj(h, tm=PROJ_TM, name="diff_in_proj", **in1)
    y = _diff_attention(q, k, vt, k_meta, vt_meta, diff_lambda[0], diff_head_gain[0][None], batch, lambda_init)
    out, = _post_mlp(h, y, w_out1, norm_gains[1, 1][None], w_up1, w_down1, final_gain[None], final_norm=True,
                     tm=MLP_TM, name="diff_out_mlp")
    return out.reshape(batch, seq, D_MODEL)
```
